```python
import math
import jax, jax.numpy as jnp
from jax import lax
import numpy as np


D_MODEL = 1024
BATCH = 8
SEQ = 4096
DEPTH = 1

D_MIX = D_MODEL
D_MLSTM = D_MIX // 2
D_DIFF = D_MIX - D_MLSTM
ML_HEADS = 4
ML_HD = D_MLSTM // ML_HEADS
CONV_W = 4
CHUNK = 128
DA_HEADS = 4
DA_VD = D_DIFF // DA_HEADS
DA_QD = DA_VD // 2
Q_BLOCK = 128
REL_BUCKETS = 32
REL_MAX_DIST = 128
N_GROUPS = 4
EXP_PER_GROUP = 8
N_EXPERTS = N_GROUPS * EXP_PER_GROUP
TOP_K_FINE = 2
D_FF_EXP = 512
EPS = 1e-6
SUBLN_EPS = 1e-5
IN_COLS = 3 * D_MLSTM + 3 * D_DIFF

kernel_name = 'hybrid_mlstm_diffattn_hmoe'


def rmsnorm(x, g, eps=EPS):
    xf = x.astype(jnp.float32)
    y = xf * lax.rsqrt(jnp.mean(xf * xf, axis=-1, keepdims=True) + eps)
    return (y * g.astype(jnp.float32)).astype(x.dtype)


def head_layernorm(h, g):
    mu = jnp.mean(h, axis=-1, keepdims=True)
    var = jnp.mean(jnp.square(h - mu), axis=-1, keepdims=True)
    return (h - mu) * lax.rsqrt(var + EPS) * g.astype(jnp.float32)


def causal_conv(x, w, b):
    c = x.shape[-1]
    y = lax.conv_general_dilated(x, w[:, None, :].astype(x.dtype), window_strides=(1,),
                                 padding=[(CONV_W - 1, 0)],
                                 dimension_numbers=('NWC', 'WIO', 'NWC'),
                                 feature_group_count=c)
    return y + b.astype(x.dtype)


def mlstm_chunkwise(q, k, v, i_pre, logf):
    bsz, s, nh, d = q.shape
    nc = s // CHUNK
    def chunks(t):
        return t.reshape(bsz, nc, CHUNK, nh, d).transpose(1, 0, 3, 2, 4)
    def gchunks(t):
        return t.reshape(bsz, nc, CHUNK, nh).transpose(1, 0, 3, 2)
    tril = jnp.tril(jnp.ones((CHUNK, CHUNK), dtype=bool))

    def step(carry, xs):
        cmat, nvec, m = carry
        qc, kc, vc, ic, fc = xs
        b = jnp.cumsum(fc, axis=-1)
        inter = b + m[..., None]
        dmat = jnp.where(tril, b[..., :, None] - b[..., None, :] + ic[..., None, :], -jnp.inf)
        m_t = jnp.maximum(inter, jnp.max(dmat, axis=-1))
        w = jnp.exp(dmat - m_t[..., None]) * jnp.einsum('bhtd,bhsd->bhts', qc, kc)
        sp = jnp.exp(inter - m_t)
        num = sp[..., None] * jnp.einsum('bhtk,bhvk->bhtv', qc, cmat) + jnp.einsum('bhts,bhsv->bhtv', w, vc)
        den = sp * jnp.einsum('bhtk,bhk->bht', qc, nvec) + jnp.sum(w, axis=-1)
        h = num / jnp.maximum(jnp.abs(den), jnp.exp(-m_t))[..., None]
        b_end = b[..., -1]
        g = b_end[..., None] - b + ic
        m_new = jnp.maximum(b_end + m, jnp.max(g, axis=-1))
        wk = jnp.exp(g - m_new[..., None])
        decay = jnp.exp(b_end + m - m_new)
        c_new = decay[..., None, None] * cmat + jnp.einsum('bhs,bhsv,bhsk->bhvk', wk, vc, kc)
        n_new = decay[..., None] * nvec + jnp.einsum('bhs,bhsk->bhk', wk, kc)
        return (c_new, n_new, m_new), h

    init = (jnp.zeros((bsz, nh, d, d), jnp.float32), jnp.zeros((bsz, nh, d), jnp.float32),
            jnp.zeros((bsz, nh), jnp.float32))
    _, hs = lax.scan(step, init, (chunks(q), chunks(k), chunks(v), gchunks(i_pre), gchunks(logf)))
    return hs.transpose(1, 0, 3, 2, 4).reshape(bsz, s, nh, d)


def mlstm_group(c, v, z, conv_w, conv_b, w_mq, w_mk, w_mgate, b_mgate, m_norm_g, m_skip):
    bsz, s, _ = c.shape
    c_act = jax.nn.silu(causal_conv(c, conv_w, conv_b))
    ch = c_act.reshape(bsz, s, ML_HEADS, ML_HD)
    q = jnp.einsum('bshd,hde->bshe', ch, w_mq)
    k = jnp.einsum('bshd,hde->bshe', ch, w_mk)
    vh = v.reshape(bsz, s, ML_HEADS, ML_HD)
    gin = jnp.concatenate([q.reshape(bsz, s, -1), k.reshape(bsz, s, -1), v], axis=-1)
    gates = (gin @ w_mgate).astype(jnp.float32) + b_mgate.astype(jnp.float32)
    i_pre = gates[..., :ML_HEADS]
    logf = jax.nn.log_sigmoid(gates[..., ML_HEADS:])
    h = mlstm_chunkwise(q.astype(jnp.float32), (k * (ML_HD ** -0.5)).astype(jnp.float32),
                        vh.astype(jnp.float32), i_pre, logf)
    h = head_layernorm(h, m_norm_g.reshape(ML_HEADS, ML_HD)).reshape(bsz, s, D_MLSTM)
    h = h + m_skip.astype(jnp.float32) * c_act.astype(jnp.float32)
    o = jax.nn.sigmoid(z.astype(jnp.float32))
    return (o * h).astype(c.dtype)


def rel_bucket(qpos, kpos):
    n = jnp.maximum(qpos[:, None] - kpos[None, :], 0)
    max_exact = REL_BUCKETS // 2
    large = max_exact + (jnp.log(jnp.maximum(n, 1).astype(jnp.float32) / max_exact)
                         / math.log(REL_MAX_DIST / max_exact) * (REL_BUCKETS - max_exact)).astype(jnp.int32)
    large = jnp.minimum(large, REL_BUCKETS - 1)
    return jnp.where(n < max_exact, n, large)


def diff_attention(q, k, v, lam, rel_bias):
    bsz, s = q.shape[0], q.shape[1]
    nb = s // Q_BLOCK
    scale = DA_QD ** -0.5
    qb = jnp.moveaxis(q.reshape(bsz, nb, Q_BLOCK, DA_HEADS, 2, DA_QD), 1, 0)
    kpos = jnp.arange(s)

    def block(args):
        q_blk, bi = args
        qpos = bi * Q_BLOCK + jnp.arange(Q_BLOCK)
        bias = rel_bias[rel_bucket(qpos, kpos)].transpose(2, 0, 1).astype(jnp.float32)
        mask = kpos[None, :] <= qpos[:, None]
        logits = jnp.einsum('bqhcd,bkhcd->bchqk', q_blk, k).astype(jnp.float32) * scale + bias[None, None]
        p = jax.nn.softmax(jnp.where(mask, logits, -jnp.inf), axis=-1)
        a = p[:, 0] - lam * p[:, 1]
        return jnp.einsum('bhqk,bkhd->bqhd', a.astype(v.dtype), v)

    out = lax.map(block, (qb, jnp.arange(nb)))
    return jnp.moveaxis(out, 0, 1).reshape(bsz, s, DA_HEADS, DA_VD)


def hier_moe(h, w_rg, b_rg, w_re, b_re, w_eg, w_eu, w_ed):
    bsz, s, d = h.shape
    t = h.reshape(-1, d)
    lg = (t @ w_rg).astype(jnp.float32) + b_rg.astype(jnp.float32)
    pg = jax.nn.softmax(lg, axis=-1)
    _, g_idx = lax.top_k(lg, 1)
    pg_sel = jnp.take_along_axis(pg, g_idx, axis=-1)
    le = jnp.einsum('nd,dge->nge', t, w_re).astype(jnp.float32) + b_re.astype(jnp.float32)
    le_sel = jnp.take_along_axis(le, g_idx[:, :, None], axis=1)[:, 0]
    top_v, top_i = lax.top_k(le_sel, TOP_K_FINE)
    pw = jax.nn.softmax(top_v, axis=-1)
    eid = g_idx * EXP_PER_GROUP + top_i
    gates = jnp.sum(jax.nn.one_hot(eid, N_EXPERTS, dtype=jnp.float32) * (pg_sel * pw)[..., None], axis=1)
    gates = gates.astype(t.dtype)
    out = jnp.zeros_like(t)
    for e in range(N_EXPERTS):
        hid = jax.nn.silu(t @ w_eg[e]) * (t @ w_eu[e])
        out = out + gates[:, e:e + 1] * (hid @ w_ed[e])
    return out.reshape(bsz, s, d)


def setup_inputs(seed: int = 0) -> dict:
    key = jax.random.key(seed)
    ks = jax.random.split(key, 32)
    f32 = jnp.float32
    def nrm(k, shape, sc):
        return jax.random.normal(k, shape, f32) * sc
    L = DEPTH
    b_i = nrm(ks[7], (L, ML_HEADS), 0.1)
    b_f = jnp.linspace(3.0, 6.0, ML_HEADS, dtype=f32)[None, :] + nrm(ks[8], (L, ML_HEADS), 0.01)
    return {
        'x': nrm(ks[0], (BATCH, SEQ, D_MODEL), 1.0),
        'w_in': nrm(ks[1], (L, D_MODEL, IN_COLS), D_MODEL ** -0.5),
        'conv_w': nrm(ks[2], (L, CONV_W, D_MLSTM), CONV_W ** -0.5),
        'conv_b': nrm(ks[3], (L, D_MLSTM), 0.01),
        'w_mq': nrm(ks[4], (L, ML_HEADS, ML_HD, ML_HD), ML_HD ** -0.5),
        'w_mk': nrm(ks[5], (L, ML_HEADS, ML_HD, ML_HD), ML_HD ** -0.5),
        'w_mgate': nrm(ks[6], (L, 3 * D_MLSTM, 2 * ML_HEADS), 0.1 * (3 * D_MLSTM) ** -0.5),
        'b_mgate': jnp.concatenate([b_i, b_f], axis=-1),
        'm_norm_g': 1.0 + nrm(ks[9], (L, D_MLSTM), 0.02),
        'm_skip': 1.0 + nrm(ks[10], (L, D_MLSTM), 0.02),
        'lambda_qk': nrm(ks[11], (L, 4, DA_QD), 0.1),
        'da_norm_g': 1.0 + nrm(ks[12], (L, DA_VD), 0.02),
        'rel_bias': nrm(ks[13], (REL_BUCKETS, DA_HEADS), 0.2),
        'w_out': nrm(ks[14], (L, D_MIX, D_MODEL), D_MIX ** -0.5),
        'norm1_g': 1.0 + nrm(ks[15], (L, D_MODEL), 0.02),
        'norm2_g': 1.0 + nrm(ks[16], (L, D_MODEL), 0.02),
        'w_rg': nrm(ks[17], (L, D_MODEL, N_GROUPS), D_MODEL ** -0.5),
        'b_rg': nrm(ks[18], (L, N_GROUPS), 0.01),
        'w_re': nrm(ks[19], (L, D_MODEL, N_GROUPS, EXP_PER_GROUP), D_MODEL ** -0.5),
        'b_re': nrm(ks[20], (L, N_GROUPS, EXP_PER_GROUP), 0.01),
        'w_eg': nrm(ks[21], (L, N_EXPERTS, D_MODEL, D_FF_EXP), D_MODEL ** -0.5),
        'w_eu': nrm(ks[22], (L, N_EXPERTS, D_MODEL, D_FF_EXP), D_MODEL ** -0.5),
        'w_ed': nrm(ks[23], (L, N_EXPERTS, D_FF_EXP, D_MODEL), D_FF_EXP ** -0.5),
        'normf_g': 1.0 + nrm(ks[24], (D_MODEL,), 0.02),
    }


def reference(x, w_in, conv_w, conv_b, w_mq, w_mk, w_mgate, b_mgate, m_norm_g, m_skip,
              lambda_qk, da_norm_g, rel_bias, w_out, norm1_g, norm2_g, w_rg, b_rg, w_re, b_re,
              w_eg, w_eu, w_ed, normf_g):
    bsz, s, _ = x.shape
    splits = [D_MLSTM, 2 * D_MLSTM, 3 * D_MLSTM, 3 * D_MLSTM + D_DIFF, 3 * D_MLSTM + 2 * D_DIFF]
    for l in range(DEPTH):
        lam_init = 0.8 - 0.6 * math.exp(-0.3 * l)
        h = rmsnorm(x, norm1_g[l])
        proj = h @ w_in[l]
        c, vm, z, qd, kd, vd = jnp.split(proj, splits, axis=-1)
        y_m = mlstm_group(c, vm, z, conv_w[l], conv_b[l], w_mq[l], w_mk[l], w_mgate[l], b_mgate[l],
                          m_norm_g[l], m_skip[l])
        lq = lambda_qk[l].astype(jnp.float32)
        lam = jnp.exp(jnp.dot(lq[0], lq[1])) - jnp.exp(jnp.dot(lq[2], lq[3])) + lam_init
        y_d = diff_attention(qd.reshape(bsz, s, DA_HEADS, 2, DA_QD), kd.reshape(bsz, s, DA_HEADS, 2, DA_QD),
                             vd.reshape(bsz, s, DA_HEADS, DA_VD), lam, rel_bias)
        y_d = (rmsnorm(y_d, da_norm_g[l], SUBLN_EPS) * (1.0 - lam_init)).reshape(bsz, s, D_DIFF)
        x = x + jnp.concatenate([y_m, y_d.astype(x.dtype)], axis=-1) @ w_out[l]
        h2 = rmsnorm(x, norm2_g[l])
        x = x + hier_moe(h2, w_rg[l], b_rg[l], w_re[l], b_re[l], w_eg[l], w_eu[l], w_ed[l])
    return rmsnorm(x, normf_g)
```

```python
import functools
import math

import jax
import jax.numpy as jnp
from jax import lax
from jax.experimental import pallas as pl
from jax.experimental.pallas import tpu as pltpu

F32 = jnp.float32
BF16 = jnp.bfloat16

D_MODEL = 1024
D_MLSTM = 512
D_DIFF = 512
ML_HEADS = 4
ML_HD = 128
CONV_W = 4
CHUNK = 128
DA_HEADS = 4
DA_VD = 128
DA_QD = 64
REL_BUCKETS = 32
REL_MAX_DIST = 128
N_GROUPS = 4
EXP_PER_GROUP = 8
N_EXPERTS = 32
D_FF_EXP = 512
EPS = 1e-6
SUBLN_EPS = 1e-5
LAM_INIT = 0.8 - 0.6 * math.exp(-0.3 * 0)
NEG = -1e30

LANES = 128
SUBLANES = 8
MXU_COLS = 256

TM_PROJ = 512
TQ = 256
TM_MOE = 256
TT = 512
VMEM_LIMIT = 48 * 1024 * 1024


def _cparams(sem):
    return pltpu.CompilerParams(dimension_semantics=sem, vmem_limit_bytes=VMEM_LIMIT)


def _inproj_kernel(x_ref, g_ref, w_ref, c_ref, vm_ref, z_ref, q_ref, k_ref, v_ref):
    x = x_ref[...]
    ms = jnp.mean(x * x, axis=-1, keepdims=True)
    h = (x * lax.rsqrt(ms + EPS) * g_ref[...]).astype(BF16)
    outs = (c_ref, vm_ref, z_ref, q_ref, k_ref, v_ref)
    for j, o in enumerate(outs):
        r = jnp.dot(h, w_ref[:, j * 512:(j + 1) * 512], preferred_element_type=F32)
        if o is q_ref:
            r = r * (DA_QD ** -0.5)
        o[...] = r.astype(o.dtype)


def _inproj(x2, g1, w_in):
    n = x2.shape[0]
    row = lambda i: (i, 0)
    fixed = lambda i: (0, 0)
    sds = lambda dt: jax.ShapeDtypeStruct((n, 512), dt)
    blk = pl.BlockSpec((TM_PROJ, 512), row)
    return pl.pallas_call(
        _inproj_kernel,
        grid=(n // TM_PROJ,),
        in_specs=[pl.BlockSpec((TM_PROJ, D_MODEL), row),
                  pl.BlockSpec((1, D_MODEL), fixed),
                  pl.BlockSpec((D_MODEL, 6 * 512), fixed)],
        out_specs=[blk] * 6,
        out_shape=[sds(F32), sds(BF16), sds(F32), sds(BF16), sds(BF16), sds(BF16)],
        compiler_params=_cparams(("parallel",)),
        name="inproj",
    )(x2, g1, w_in)


def _mlstm_kernel(c_ref, vm_ref, z_ref, cw_ref, cb_ref, wq_ref, wk_ref, wg_ref, bg_ref,
                  ng_ref, sk_ref, tril_ref, y_ref, ext_ref, st_ref, m_ref):
    L = CHUNK
    j = pl.program_id(1)

    @pl.when(j == 0)
    def _():
        ext_ref[0:SUBLANES, :] = jnp.zeros((SUBLANES, D_MLSTM), F32)
        st_ref[...] = jnp.zeros(st_ref.shape, F32)
        m_ref[...] = jnp.zeros(m_ref.shape, F32)

    ext_ref[SUBLANES:SUBLANES + L, :] = c_ref[...]
    cw = cw_ref[...]
    conv = cb_ref[...] + cw[CONV_W - 1:CONV_W, :] * ext_ref[SUBLANES:SUBLANES + L, :]
    for k in range(1, CONV_W):
        conv = conv + cw[CONV_W - 1 - k:CONV_W - k, :] * ext_ref[SUBLANES - k:SUBLANES - k + L, :]
    ext_ref[0:SUBLANES, :] = ext_ref[L:L + SUBLANES, :]
    c_act = conv * jax.nn.sigmoid(conv)

    vm = vm_ref[...]
    qs, ks = [], []
    gates = jnp.dot(vm, wg_ref[2 * D_MLSTM:3 * D_MLSTM, :], preferred_element_type=F32) + bg_ref[...]
    for h in range(ML_HEADS):
        ch = c_act[:, h * ML_HD:(h + 1) * ML_HD].astype(BF16)
        qh = jnp.dot(ch, wq_ref[h], preferred_element_type=F32)
        kh = jnp.dot(ch, wk_ref[h], preferred_element_type=F32)
        qs.append(qh)
        ks.append(kh)
        gates = gates + jnp.dot(qh.astype(BF16), wg_ref[h * ML_HD:(h + 1) * ML_HD, :],
                                preferred_element_type=F32)
        gates = gates + jnp.dot(kh.astype(BF16), wg_ref[D_MLSTM + h * ML_HD:D_MLSTM + (h + 1) * ML_HD, :],
                                preferred_element_type=F32)

    logf = jnp.minimum(gates, 0.0) - jnp.log(1.0 + jnp.exp(-jnp.abs(gates)))
    bcum = jnp.dot(tril_ref[...], logf, preferred_element_type=F32, precision=lax.Precision.HIGHEST)
    gates_t = gates.T
    bcum_t = bcum.T
    row_i = lax.broadcasted_iota(jnp.int32, (L, L), 0)
    col_i = lax.broadcasted_iota(jnp.int32, (L, L), 1)
    causal = col_i <= row_i
    ones_col = jnp.where(lax.broadcasted_iota(jnp.int32, (L, LANES), 1) == 0, 1.0, 0.0).astype(F32)

    for h in range(ML_HEADS):
        q = qs[h].astype(BF16)
        k = (ks[h] * (ML_HD ** -0.5)).astype(BF16)
        v = vm[:, h * ML_HD:(h + 1) * ML_HD]
        i_col = gates[:, h:h + 1]
        i_row = gates_t[h:h + 1, :]
        b_col = bcum[:, ML_HEADS + h:ML_HEADS + h + 1]
        b_row = bcum_t[ML_HEADS + h:ML_HEADS + h + 1, :]
        m_prev = m_ref[h][0:1, 0:1]
        state = st_ref[h]

        inter = b_col + m_prev
        dmat = jnp.where(causal, b_col - b_row + i_row, NEG)
        m_t = jnp.maximum(inter, jnp.max(dmat, axis=-1, keepdims=True))
        dexp = jnp.exp(dmat - m_t)
        s = lax.dot_general(q, k, (((1,), (1,)), ((), ())), preferred_element_type=F32)
        w = (dexp * s).astype(BF16)
        sp = jnp.exp(inter - m_t)
        vaug = jnp.concatenate([v, ones_col.astype(BF16)], axis=1)
        r_inter = jnp.dot(q, state.astype(BF16), preferred_element_type=F32)
        r_intra = jnp.dot(w, vaug, preferred_element_type=F32)
        tot = sp * r_inter + r_intra
        num = tot[:, :ML_HD]
        den = tot[:, ML_HD:ML_HD + 1]
        hh = num / jnp.maximum(jnp.abs(den), jnp.exp(-m_t))

        b_end = b_col[L - 1:L, :]
        g_col = b_end - b_col + i_col
        m_new = jnp.maximum(b_end + m_prev, jnp.max(g_col, axis=0, keepdims=True))
        wk = jnp.exp(g_col - m_new)
        decay = jnp.exp(b_end + m_prev - m_new)
        vw = (jnp.concatenate([v.astype(F32), ones_col], axis=1) * wk).astype(BF16)
        upd = lax.dot_general(k, vw, (((0,), (0,)), ((), ())), preferred_element_type=F32)
        st_ref[h] = decay * state + upd
        m_ref[h] = jnp.broadcast_to(m_new, (SUBLANES, LANES))

        mu = jnp.mean(hh, axis=-1, keepdims=True)
        xc = hh - mu
        var = jnp.mean(xc * xc, axis=-1, keepdims=True)
        sl = slice(h * ML_HD, (h + 1) * ML_HD)
        hn = xc * lax.rsqrt(var + EPS) * ng_ref[:, sl]
        hn = hn + sk_ref[:, sl] * c_act[:, sl]
        o = jax.nn.sigmoid(z_ref[:, sl])
        y_ref[:, sl] = (o * hn).astype(y_ref.dtype)


def _mlstm(c, vm, z, conv_w, conv_b, wq, wk, wg, bg, ng, sk, bsz, seq):
    nc = seq // CHUNK
    row = lambda b, j: (b * nc + j, 0)
    f2 = lambda b, j: (0, 0)
    f3 = lambda b, j: (0, 0, 0)
    tril = jnp.tril(jnp.ones((CHUNK, CHUNK), F32))
    blk = pl.BlockSpec((CHUNK, D_MLSTM), row)
    return pl.pallas_call(
        _mlstm_kernel,
        grid=(bsz, nc),
        in_specs=[blk, blk, blk,
                  pl.BlockSpec((CONV_W, D_MLSTM), f2),
                  pl.BlockSpec((1, D_MLSTM), f2),
                  pl.BlockSpec((ML_HEADS, ML_HD, ML_HD), f3),
                  pl.BlockSpec((ML_HEADS, ML_HD, ML_HD), f3),
                  pl.BlockSpec((3 * D_MLSTM, LANES), f2),
                  pl.BlockSpec((1, LANES), f2),
                  pl.BlockSpec((1, D_MLSTM), f2),
                  pl.BlockSpec((1, D_MLSTM), f2),
                  pl.BlockSpec((CHUNK, CHUNK), f2)],
        out_specs=blk,
        out_shape=jax.ShapeDtypeStruct((bsz * seq, D_MLSTM), BF16),
        scratch_shapes=[pltpu.VMEM((SUBLANES + CHUNK, D_MLSTM), F32),
                        pltpu.VMEM((ML_HEADS, ML_HD, 2 * ML_HD), F32),
                        pltpu.VMEM((ML_HEADS, SUBLANES, LANES), F32)],
        compiler_params=_cparams(("parallel", "arbitrary")),
        name="mlstm",
    )(c, vm, z, conv_w, conv_b, wq, wk, wg, bg, ng, sk, tril)


def _attn_kernel(lq_ref, q_ref, k_ref, v_ref, bias_ref, g_ref, o_ref, acc_ref, m_ref):
    tq = TQ
    i = pl.program_id(2)
    q = q_ref[...]
    lane = lax.broadcasted_iota(jnp.int32, (tq, LANES), 1)
    zero = jnp.zeros_like(q)
    q2 = jnp.concatenate([jnp.where(lane < DA_QD, q, zero), jnp.where(lane >= DA_QD, q, zero)], axis=0)
    e0 = jnp.where(lax.broadcasted_iota(jnp.int32, (tq, LANES), 1) == 0, 1.0, 0.0).astype(BF16)
    acc_ref[...] = jnp.zeros(acc_ref.shape, F32)
    m_ref[...] = jnp.full(m_ref.shape, NEG, F32)

    def step(kb, bias):
        start = pl.multiple_of(kb * tq, tq)
        k_blk = k_ref[pl.ds(start, tq), :]
        v_blk = v_ref[pl.ds(start, tq), :]
        s = lax.dot_general(q2, k_blk, (((1,), (1,)), ((), ())), preferred_element_type=F32)
        if bias is not None:
            s = s + jnp.concatenate([bias, bias], axis=0)
        m_old = m_ref[...]
        m_new = jnp.maximum(m_old, jnp.max(s, axis=-1, keepdims=True))
        alpha = jnp.exp(m_old - m_new)
        p = jnp.exp(s - m_new).astype(BF16)
        va = jnp.concatenate([v_blk, e0], axis=1)
        acc_ref[...] = alpha * acc_ref[...] + jnp.dot(p, va, preferred_element_type=F32)
        m_ref[...] = m_new

    def far(kb, carry):
        step(kb, None)
        return carry

    lax.fori_loop(0, i - 1, far, 0)

    @pl.when(i >= 1)
    def _():
        step(i - 1, bias_ref[1])

    step(i, bias_ref[0])

    acc = acc_ref[...]
    o0 = acc[:tq, :DA_VD] / acc[:tq, DA_VD:DA_VD + 1]
    o1 = acc[tq:, :DA_VD] / acc[tq:, DA_VD:DA_VD + 1]
    lq = lq_ref[...]
    d01 = jnp.sum(lq[0:1, :] * lq[1:2, :], axis=-1, keepdims=True)
    d23 = jnp.sum(lq[2:3, :] * lq[3:4, :], axis=-1, keepdims=True)
    lam = jnp.exp(d01) - jnp.exp(d23) + LAM_INIT
    y = o0 - lam * o1
    y = y * lax.rsqrt(jnp.mean(y * y, axis=-1, keepdims=True) + SUBLN_EPS) * g_ref[...]
    o_ref[...] = (y * (1.0 - LAM_INIT)).astype(o_ref.dtype)


def _rel_bias_tiles(rel_bias):
    r = jnp.arange(TQ)[:, None]
    c = jnp.arange(TQ)[None, :]
    max_exact = REL_BUCKETS // 2

    def bucket(n):
        n = jnp.maximum(n, 0)
        large = max_exact + (jnp.log(jnp.maximum(n, 1).astype(F32) / max_exact)
                             / math.log(REL_MAX_DIST / max_exact) * (REL_BUCKETS - max_exact)).astype(jnp.int32)
        large = jnp.minimum(large, REL_BUCKETS - 1)
        return jnp.where(n < max_exact, n, large)

    rb = rel_bias.astype(F32) - rel_bias[REL_BUCKETS - 1].astype(F32)[None, :]
    n0 = r - c
    t0 = jnp.where((n0 >= 0)[:, :, None], rb[bucket(n0)], NEG)
    t1 = rb[bucket(n0 + TQ)]
    return jnp.stack([t0, t1], axis=0).transpose(3, 0, 1, 2)


def _attn(lq, qd, kd, vd, bias, g, bsz, seq):
    nq = seq // TQ
    return pl.pallas_call(
        _attn_kernel,
        grid=(bsz, DA_HEADS, nq),
        in_specs=[pl.BlockSpec((4, DA_QD), lambda b, h, i: (0, 0)),
                  pl.BlockSpec((TQ, LANES), lambda b, h, i: (b * nq + i, h)),
                  pl.BlockSpec((seq, LANES), lambda b, h, i: (b, h)),
                  pl.BlockSpec((seq, LANES), lambda b, h, i: (b, h)),
                  pl.BlockSpec((None, 2, TQ, TQ), lambda b, h, i: (h, 0, 0, 0)),
                  pl.BlockSpec((1, DA_VD), lambda b, h, i: (0, 0))],
        out_specs=pl.BlockSpec((TQ, LANES), lambda b, h, i: (b * nq + i, h)),
        out_shape=jax.ShapeDtypeStruct((bsz * seq, D_DIFF), BF16),
        scratch_shapes=[pltpu.VMEM((2 * TQ, MXU_COLS), F32),
                        pltpu.VMEM((2 * TQ, 1), F32)],
        compiler_params=_cparams(("parallel", "parallel", "arbitrary")),
        name="attn",
    )(lq, qd, kd, vd, bias, g)


def _outproj_kernel(x_ref, ym_ref, yd_ref, wo_ref, g2_ref, wr_ref, br_ref, tri_ref,
                    x1_ref, h2_ref, meta_ref, gates_ref, cnt_ref, run_ref):
    tm = TM_PROJ
    step = pl.program_id(0)

    @pl.when(step == 0)
    def _():
        run_ref[...] = jnp.zeros(run_ref.shape, F32)

    x1 = (x_ref[...]
          + jnp.dot(ym_ref[...], wo_ref[0:D_MLSTM, :], preferred_element_type=F32)
          + jnp.dot(yd_ref[...], wo_ref[D_MLSTM:, :], preferred_element_type=F32))
    x1_ref[...] = x1
    ms = jnp.mean(x1 * x1, axis=-1, keepdims=True)
    h2 = x1 * lax.rsqrt(ms + EPS) * g2_ref[...]
    h2_ref[...] = h2

    h_hi = h2.astype(BF16)
    h_lo = (h2 - h_hi.astype(F32)).astype(BF16)
    lhs = jnp.concatenate([h_hi, h_lo, h_hi], axis=1)
    logits = jnp.dot(lhs, wr_ref[...], preferred_element_type=F32) + br_ref[...]
    lt = logits.T

    row8 = lax.broadcasted_iota(jnp.int32, (SUBLANES, tm), 0)
    lg = jnp.where(row8 < N_GROUPS, lt[0:SUBLANES], NEG)
    gmax = jnp.max(lg, axis=0, keepdims=True)
    gidx = jnp.min(jnp.where(lg == gmax, row8, SUBLANES), axis=0, keepdims=True)
    pg_sel = 1.0 / jnp.sum(jnp.exp(lg - gmax), axis=0, keepdims=True)
    le = jnp.zeros((EXP_PER_GROUP, tm), F32)
    for g in range(N_GROUPS):
        le = jnp.where(gidx == g, lt[SUBLANES * (g + 1):SUBLANES * (g + 2)], le)
    v1 = jnp.max(le, axis=0, keepdims=True)
    i1 = jnp.min(jnp.where(le == v1, row8, SUBLANES), axis=0, keepdims=True)
    le2 = jnp.where(row8 == i1, NEG, le)
    v2 = jnp.max(le2, axis=0, keepdims=True)
    i2 = jnp.min(jnp.where(le2 == v2, row8, SUBLANES), axis=0, keepdims=True)
    e21 = jnp.exp(v2 - v1)
    pw1 = 1.0 / (1.0 + e21)
    gate1 = pg_sel * pw1
    gate2 = pg_sel * (e21 * pw1)
    eid1 = gidx * EXP_PER_GROUP + i1
    eid2 = gidx * EXP_PER_GROUP + i2

    row32 = lax.broadcasted_iota(jnp.int32, (N_EXPERTS, tm), 0)
    oh1 = row32 == eid1
    oh2 = row32 == eid2
    oh = jnp.where(oh1, 1.0, 0.0) + jnp.where(oh2, 1.0, 0.0)
    before = jnp.dot(oh.astype(BF16), tri_ref[...], preferred_element_type=F32) + run_ref[:, 0:1]
    rank1 = jnp.sum(jnp.where(oh1, before, 0.0), axis=0, keepdims=True).astype(jnp.int32)
    rank2 = jnp.sum(jnp.where(oh2, before, 0.0), axis=0, keepdims=True).astype(jnp.int32)
    run = run_ref[...] + jnp.sum(oh, axis=1, keepdims=True)
    run_ref[...] = run
    cnt_ref[...] = run.astype(jnp.int32)

    meta_ref[...] = jnp.where(row8 == 0, eid1, jnp.where(row8 == 1, eid2,
                              jnp.where(row8 == 2, rank1, jnp.where(row8 == 3, rank2, 0))))
    row128 = lax.broadcasted_iota(jnp.int32, (LANES, tm), 0)
    gt = jnp.where(row128 == 0, gate1, jnp.where(row128 == 1, gate2, 0.0))
    gates_ref[...] = gt.T


def _outproj(x2, ym, yd, wo, g2, wr, br):
    n = x2.shape[0]
    row = lambda i: (i, 0)
    fixed = lambda i: (0, 0)
    tri = jnp.triu(jnp.ones((TM_PROJ, TM_PROJ), F32), k=1).astype(BF16)
    return pl.pallas_call(
        _outproj_kernel,
        grid=(n // TM_PROJ,),
        in_specs=[pl.BlockSpec((TM_PROJ, D_MODEL), row),
                  pl.BlockSpec((TM_PROJ, D_MLSTM), row),
                  pl.BlockSpec((TM_PROJ, D_DIFF), row),
                  pl.BlockSpec((D_MODEL, D_MODEL), fixed),
                  pl.BlockSpec((1, D_MODEL), fixed),
                  pl.BlockSpec((3 * D_MODEL, LANES), fixed),
                  pl.BlockSpec((1, LANES), fixed),
                  pl.BlockSpec((TM_PROJ, TM_PROJ), fixed)],
        out_specs=[pl.BlockSpec((TM_PROJ, D_MODEL), row),
                   pl.BlockSpec((TM_PROJ, D_MODEL), row),
                   pl.BlockSpec((SUBLANES, TM_PROJ), lambda i: (0, i)),
                   pl.BlockSpec((TM_PROJ, LANES), row),
                   pl.BlockSpec((N_EXPERTS, LANES), fixed)],
        out_shape=[jax.ShapeDtypeStruct((n, D_MODEL), F32),
                   jax.ShapeDtypeStruct((n, D_MODEL), F32),
                   jax.ShapeDtypeStruct((SUBLANES, n), jnp.int32),
                   jax.ShapeDtypeStruct((n, LANES), F32),
                   jax.ShapeDtypeStruct((N_EXPERTS, LANES), jnp.int32)],
        scratch_shapes=[pltpu.VMEM((N_EXPERTS, LANES), F32)],
        compiler_params=_cparams(("arbitrary",)),
        name="outproj_router",
    )(x2, ym, yd, wo, g2, wr, br, tri)


def _row_copy(src_ref, src_row, dst_ref, dst_row, sem):
    return pltpu.make_async_copy(src_ref.at[pl.ds(src_row, 1)], dst_ref.at[pl.ds(dst_row, 1)], sem)


def _dispatch_kernel(ends_ref, pos_ref, h_ref, xs_ref, zero_ref, sem, zsem):
    step = pl.program_id(0)

    @pl.when(step == 0)
    def _():
        zero_ref[...] = jnp.zeros(zero_ref.shape, F32)

        def tile_copy(e):
            start = pl.multiple_of(ends_ref[e + 1] - TM_MOE, TM_MOE)
            return pltpu.make_async_copy(zero_ref, xs_ref.at[pl.ds(start, TM_MOE)], zsem)

        def zstart(e, carry):
            @pl.when(ends_ref[e + 1] > ends_ref[e])
            def _():
                tile_copy(e).start()
            return carry

        def zwait(e, carry):
            @pl.when(ends_ref[e + 1] > ends_ref[e])
            def _():
                tile_copy(e).wait()
            return carry

        lax.fori_loop(0, N_EXPERTS, zstart, 0)
        lax.fori_loop(0, N_EXPERTS, zwait, 0)

        def tail_copy(t):
            return pltpu.make_async_copy(zero_ref, xs_ref.at[pl.ds(pl.multiple_of(t * TM_MOE, TM_MOE), TM_MOE)], zsem)

        def tstart(t, carry):
            tail_copy(t).start()
            return carry

        def twait(t, carry):
            tail_copy(t).wait()
            return carry

        first_tail = ends_ref[N_EXPERTS] // TM_MOE
        lax.fori_loop(first_tail, xs_ref.shape[0] // TM_MOE, tstart, 0)
        lax.fori_loop(first_tail, xs_ref.shape[0] // TM_MOE, twait, 0)

    def issue(r, carry):
        _row_copy(h_ref, r, xs_ref, pos_ref[0, r], sem).start()
        _row_copy(h_ref, r, xs_ref, pos_ref[0, TT + r], sem).start()
        return carry

    def drain(r, carry):
        _row_copy(h_ref, r, xs_ref, pos_ref[0, r], sem).wait()
        _row_copy(h_ref, r, xs_ref, pos_ref[0, TT + r], sem).wait()
        return carry

    lax.fori_loop(0, TT, issue, 0)
    lax.fori_loop(0, TT, drain, 0)


def _dispatch(ends, pos, h2, n_rows):
    n = h2.shape[0]
    return pl.pallas_call(
        _dispatch_kernel,
        grid_spec=pltpu.PrefetchScalarGridSpec(
            num_scalar_prefetch=1,
            grid=(n // TT,),
            in_specs=[pl.BlockSpec((None, 1, 2 * TT), lambda i, ends: (i, 0, 0), memory_space=pltpu.SMEM),
                      pl.BlockSpec((TT, D_MODEL), lambda i, ends: (i, 0))],
            out_specs=pl.BlockSpec(memory_space=pl.ANY),
            scratch_shapes=[pltpu.VMEM((TM_MOE, D_MODEL), F32),
                            pltpu.SemaphoreType.DMA(()),
                            pltpu.SemaphoreType.DMA(())]),
        out_shape=jax.ShapeDtypeStruct((n_rows, D_MODEL), F32),
        compiler_params=_cparams(("arbitrary",)),
        name="dispatch",
    )(ends, pos, h2)


def _moe_kernel(texp_ref, nact_ref, x_ref, wg_ref, wu_ref, wd_ref, y_ref):
    t = pl.program_id(0)

    @pl.when(t < nact_ref[0])
    def _():
        x = x_ref[...].astype(BF16)
        g = jnp.dot(x, wg_ref[...], preferred_element_type=F32)
        u = jnp.dot(x, wu_ref[...], preferred_element_type=F32)
        hid = (g * jax.nn.sigmoid(g) * u).astype(BF16)
        y_ref[...] = jnp.dot(hid, wd_ref[...], preferred_element_type=F32)

    @pl.when(t >= nact_ref[0])
    def _():
        y_ref[...] = jnp.zeros(y_ref.shape, F32)


def _moe(tile_expert, n_active, xs, w_eg, w_eu, w_ed):
    n_tiles = xs.shape[0] // TM_MOE
    tile = lambda t, te, na: (jnp.minimum(t, na[0] - 1), 0)
    out_tile = lambda t, te, na: (t, 0)
    wsel = lambda t, te, na: (te[t], 0, 0)
    return pl.pallas_call(
        _moe_kernel,
        grid_spec=pltpu.PrefetchScalarGridSpec(
            num_scalar_prefetch=2,
            grid=(n_tiles,),
            in_specs=[pl.BlockSpec((TM_MOE, D_MODEL), tile),
                      pl.BlockSpec((None, D_MODEL, D_FF_EXP), wsel),
                      pl.BlockSpec((None, D_MODEL, D_FF_EXP), wsel),
                      pl.BlockSpec((None, D_FF_EXP, D_MODEL), wsel)],
            out_specs=pl.BlockSpec((TM_MOE, D_MODEL), out_tile)),
        out_shape=jax.ShapeDtypeStruct(xs.shape, F32),
        compiler_params=_cparams(("arbitrary",)),
        name="moe",
    )(tile_expert, n_active, xs, w_eg, w_eu, w_ed)


def _combine_kernel(pos_ref, x1_ref, gates_ref, gf_ref, ys_ref, o_ref, buf_ref, sem):
    def issue(r, carry):
        _row_copy(ys_ref, pos_ref[0, r], buf_ref.at[0], r, sem).start()
        _row_copy(ys_ref, pos_ref[0, TT + r], buf_ref.at[1], r, sem).start()
        return carry

    def drain(r, carry):
        _row_copy(ys_ref, pos_ref[0, r], buf_ref.at[0], r, sem).wait()
        _row_copy(ys_ref, pos_ref[0, TT + r], buf_ref.at[1], r, sem).wait()
        return carry

    lax.fori_loop(0, TT, issue, 0)
    lax.fori_loop(0, TT, drain, 0)
    gates = gates_ref[...]
    y = x1_ref[...] + gates[:, 0:1] * buf_ref[0] + gates[:, 1:2] * buf_ref[1]
    ms = jnp.mean(y * y, axis=-1, keepdims=True)
    o_ref[...] = y * lax.rsqrt(ms + EPS) * gf_ref[...]


def _combine(pos, x1, gates, gf, ys):
    n = x1.shape[0]
    return pl.pallas_call(
        _combine_kernel,
        grid=(n // TT,),
        in_specs=[pl.BlockSpec((None, 1, 2 * TT), lambda i: (i, 0, 0), memory_space=pltpu.SMEM),
                  pl.BlockSpec((TT, D_MODEL), lambda i: (i, 0)),
                  pl.BlockSpec((TT, LANES), lambda i: (i, 0)),
                  pl.BlockSpec((1, D_MODEL), lambda i: (0, 0)),
                  pl.BlockSpec(memory_space=pl.ANY)],
        out_specs=pl.BlockSpec((TT, D_MODEL), lambda i: (i, 0)),
        out_shape=jax.ShapeDtypeStruct((n, D_MODEL), F32),
        scratch_shapes=[pltpu.VMEM((2, TT, D_MODEL), F32),
                        pltpu.SemaphoreType.DMA(())],
        compiler_params=_cparams(("arbitrary",)),
        name="combine",
    )(pos, x1, gates, gf, ys)


def kernel(x, w_in, conv_w, conv_b, w_mq, w_mk, w_mgate, b_mgate, m_norm_g, m_skip, lambda_qk, da_norm_g,
           rel_bias, w_out, norm1_g, norm2_g, w_rg, b_rg, w_re, b_re, w_eg, w_eu, w_ed, normf_g):
    bsz, seq, _ = x.shape
    n = bsz * seq
    assert seq % TQ == 0 and seq % CHUNK == 0 and n % TM_PROJ == 0 and n % TT == 0
    l = 0
    x2 = x.reshape(n, D_MODEL)

    c, vm, z, qd, kd, vd = _inproj(x2, norm1_g[l][None, :], w_in[l].astype(BF16))

    wg = jnp.zeros((3 * D_MLSTM, LANES), F32).at[:, :2 * ML_HEADS].set(w_mgate[l]).astype(BF16)
    bg = jnp.zeros((1, LANES), F32).at[0, :2 * ML_HEADS].set(b_mgate[l])
    y_m = _mlstm(c, vm, z, conv_w[l], conv_b[l][None, :], w_mq[l].astype(BF16), w_mk[l].astype(BF16),
                 wg, bg, m_norm_g[l][None, :], m_skip[l][None, :], bsz, seq)

    y_d = _attn(lambda_qk[l].astype(F32), qd, kd, vd, _rel_bias_tiles(rel_bias),
                da_norm_g[l][None, :], bsz, seq)

    wr = jnp.zeros((D_MODEL, LANES), F32)
    wr = wr.at[:, :N_GROUPS].set(w_rg[l])
    wr = wr.at[:, SUBLANES:SUBLANES + N_EXPERTS].set(w_re[l].reshape(D_MODEL, N_EXPERTS))
    wr_hi = wr.astype(BF16)
    wr_lo = (wr - wr_hi.astype(F32)).astype(BF16)
    wr3 = jnp.concatenate([wr_hi, wr_hi, wr_lo], axis=0)
    br = jnp.zeros((1, LANES), F32)
    br = br.at[0, :N_GROUPS].set(b_rg[l])
    br = br.at[0, SUBLANES:SUBLANES + N_EXPERTS].set(b_re[l].reshape(N_EXPERTS))
    x1, h2, meta, gates, counts = _outproj(x2, y_m, y_d, w_out[l].astype(BF16), norm2_g[l][None, :], wr3, br)

    cnt = counts[:, 0]
    tiles_e = (cnt + TM_MOE - 1) // TM_MOE
    ends_t = jnp.cumsum(tiles_e)
    ends = jnp.concatenate([jnp.zeros((1,), jnp.int32), ends_t * TM_MOE]).astype(jnp.int32)
    offs = ends[:-1]
    n_tiles = (2 * n) // TM_MOE + N_EXPERTS
    n_active = ends_t[-1:].astype(jnp.int32)
    tile_ids = jnp.minimum(jnp.arange(n_tiles, dtype=jnp.int32), n_active[0] - 1)
    tile_expert = jnp.sum(tile_ids[:, None] >= ends_t[None, :], axis=1).astype(jnp.int32)
    pos1 = offs[meta[0]] + meta[2]
    pos2 = offs[meta[1]] + meta[3]
    pos = jnp.concatenate([pos1.reshape(n // TT, 1, TT), pos2.reshape(n // TT, 1, TT)], axis=2)

    xs = _dispatch(ends, pos, h2, n_tiles * TM_MOE)
    ys = _moe(tile_expert, n_active, xs, w_eg[l].astype(BF16), w_eu[l].astype(BF16), w_ed[l].astype(BF16))
    out = _combine(pos, x1, gates, normf_g[None, :], ys)
    return out.reshape(bsz, seq, D_MODEL)
```

```python
import functools
import math

import jax
import jax.numpy as jnp
from jax import lax
from jax.experimental import pallas as pl
from jax.experimental.pallas import tpu as pltpu

F32 = jnp.float32
BF16 = jnp.bfloat16

D_MODEL = 1024
D_MLSTM = 512
D_DIFF = 512
ML_HEADS = 4
ML_HD = 128
CONV_W = 4
CHUNK = 128
DA_HEADS = 4
DA_VD = 128
DA_QD = 64
REL_BUCKETS = 32
REL_MAX_DIST = 128
N_GROUPS = 4
EXP_PER_GROUP = 8
N_EXPERTS = 32
D_FF_EXP = 512
EPS = 1e-6
SUBLN_EPS = 1e-5
LAM_INIT = 0.8 - 0.6 * math.exp(-0.3 * 0)
NEG = -1e30

LANES = 128
SUBLANES = 8
MXU_COLS = 256

TM_PROJ = 512
TQ = 512
TM_MOE = 256
TT = 512
VMEM_LIMIT = 48 * 1024 * 1024


def _cparams(sem):
    return pltpu.CompilerParams(dimension_semantics=sem, vmem_limit_bytes=VMEM_LIMIT)


def _inproj_kernel(x_ref, g_ref, w_ref, wqv_ref, c_ref, vm_ref, z_ref, q_ref, k_ref, v_ref):
    x = x_ref[...]
    ms = jnp.mean(x * x, axis=-1, keepdims=True)
    h = (x * lax.rsqrt(ms + EPS) * g_ref[...]).astype(BF16)
    outs = (c_ref, vm_ref, z_ref, None, k_ref, None)
    for j, o in enumerate(outs):
        if o is not None:
            r = jnp.dot(h, w_ref[:, j * 512:(j + 1) * 512], preferred_element_type=F32)
            o[...] = r.astype(o.dtype)
    nt = (((1,), (1,)), ((), ()))
    qt = lax.dot_general(wqv_ref[0], h, nt, preferred_element_type=F32)
    q_ref[...] = (qt * (DA_QD ** -0.5)).astype(q_ref.dtype)
    v_ref[...] = lax.dot_general(wqv_ref[1], h, nt, preferred_element_type=F32).astype(v_ref.dtype)


def _inproj(x2, g1, w_in, wqv_t):
    n = x2.shape[0]
    row = lambda i: (i, 0)
    fixed = lambda i: (0, 0)
    sds = lambda dt: jax.ShapeDtypeStruct((n, 512), dt)
    sds_t = jax.ShapeDtypeStruct((512, n), BF16)
    blk = pl.BlockSpec((TM_PROJ, 512), row)
    blk_t = pl.BlockSpec((512, TM_PROJ), lambda i: (0, i))
    return pl.pallas_call(
        _inproj_kernel,
        grid=(n // TM_PROJ,),
        in_specs=[pl.BlockSpec((TM_PROJ, D_MODEL), row),
                  pl.BlockSpec((1, D_MODEL), fixed),
                  pl.BlockSpec((D_MODEL, 6 * 512), fixed),
                  pl.BlockSpec((2, 512, D_MODEL), lambda i: (0, 0, 0))],
        out_specs=[blk, blk, blk, blk_t, blk, blk_t],
        out_shape=[sds(F32), sds(BF16), sds(F32), sds_t, sds(BF16), sds_t],
        compiler_params=_cparams(("parallel",)),
        name="inproj",
    )(x2, g1, w_in, wqv_t)


def _mlstm_kernel(c_ref, vm_ref, z_ref, cw_ref, cb_ref, wq_ref, wk_ref, wg_ref, bg_ref,
                  ng_ref, sk_ref, tril_ref, y_ref, ext_ref, st_ref, m_ref):
    L = CHUNK
    j = pl.program_id(1)

    @pl.when(j == 0)
    def _():
        ext_ref[0:SUBLANES, :] = jnp.zeros((SUBLANES, D_MLSTM), F32)
        st_ref[...] = jnp.zeros(st_ref.shape, F32)
        m_ref[...] = jnp.zeros(m_ref.shape, F32)

    ext_ref[SUBLANES:SUBLANES + L, :] = c_ref[...]
    cw = cw_ref[...]
    conv = cb_ref[...] + cw[CONV_W - 1:CONV_W, :] * ext_ref[SUBLANES:SUBLANES + L, :]
    for k in range(1, CONV_W):
        conv = conv + cw[CONV_W - 1 - k:CONV_W - k, :] * ext_ref[SUBLANES - k:SUBLANES - k + L, :]
    ext_ref[0:SUBLANES, :] = ext_ref[L:L + SUBLANES, :]
    c_act = conv * jax.nn.sigmoid(conv)

    vm = vm_ref[...]
    qs, ks = [], []
    gates = jnp.dot(vm, wg_ref[2 * D_MLSTM:3 * D_MLSTM, :], preferred_element_type=F32) + bg_ref[...]
    for h in range(ML_HEADS):
        ch = c_act[:, h * ML_HD:(h + 1) * ML_HD].astype(BF16)
        qh = jnp.dot(ch, wq_ref[h], preferred_element_type=F32)
        kh = jnp.dot(ch, wk_ref[h], preferred_element_type=F32)
        qs.append(qh)
        ks.append(kh)
        gates = gates + jnp.dot(qh.astype(BF16), wg_ref[h * ML_HD:(h + 1) * ML_HD, :],
                                preferred_element_type=F32)
        gates = gates + jnp.dot(kh.astype(BF16), wg_ref[D_MLSTM + h * ML_HD:D_MLSTM + (h + 1) * ML_HD, :],
                                preferred_element_type=F32)

    logf = jnp.minimum(gates, 0.0) - jnp.log(1.0 + jnp.exp(-jnp.abs(gates)))
    bcum = jnp.dot(tril_ref[...], logf, preferred_element_type=F32, precision=lax.Precision.HIGHEST)
    gates_t = gates.T
    bcum_t = bcum.T
    row_i = lax.broadcasted_iota(jnp.int32, (L, L), 0)
    col_i = lax.broadcasted_iota(jnp.int32, (L, L), 1)
    causal = col_i <= row_i
    ones_col = jnp.where(lax.broadcasted_iota(jnp.int32, (L, LANES), 1) == 0, 1.0, 0.0).astype(F32)

    for h in range(ML_HEADS):
        q = qs[h].astype(BF16)
        k = (ks[h] * (ML_HD ** -0.5)).astype(BF16)
        v = vm[:, h * ML_HD:(h + 1) * ML_HD]
        i_col = gates[:, h:h + 1]
        i_row = gates_t[h:h + 1, :]
        b_col = bcum[:, ML_HEADS + h:ML_HEADS + h + 1]
        b_row = bcum_t[ML_HEADS + h:ML_HEADS + h + 1, :]
        m_prev = m_ref[h][0:1, 0:1]
        state = st_ref[h]

        inter = b_col + m_prev
        dmat = jnp.where(causal, b_col - b_row + i_row, NEG)
        m_t = jnp.maximum(inter, jnp.max(dmat, axis=-1, keepdims=True))
        dexp = jnp.exp(dmat - m_t)
        s = lax.dot_general(q, k, (((1,), (1,)), ((), ())), preferred_element_type=F32)
        w = (dexp * s).astype(BF16)
        sp = jnp.exp(inter - m_t)
        vaug = jnp.concatenate([v, ones_col.astype(BF16)], axis=1)
        r_inter = jnp.dot(q, state.astype(BF16), preferred_element_type=F32)
        r_intra = jnp.dot(w, vaug, preferred_element_type=F32)
        tot = sp * r_inter + r_intra
        num = tot[:, :ML_HD]
        den = tot[:, ML_HD:ML_HD + 1]
        hh = num / jnp.maximum(jnp.abs(den), jnp.exp(-m_t))

        b_end = b_col[L - 1:L, :]
        g_col = b_end - b_col + i_col
        m_new = jnp.maximum(b_end + m_prev, jnp.max(g_col, axis=0, keepdims=True))
        wk = jnp.exp(g_col - m_new)
        decay = jnp.exp(b_end + m_prev - m_new)
        vw = (jnp.concatenate([v.astype(F32), ones_col], axis=1) * wk).astype(BF16)
        upd = lax.dot_general(k, vw, (((0,), (0,)), ((), ())), preferred_element_type=F32)
        st_ref[h] = decay * state + upd
        m_ref[h] = jnp.broadcast_to(m_new, (SUBLANES, LANES))

        mu = jnp.mean(hh, axis=-1, keepdims=True)
        xc = hh - mu
        var = jnp.mean(xc * xc, axis=-1, keepdims=True)
        sl = slice(h * ML_HD, (h + 1) * ML_HD)
        hn = xc * lax.rsqrt(var + EPS) * ng_ref[:, sl]
        hn = hn + sk_ref[:, sl] * c_act[:, sl]
        o = jax.nn.sigmoid(z_ref[:, sl])
        y_ref[:, sl] = (o * hn).astype(y_ref.dtype)


def _mlstm(c, vm, z, conv_w, conv_b, wq, wk, wg, bg, ng, sk, bsz, seq):
    nc = seq // CHUNK
    row = lambda b, j: (b * nc + j, 0)
    f2 = lambda b, j: (0, 0)
    f3 = lambda b, j: (0, 0, 0)
    tril = jnp.tril(jnp.ones((CHUNK, CHUNK), F32))
    blk = pl.BlockSpec((CHUNK, D_MLSTM), row)
    return pl.pallas_call(
        _mlstm_kernel,
        grid=(bsz, nc),
        in_specs=[blk, blk, blk,
                  pl.BlockSpec((CONV_W, D_MLSTM), f2),
                  pl.BlockSpec((1, D_MLSTM), f2),
                  pl.BlockSpec((ML_HEADS, ML_HD, ML_HD), f3),
                  pl.BlockSpec((ML_HEADS, ML_HD, ML_HD), f3),
                  pl.BlockSpec((3 * D_MLSTM, LANES), f2),
                  pl.BlockSpec((1, LANES), f2),
                  pl.BlockSpec((1, D_MLSTM), f2),
                  pl.BlockSpec((1, D_MLSTM), f2),
                  pl.BlockSpec((CHUNK, CHUNK), f2)],
        out_specs=blk,
        out_shape=jax.ShapeDtypeStruct((bsz * seq, D_MLSTM), BF16),
        scratch_shapes=[pltpu.VMEM((SUBLANES + CHUNK, D_MLSTM), F32),
                        pltpu.VMEM((ML_HEADS, ML_HD, 2 * ML_HD), F32),
                        pltpu.VMEM((ML_HEADS, SUBLANES, LANES), F32)],
        compiler_params=_cparams(("parallel", "arbitrary")),
        name="mlstm",
    )(c, vm, z, conv_w, conv_b, wq, wk, wg, bg, ng, sk, tril)


def _attn_kernel(lq_ref, qt_ref, k_ref, vt_ref, bias_ref, g_ref, o_ref, acc_ref, m_ref, l_ref):
    tq = TQ
    i = pl.program_id(2)
    qt = qt_ref[...]
    row = lax.broadcasted_iota(jnp.int32, (LANES, tq), 0)
    zero = jnp.zeros_like(qt)
    q2t = jnp.concatenate([jnp.where(row < DA_QD, qt, zero), jnp.where(row >= DA_QD, qt, zero)], axis=1)
    acc_ref[...] = jnp.zeros(acc_ref.shape, F32)
    m_ref[...] = jnp.full(m_ref.shape, NEG, F32)
    l_ref[...] = jnp.zeros(l_ref.shape, F32)

    def step(kb, bias):
        start = pl.multiple_of(kb * tq, tq)
        k_blk = k_ref[pl.ds(start, tq), :]
        vt_blk = vt_ref[:, pl.ds(start, tq)]
        s = jnp.dot(k_blk, q2t, preferred_element_type=F32)
        if bias is not None:
            s = s + jnp.concatenate([bias, bias], axis=1)
        m_old = m_ref[...]
        m_new = jnp.maximum(m_old, jnp.max(s, axis=0, keepdims=True))
        alpha = jnp.exp(m_old - m_new)
        p = jnp.exp(s - m_new)
        l_ref[...] = alpha * l_ref[...] + jnp.sum(p, axis=0, keepdims=True)
        acc_ref[...] = alpha * acc_ref[...] + jnp.dot(vt_blk, p.astype(BF16), preferred_element_type=F32)
        m_ref[...] = m_new

    def far(kb, carry):
        step(kb, None)
        return carry

    lax.fori_loop(0, i - 1, far, 0)

    @pl.when(i >= 1)
    def _():
        step(i - 1, bias_ref[1])

    step(i, bias_ref[0])

    o = acc_ref[...] * (1.0 / l_ref[...])
    lq = lq_ref[...]
    d01 = jnp.sum(lq[0:1, :] * lq[1:2, :], axis=-1, keepdims=True)
    d23 = jnp.sum(lq[2:3, :] * lq[3:4, :], axis=-1, keepdims=True)
    lam = jnp.exp(d01) - jnp.exp(d23) + LAM_INIT
    y = o[:, :tq] - lam * o[:, tq:]
    y = y * lax.rsqrt(jnp.mean(y * y, axis=0, keepdims=True) + SUBLN_EPS) * g_ref[...]
    o_ref[...] = (y * (1.0 - LAM_INIT)).T.astype(o_ref.dtype)


def _rel_bias_tiles(rel_bias):
    r = jnp.arange(TQ)[None, :]
    c = jnp.arange(TQ)[:, None]
    max_exact = REL_BUCKETS // 2

    def bucket(n):
        n = jnp.maximum(n, 0)
        large = max_exact + (jnp.log(jnp.maximum(n, 1).astype(F32) / max_exact)
                             / math.log(REL_MAX_DIST / max_exact) * (REL_BUCKETS - max_exact)).astype(jnp.int32)
        large = jnp.minimum(large, REL_BUCKETS - 1)
        return jnp.where(n < max_exact, n, large)

    rb = rel_bias.astype(F32) - rel_bias[REL_BUCKETS - 1].astype(F32)[None, :]
    n0 = r - c
    t0 = jnp.where((n0 >= 0)[:, :, None], rb[bucket(n0)], NEG)
    t1 = rb[bucket(n0 + TQ)]
    return jnp.stack([t0, t1], axis=0).transpose(3, 0, 1, 2)


def _attn(lq, qt, kd, vt, bias, g, bsz, seq):
    nq = seq // TQ
    return pl.pallas_call(
        _attn_kernel,
        grid=(bsz, DA_HEADS, nq),
        in_specs=[pl.BlockSpec((4, DA_QD), lambda b, h, i: (0, 0)),
                  pl.BlockSpec((LANES, TQ), lambda b, h, i: (h, b * nq + i)),
                  pl.BlockSpec((seq, LANES), lambda b, h, i: (b, h)),
                  pl.BlockSpec((LANES, seq), lambda b, h, i: (h, b)),
                  pl.BlockSpec((None, 2, TQ, TQ), lambda b, h, i: (h, 0, 0, 0)),
                  pl.BlockSpec((DA_VD, 1), lambda b, h, i: (0, 0))],
        out_specs=pl.BlockSpec((TQ, LANES), lambda b, h, i: (b * nq + i, h)),
        out_shape=jax.ShapeDtypeStruct((bsz * seq, D_DIFF), BF16),
        scratch_shapes=[pltpu.VMEM((DA_VD, 2 * TQ), F32),
                        pltpu.VMEM((1, 2 * TQ), F32),
                        pltpu.VMEM((1, 2 * TQ), F32)],
        compiler_params=_cparams(("parallel", "parallel", "arbitrary")),
        name="attn",
    )(lq, qt, kd, vt, bias, g)


def _outproj_kernel(x_ref, ym_ref, yd_ref, wo_ref, g2_ref, wr_ref, br_ref, tri_ref,
                    x1_ref, h2_ref, meta_ref, gates_ref, cnt_ref, run_ref):
    tm = TM_PROJ
    step = pl.program_id(0)

    @pl.when(step == 0)
    def _():
        run_ref[...] = jnp.zeros(run_ref.shape, F32)

    x1 = (x_ref[...]
          + jnp.dot(ym_ref[...], wo_ref[0:D_MLSTM, :], preferred_element_type=F32)
          + jnp.dot(yd_ref[...], wo_ref[D_MLSTM:, :], preferred_element_type=F32))
    x1_ref[...] = x1
    ms = jnp.mean(x1 * x1, axis=-1, keepdims=True)
    h2 = x1 * lax.rsqrt(ms + EPS) * g2_ref[...]
    h2_ref[...] = h2

    h_hi = h2.astype(BF16)
    h_lo = (h2 - h_hi.astype(F32)).astype(BF16)
    lhs = jnp.concatenate([h_hi, h_lo, h_hi], axis=1)
    logits = jnp.dot(lhs, wr_ref[...], preferred_element_type=F32) + br_ref[...]
    lt = logits.T

    row8 = lax.broadcasted_iota(jnp.int32, (SUBLANES, tm), 0)
    lg = jnp.where(row8 < N_GROUPS, lt[0:SUBLANES], NEG)
    gmax = jnp.max(lg, axis=0, keepdims=True)
    gidx = jnp.min(jnp.where(lg == gmax, row8, SUBLANES), axis=0, keepdims=True)
    pg_sel = 1.0 / jnp.sum(jnp.exp(lg - gmax), axis=0, keepdims=True)
    le = jnp.zeros((EXP_PER_GROUP, tm), F32)
    for g in range(N_GROUPS):
        le = jnp.where(gidx == g, lt[SUBLANES * (g + 1):SUBLANES * (g + 2)], le)
    v1 = jnp.max(le, axis=0, keepdims=True)
    i1 = jnp.min(jnp.where(le == v1, row8, SUBLANES), axis=0, keepdims=True)
    le2 = jnp.where(row8 == i1, NEG, le)
    v2 = jnp.max(le2, axis=0, keepdims=True)
    i2 = jnp.min(jnp.where(le2 == v2, row8, SUBLANES), axis=0, keepdims=True)
    e21 = jnp.exp(v2 - v1)
    pw1 = 1.0 / (1.0 + e21)
    gate1 = pg_sel * pw1
    gate2 = pg_sel * (e21 * pw1)
    eid1 = gidx * EXP_PER_GROUP + i1
    eid2 = gidx * EXP_PER_GROUP + i2

    row32 = lax.broadcasted_iota(jnp.int32, (N_EXPERTS, tm), 0)
    oh1 = row32 == eid1
    oh2 = row32 == eid2
    oh = jnp.where(oh1, 1.0, 0.0) + jnp.where(oh2, 1.0, 0.0)
    before = jnp.dot(oh.astype(BF16), tri_ref[...], preferred_element_type=F32) + run_ref[:, 0:1]
    rank1 = jnp.sum(jnp.where(oh1, before, 0.0), axis=0, keepdims=True).astype(jnp.int32)
    rank2 = jnp.sum(jnp.where(oh2, before, 0.0), axis=0, keepdims=True).astype(jnp.int32)
    run = run_ref[...] + jnp.sum(oh, axis=1, keepdims=True)
    run_ref[...] = run
    cnt_ref[...] = run.astype(jnp.int32)

    meta_ref[...] = jnp.where(row8 == 0, eid1, jnp.where(row8 == 1, eid2,
                              jnp.where(row8 == 2, rank1, jnp.where(row8 == 3, rank2, 0))))
    row128 = lax.broadcasted_iota(jnp.int32, (LANES, tm), 0)
    gt = jnp.where(row128 == 0, gate1, jnp.where(row128 == 1, gate2, 0.0))
    gates_ref[...] = gt.T


def _outproj(x2, ym, yd, wo, g2, wr, br):
    n = x2.shape[0]
    row = lambda i: (i, 0)
    fixed = lambda i: (0, 0)
    tri = jnp.triu(jnp.ones((TM_PROJ, TM_PROJ), F32), k=1).astype(BF16)
    return pl.pallas_call(
        _outproj_kernel,
        grid=(n // TM_PROJ,),
        in_specs=[pl.BlockSpec((TM_PROJ, D_MODEL), row),
                  pl.BlockSpec((TM_PROJ, D_MLSTM), row),
                  pl.BlockSpec((TM_PROJ, D_DIFF), row),
                  pl.BlockSpec((D_MODEL, D_MODEL), fixed),
                  pl.BlockSpec((1, D_MODEL), fixed),
                  pl.BlockSpec((3 * D_MODEL, LANES), fixed),
                  pl.BlockSpec((1, LANES), fixed),
                  pl.BlockSpec((TM_PROJ, TM_PROJ), fixed)],
        out_specs=[pl.BlockSpec((TM_PROJ, D_MODEL), row),
                   pl.BlockSpec((TM_PROJ, D_MODEL), row),
                   pl.BlockSpec((SUBLANES, TM_PROJ), lambda i: (0, i)),
                   pl.BlockSpec((TM_PROJ, LANES), row),
                   pl.BlockSpec((N_EXPERTS, LANES), fixed)],
        out_shape=[jax.ShapeDtypeStruct((n, D_MODEL), F32),
                   jax.ShapeDtypeStruct((n, D_MODEL), F32),
                   jax.ShapeDtypeStruct((SUBLANES, n), jnp.int32),
                   jax.ShapeDtypeStruct((n, LANES), F32),
                   jax.ShapeDtypeStruct((N_EXPERTS, LANES), jnp.int32)],
        scratch_shapes=[pltpu.VMEM((N_EXPERTS, LANES), F32)],
        compiler_params=_cparams(("arbitrary",)),
        name="outproj_router",
    )(x2, ym, yd, wo, g2, wr, br, tri)


def _row_copy(src_ref, src_row, dst_ref, dst_row, sem):
    return pltpu.make_async_copy(src_ref.at[pl.ds(src_row, 1)], dst_ref.at[pl.ds(dst_row, 1)], sem)


def _dispatch_kernel(ends_ref, pos_ref, h_ref, xs_ref, zero_ref, sem, zsem):
    step = pl.program_id(0)

    @pl.when(step == 0)
    def _():
        zero_ref[...] = jnp.zeros(zero_ref.shape, F32)

        def tile_copy(e):
            start = pl.multiple_of(ends_ref[e + 1] - TM_MOE, TM_MOE)
            return pltpu.make_async_copy(zero_ref, xs_ref.at[pl.ds(start, TM_MOE)], zsem)

        def zstart(e, carry):
            @pl.when(ends_ref[e + 1] > ends_ref[e])
            def _():
                tile_copy(e).start()
            return carry

        def zwait(e, carry):
            @pl.when(ends_ref[e + 1] > ends_ref[e])
            def _():
                tile_copy(e).wait()
            return carry

        lax.fori_loop(0, N_EXPERTS, zstart, 0)
        lax.fori_loop(0, N_EXPERTS, zwait, 0)

        def tail_copy(t):
            return pltpu.make_async_copy(zero_ref, xs_ref.at[pl.ds(pl.multiple_of(t * TM_MOE, TM_MOE), TM_MOE)], zsem)

        def tstart(t, carry):
            tail_copy(t).start()
            return carry

        def twait(t, carry):
            tail_copy(t).wait()
            return carry

        first_tail = ends_ref[N_EXPERTS] // TM_MOE
        lax.fori_loop(first_tail, xs_ref.shape[0] // TM_MOE, tstart, 0)
        lax.fori_loop(first_tail, xs_ref.shape[0] // TM_MOE, twait, 0)

    def issue(r, carry):
        _row_copy(h_ref, r, xs_ref, pos_ref[0, r], sem).start()
        _row_copy(h_ref, r, xs_ref, pos_ref[0, TT + r], sem).start()
        return carry

    def drain(r, carry):
        _row_copy(h_ref, r, xs_ref, pos_ref[0, r], sem).wait()
        _row_copy(h_ref, r, xs_ref, pos_ref[0, TT + r], sem).wait()
        return carry

    lax.fori_loop(0, TT, issue, 0)
    lax.fori_loop(0, TT, drain, 0)


def _dispatch(ends, pos, h2, n_rows):
    n = h2.shape[0]
    return pl.pallas_call(
        _dispatch_kernel,
        grid_spec=pltpu.PrefetchScalarGridSpec(
            num_scalar_prefetch=1,
            grid=(n // TT,),
            in_specs=[pl.BlockSpec((None, 1, 2 * TT), lambda i, ends: (i, 0, 0), memory_space=pltpu.SMEM),
                      pl.BlockSpec((TT, D_MODEL), lambda i, ends: (i, 0))],
            out_specs=pl.BlockSpec(memory_space=pl.ANY),
            scratch_shapes=[pltpu.VMEM((TM_MOE, D_MODEL), F32),
                            pltpu.SemaphoreType.DMA(()),
                            pltpu.SemaphoreType.DMA(())]),
        out_shape=jax.ShapeDtypeStruct((n_rows, D_MODEL), F32),
        compiler_params=_cparams(("arbitrary",)),
        name="dispatch",
    )(ends, pos, h2)


def _moe_kernel(texp_ref, nact_ref, x_ref, wg_ref, wu_ref, wd_ref, y_ref):
    t = pl.program_id(0)

    @pl.when(t < nact_ref[0])
    def _():
        x = x_ref[...].astype(BF16)
        g = jnp.dot(x, wg_ref[...], preferred_element_type=F32)
        u = jnp.dot(x, wu_ref[...], preferred_element_type=F32)
        hid = (g * jax.nn.sigmoid(g) * u).astype(BF16)
        y_ref[...] = jnp.dot(hid, wd_ref[...], preferred_element_type=F32)

    @pl.when(t >= nact_ref[0])
    def _():
        y_ref[...] = jnp.zeros(y_ref.shape, F32)


def _moe(tile_expert, n_active, xs, w_eg, w_eu, w_ed):
    n_tiles = xs.shape[0] // TM_MOE
    tile = lambda t, te, na: (jnp.minimum(t, na[0] - 1), 0)
    out_tile = lambda t, te, na: (t, 0)
    wsel = lambda t, te, na: (te[t], 0, 0)
    return pl.pallas_call(
        _moe_kernel,
        grid_spec=pltpu.PrefetchScalarGridSpec(
            num_scalar_prefetch=2,
            grid=(n_tiles,),
            in_specs=[pl.BlockSpec((TM_MOE, D_MODEL), tile),
                      pl.BlockSpec((None, D_MODEL, D_FF_EXP), wsel),
                      pl.BlockSpec((None, D_MODEL, D_FF_EXP), wsel),
                      pl.BlockSpec((None, D_FF_EXP, D_MODEL), wsel)],
            out_specs=pl.BlockSpec((TM_MOE, D_MODEL), out_tile)),
        out_shape=jax.ShapeDtypeStruct(xs.shape, F32),
        compiler_params=_cparams(("arbitrary",)),
        name="moe",
    )(tile_expert, n_active, xs, w_eg, w_eu, w_ed)


def _combine_kernel(pos_ref, x1_ref, gates_ref, gf_ref, ys_ref, o_ref, buf_ref, sem):
    def issue(r, carry):
        _row_copy(ys_ref, pos_ref[0, r], buf_ref.at[0], r, sem).start()
        _row_copy(ys_ref, pos_ref[0, TT + r], buf_ref.at[1], r, sem).start()
        return carry

    def drain(r, carry):
        _row_copy(ys_ref, pos_ref[0, r], buf_ref.at[0], r, sem).wait()
        _row_copy(ys_ref, pos_ref[0, TT + r], buf_ref.at[1], r, sem).wait()
        return carry

    lax.fori_loop(0, TT, issue, 0)
    lax.fori_loop(0, TT, drain, 0)
    gates = gates_ref[...]
    y = x1_ref[...] + gates[:, 0:1] * buf_ref[0] + gates[:, 1:2] * buf_ref[1]
    ms = jnp.mean(y * y, axis=-1, keepdims=True)
    o_ref[...] = y * lax.rsqrt(ms + EPS) * gf_ref[...]


def _combine(pos, x1, gates, gf, ys):
    n = x1.shape[0]
    return pl.pallas_call(
        _combine_kernel,
        grid=(n // TT,),
        in_specs=[pl.BlockSpec((None, 1, 2 * TT), lambda i: (i, 0, 0), memory_space=pltpu.SMEM),
                  pl.BlockSpec((TT, D_MODEL), lambda i: (i, 0)),
                  pl.BlockSpec((TT, LANES), lambda i: (i, 0)),
                  pl.BlockSpec((1, D_MODEL), lambda i: (0, 0)),
                  pl.BlockSpec(memory_space=pl.ANY)],
        out_specs=pl.BlockSpec((TT, D_MODEL), lambda i: (i, 0)),
        out_shape=jax.ShapeDtypeStruct((n, D_MODEL), F32),
        scratch_shapes=[pltpu.VMEM((2, TT, D_MODEL), F32),
                        pltpu.SemaphoreType.DMA(())],
        compiler_params=_cparams(("arbitrary",)),
        name="combine",
    )(pos, x1, gates, gf, ys)


def kernel(x, w_in, conv_w, conv_b, w_mq, w_mk, w_mgate, b_mgate, m_norm_g, m_skip, lambda_qk, da_norm_g,
           rel_bias, w_out, norm1_g, norm2_g, w_rg, b_rg, w_re, b_re, w_eg, w_eu, w_ed, normf_g):
    bsz, seq, _ = x.shape
    n = bsz * seq
    assert seq % TQ == 0 and seq % CHUNK == 0 and n % TM_PROJ == 0 and n % TT == 0
    l = 0
    x2 = x.reshape(n, D_MODEL)

    w_in_b = w_in[l].astype(BF16)
    wqv_t = jnp.stack([w_in_b[:, 3 * 512:4 * 512].T, w_in_b[:, 5 * 512:6 * 512].T])
    c, vm, z, qt, kd, vt = _inproj(x2, norm1_g[l][None, :], w_in_b, wqv_t)

    wg = jnp.zeros((3 * D_MLSTM, LANES), F32).at[:, :2 * ML_HEADS].set(w_mgate[l]).astype(BF16)
    bg = jnp.zeros((1, LANES), F32).at[0, :2 * ML_HEADS].set(b_mgate[l])
    y_m = _mlstm(c, vm, z, conv_w[l], conv_b[l][None, :], w_mq[l].astype(BF16), w_mk[l].astype(BF16),
                 wg, bg, m_norm_g[l][None, :], m_skip[l][None, :], bsz, seq)

    y_d = _attn(lambda_qk[l].astype(F32), qt, kd, vt, _rel_bias_tiles(rel_bias),
                da_norm_g[l][:, None], bsz, seq)

    wr = jnp.zeros((D_MODEL, LANES), F32)
    wr = wr.at[:, :N_GROUPS].set(w_rg[l])
    wr = wr.at[:, SUBLANES:SUBLANES + N_EXPERTS].set(w_re[l].reshape(D_MODEL, N_EXPERTS))
    wr_hi = wr.astype(BF16)
    wr_lo = (wr - wr_hi.astype(F32)).astype(BF16)
    wr3 = jnp.concatenate([wr_hi, wr_hi, wr_lo], axis=0)
    br = jnp.zeros((1, LANES), F32)
    br = br.at[0, :N_GROUPS].set(b_rg[l])
    br = br.at[0, SUBLANES:SUBLANES + N_EXPERTS].set(b_re[l].reshape(N_EXPERTS))
    x1, h2, meta, gates, counts = _outproj(x2, y_m, y_d, w_out[l].astype(BF16), norm2_g[l][None, :], wr3, br)

    cnt = counts[:, 0]
    tiles_e = (cnt + TM_MOE - 1) // TM_MOE
    ends_t = jnp.cumsum(tiles_e)
    ends = jnp.concatenate([jnp.zeros((1,), jnp.int32), ends_t * TM_MOE]).astype(jnp.int32)
    offs = ends[:-1]
    n_tiles = (2 * n) // TM_MOE + N_EXPERTS
    n_active = ends_t[-1:].astype(jnp.int32)
    tile_ids = jnp.minimum(jnp.arange(n_tiles, dtype=jnp.int32), n_active[0] - 1)
    tile_expert = jnp.sum(tile_ids[:, None] >= ends_t[None, :], axis=1).astype(jnp.int32)
    eids = jnp.arange(N_EXPERTS, dtype=jnp.int32)[:, None]
    pos1 = jnp.sum(jnp.where(meta[0][None, :] == eids, offs[:, None], 0), axis=0) + meta[2]
    pos2 = jnp.sum(jnp.where(meta[1][None, :] == eids, offs[:, None], 0), axis=0) + meta[3]
    pos = jnp.concatenate([pos1.reshape(n // TT, 1, TT), pos2.reshape(n // TT, 1, TT)], axis=2)

    xs = _dispatch(ends, pos, h2, n_tiles * TM_MOE)
    ys = _moe(tile_expert, n_active, xs, w_eg[l].astype(BF16), w_eu[l].astype(BF16), w_ed[l].astype(BF16))
    out = _combine(pos, x1, gates, normf_g[None, :], ys)
    return out.reshape(bsz, seq, D_MODEL)
```

```python
import functools
import math

import jax
import jax.numpy as jnp
from jax import lax
from jax.experimental import pallas as pl
from jax.experimental.pallas import tpu as pltpu

F32 = jnp.float32
BF16 = jnp.bfloat16

D_MODEL = 1024
D_MLSTM = 512
D_DIFF = 512
ML_HEADS = 4
ML_HD = 128
CONV_W = 4
CHUNK = 128
DA_HEADS = 4
DA_VD = 128
DA_QD = 64
REL_BUCKETS = 32
REL_MAX_DIST = 128
N_GROUPS = 4
EXP_PER_GROUP = 8
N_EXPERTS = 32
D_FF_EXP = 512
EPS = 1e-6
SUBLN_EPS = 1e-5
LAM_INIT = 0.8 - 0.6 * math.exp(-0.3 * 0)
NEG = -1e30

LANES = 128
SUBLANES = 8
MXU_COLS = 256

TM_PROJ = 512
TQ = 512
MLSTM_BATCH = 2
TM_MOE = 256
TT = 512
VMEM_LIMIT = 48 * 1024 * 1024


ROW_TILE = (D_MODEL // LANES, LANES)


def _cparams(sem):
    return pltpu.CompilerParams(dimension_semantics=sem, vmem_limit_bytes=VMEM_LIMIT)


def _to_token_major(x):
    return pltpu.einshape("a(bc)->abc", x, b=ROW_TILE[0])


def _from_token_major(x):
    return pltpu.einshape("abc->a(bc)", x)


def _inproj_kernel(x_ref, g_ref, w_ref, wqv_ref, c_ref, vm_ref, z_ref, q_ref, k_ref, v_ref):
    x = x_ref[...]
    ms = jnp.mean(x * x, axis=-1, keepdims=True)
    h = (x * lax.rsqrt(ms + EPS) * g_ref[...]).astype(BF16)
    outs = (c_ref, vm_ref, z_ref, None, k_ref, None)
    for j, o in enumerate(outs):
        if o is not None:
            r = jnp.dot(h, w_ref[:, j * 512:(j + 1) * 512], preferred_element_type=F32)
            o[...] = r.astype(o.dtype)
    nt = (((1,), (1,)), ((), ()))
    qt = lax.dot_general(wqv_ref[0], h, nt, preferred_element_type=F32)
    q_ref[...] = (qt * (DA_QD ** -0.5)).astype(q_ref.dtype)
    v_ref[...] = lax.dot_general(wqv_ref[1], h, nt, preferred_element_type=F32).astype(v_ref.dtype)


def _inproj(x2, g1, w_in, wqv_t):
    n = x2.shape[0]
    row = lambda i: (i, 0)
    fixed = lambda i: (0, 0)
    sds = lambda dt: jax.ShapeDtypeStruct((n, 512), dt)
    sds_t = jax.ShapeDtypeStruct((512, n), BF16)
    blk = pl.BlockSpec((TM_PROJ, 512), row)
    blk_t = pl.BlockSpec((512, TM_PROJ), lambda i: (0, i))
    return pl.pallas_call(
        _inproj_kernel,
        grid=(n // TM_PROJ,),
        in_specs=[pl.BlockSpec((TM_PROJ, D_MODEL), row),
                  pl.BlockSpec((1, D_MODEL), fixed),
                  pl.BlockSpec((D_MODEL, 6 * 512), fixed),
                  pl.BlockSpec((2, 512, D_MODEL), lambda i: (0, 0, 0))],
        out_specs=[blk, blk, blk, blk_t, blk, blk_t],
        out_shape=[sds(F32), sds(BF16), sds(F32), sds_t, sds(BF16), sds_t],
        compiler_params=_cparams(("parallel",)),
        name="inproj",
    )(x2, g1, w_in, wqv_t)


def _mlstm_kernel(c_ref, vm_ref, z_ref, cw_ref, cb_ref, wq_ref, wk_ref, wg_ref, bg_ref,
                  ng_ref, sk_ref, tril_ref, y_ref, ext_ref, st_ref, m_ref):
    j = pl.program_id(1)

    @pl.when(j == 0)
    def _():
        ext_ref[:, 0:SUBLANES, :] = jnp.zeros((MLSTM_BATCH, SUBLANES, D_MLSTM), F32)
        st_ref[...] = jnp.zeros(st_ref.shape, F32)
        m_ref[...] = jnp.zeros(m_ref.shape, F32)

    for bb in range(MLSTM_BATCH):
        _mlstm_chunk(c_ref.at[bb], vm_ref.at[bb], z_ref.at[bb], cw_ref, cb_ref, wq_ref, wk_ref, wg_ref, bg_ref,
                     ng_ref, sk_ref, tril_ref, y_ref.at[bb], ext_ref.at[bb], st_ref.at[bb], m_ref.at[bb])


def _mlstm_chunk(c_ref, vm_ref, z_ref, cw_ref, cb_ref, wq_ref, wk_ref, wg_ref, bg_ref,
                 ng_ref, sk_ref, tril_ref, y_ref, ext_ref, st_ref, m_ref):
    L = CHUNK

    ext_ref[SUBLANES:SUBLANES + L, :] = c_ref[...]
    cw = cw_ref[...]
    conv = cb_ref[...] + cw[CONV_W - 1:CONV_W, :] * ext_ref[SUBLANES:SUBLANES + L, :]
    for k in range(1, CONV_W):
        conv = conv + cw[CONV_W - 1 - k:CONV_W - k, :] * ext_ref[SUBLANES - k:SUBLANES - k + L, :]
    ext_ref[0:SUBLANES, :] = ext_ref[L:L + SUBLANES, :]
    c_act = conv * jax.nn.sigmoid(conv)

    vm = vm_ref[...]
    qs, ks = [], []
    gates = jnp.dot(vm, wg_ref[2 * D_MLSTM:3 * D_MLSTM, :], preferred_element_type=F32) + bg_ref[...]
    for h in range(ML_HEADS):
        ch = c_act[:, h * ML_HD:(h + 1) * ML_HD].astype(BF16)
        qh = jnp.dot(ch, wq_ref[h], preferred_element_type=F32)
        kh = jnp.dot(ch, wk_ref[h], preferred_element_type=F32)
        qs.append(qh)
        ks.append(kh)
        gates = gates + jnp.dot(qh.astype(BF16), wg_ref[h * ML_HD:(h + 1) * ML_HD, :],
                                preferred_element_type=F32)
        gates = gates + jnp.dot(kh.astype(BF16), wg_ref[D_MLSTM + h * ML_HD:D_MLSTM + (h + 1) * ML_HD, :],
                                preferred_element_type=F32)

    logf = jnp.minimum(gates, 0.0) - jnp.log(1.0 + jnp.exp(-jnp.abs(gates)))
    bcum = jnp.dot(tril_ref[...], logf, preferred_element_type=F32, precision=lax.Precision.HIGHEST)
    gates_t = gates.T
    bcum_t = bcum.T
    row_i = lax.broadcasted_iota(jnp.int32, (L, L), 0)
    col_i = lax.broadcasted_iota(jnp.int32, (L, L), 1)
    causal = col_i <= row_i
    ones_col = jnp.where(lax.broadcasted_iota(jnp.int32, (L, LANES), 1) == 0, 1.0, 0.0).astype(F32)

    for h in range(ML_HEADS):
        q = qs[h].astype(BF16)
        k = (ks[h] * (ML_HD ** -0.5)).astype(BF16)
        v = vm[:, h * ML_HD:(h + 1) * ML_HD]
        i_col = gates[:, h:h + 1]
        i_row = gates_t[h:h + 1, :]
        b_col = bcum[:, ML_HEADS + h:ML_HEADS + h + 1]
        b_row = bcum_t[ML_HEADS + h:ML_HEADS + h + 1, :]
        m_prev = m_ref[h][0:1, 0:1]
        state = st_ref[h]

        inter = b_col + m_prev
        dmat = jnp.where(causal, b_col - b_row + i_row, NEG)
        m_t = jnp.maximum(inter, jnp.max(dmat, axis=-1, keepdims=True))
        dexp = jnp.exp(dmat - m_t)
        s = lax.dot_general(q, k, (((1,), (1,)), ((), ())), preferred_element_type=F32)
        w = (dexp * s).astype(BF16)
        sp = jnp.exp(inter - m_t)
        vaug = jnp.concatenate([v, ones_col.astype(BF16)], axis=1)
        r_inter = jnp.dot(q, state.astype(BF16), preferred_element_type=F32)
        r_intra = jnp.dot(w, vaug, preferred_element_type=F32)
        tot = sp * r_inter + r_intra
        num = tot[:, :ML_HD]
        den = tot[:, ML_HD:ML_HD + 1]
        hh = num / jnp.maximum(jnp.abs(den), jnp.exp(-m_t))

        b_end = b_col[L - 1:L, :]
        g_col = b_end - b_col + i_col
        m_new = jnp.maximum(b_end + m_prev, jnp.max(g_col, axis=0, keepdims=True))
        wk = jnp.exp(g_col - m_new)
        decay = jnp.exp(b_end + m_prev - m_new)
        vw = (jnp.concatenate([v.astype(F32), ones_col], axis=1) * wk).astype(BF16)
        upd = lax.dot_general(k, vw, (((0,), (0,)), ((), ())), preferred_element_type=F32)
        st_ref[h] = decay * state + upd
        m_ref[h] = jnp.broadcast_to(m_new, (SUBLANES, LANES))

        mu = jnp.mean(hh, axis=-1, keepdims=True)
        xc = hh - mu
        var = jnp.mean(xc * xc, axis=-1, keepdims=True)
        sl = slice(h * ML_HD, (h + 1) * ML_HD)
        hn = xc * lax.rsqrt(var + EPS) * ng_ref[:, sl]
        hn = hn + sk_ref[:, sl] * c_act[:, sl]
        o = jax.nn.sigmoid(z_ref[:, sl])
        y_ref[:, sl] = (o * hn).astype(y_ref.dtype)


def _mlstm(c, vm, z, conv_w, conv_b, wq, wk, wg, bg, ng, sk, bsz, seq):
    nc = seq // CHUNK
    f2 = lambda b, j: (0, 0)
    f3 = lambda b, j: (0, 0, 0)
    tril = jnp.tril(jnp.ones((CHUNK, CHUNK), F32))
    blk = pl.BlockSpec((MLSTM_BATCH, CHUNK, D_MLSTM), lambda b, j: (b, j, 0))
    c, vm, z = (a.reshape(bsz, seq, D_MLSTM) for a in (c, vm, z))
    y = pl.pallas_call(
        _mlstm_kernel,
        grid=(bsz // MLSTM_BATCH, nc),
        in_specs=[blk, blk, blk,
                  pl.BlockSpec((CONV_W, D_MLSTM), f2),
                  pl.BlockSpec((1, D_MLSTM), f2),
                  pl.BlockSpec((ML_HEADS, ML_HD, ML_HD), f3),
                  pl.BlockSpec((ML_HEADS, ML_HD, ML_HD), f3),
                  pl.BlockSpec((3 * D_MLSTM, LANES), f2),
                  pl.BlockSpec((1, LANES), f2),
                  pl.BlockSpec((1, D_MLSTM), f2),
                  pl.BlockSpec((1, D_MLSTM), f2),
                  pl.BlockSpec((CHUNK, CHUNK), f2)],
        out_specs=blk,
        out_shape=jax.ShapeDtypeStruct((bsz, seq, D_MLSTM), BF16),
        scratch_shapes=[pltpu.VMEM((MLSTM_BATCH, SUBLANES + CHUNK, D_MLSTM), F32),
                        pltpu.VMEM((MLSTM_BATCH, ML_HEADS, ML_HD, 2 * ML_HD), F32),
                        pltpu.VMEM((MLSTM_BATCH, ML_HEADS, SUBLANES, LANES), F32)],
        compiler_params=_cparams(("parallel", "arbitrary")),
        name="mlstm",
    )(c, vm, z, conv_w, conv_b, wq, wk, wg, bg, ng, sk, tril)
    return y.reshape(bsz * seq, D_MLSTM)


def _attn_kernel(lq_ref, qt_ref, k_ref, vt_ref, bias_ref, g_ref, o_ref, acc_ref, m_ref, l_ref):
    tq = TQ
    i = pl.program_id(2)
    qt = qt_ref[...]
    row = lax.broadcasted_iota(jnp.int32, (LANES, tq), 0)
    zero = jnp.zeros_like(qt)
    q2t = jnp.concatenate([jnp.where(row < DA_QD, qt, zero), jnp.where(row >= DA_QD, qt, zero)], axis=1)
    acc_ref[...] = jnp.zeros(acc_ref.shape, F32)
    m_ref[...] = jnp.full(m_ref.shape, NEG, F32)
    l_ref[...] = jnp.zeros(l_ref.shape, F32)

    def step(kb, bias):
        start = pl.multiple_of(kb * tq, tq)
        k_blk = k_ref[pl.ds(start, tq), :]
        vt_blk = vt_ref[:, pl.ds(start, tq)]
        s = jnp.dot(k_blk, q2t, preferred_element_type=F32)
        if bias is not None:
            s = s + jnp.concatenate([bias, bias], axis=1)
        m_old = m_ref[...]
        m_new = jnp.maximum(m_old, jnp.max(s, axis=0, keepdims=True))
        alpha = jnp.exp(m_old - m_new)
        p = jnp.exp(s - m_new)
        l_ref[...] = alpha * l_ref[...] + jnp.sum(p, axis=0, keepdims=True)
        acc_ref[...] = alpha * acc_ref[...] + jnp.dot(vt_blk, p.astype(BF16), preferred_element_type=F32)
        m_ref[...] = m_new

    def far(kb, carry):
        step(kb, None)
        return carry

    lax.fori_loop(0, i - 1, far, 0)

    @pl.when(i >= 1)
    def _():
        step(i - 1, bias_ref[1])

    step(i, bias_ref[0])

    o = acc_ref[...] * (1.0 / l_ref[...])
    lq = lq_ref[...]
    d01 = jnp.sum(lq[0:1, :] * lq[1:2, :], axis=-1, keepdims=True)
    d23 = jnp.sum(lq[2:3, :] * lq[3:4, :], axis=-1, keepdims=True)
    lam = jnp.exp(d01) - jnp.exp(d23) + LAM_INIT
    y = o[:, :tq] - lam * o[:, tq:]
    y = y * lax.rsqrt(jnp.mean(y * y, axis=0, keepdims=True) + SUBLN_EPS) * g_ref[...]
    o_ref[...] = (y * (1.0 - LAM_INIT)).T.astype(o_ref.dtype)


def _rel_bias_tiles(rel_bias):
    L = TQ
    max_exact = REL_BUCKETS // 2
    n = jnp.arange(-(L - 1), 2 * L, dtype=jnp.int32)
    nn = jnp.maximum(n, 0)
    large = max_exact + (jnp.log(jnp.maximum(nn, 1).astype(F32) / max_exact)
                         / math.log(REL_MAX_DIST / max_exact) * (REL_BUCKETS - max_exact)).astype(jnp.int32)
    large = jnp.minimum(large, REL_BUCKETS - 1)
    bucket = jnp.where(nn < max_exact, nn, large)
    rb = (rel_bias.astype(F32) - rel_bias[REL_BUCKETS - 1].astype(F32)[None, :]).T
    t = jnp.where(n[None, :] >= 0, rb[:, bucket], NEG)

    def toeplitz(v):
        vp = jnp.pad(v, ((0, 0), (0, 1)))
        flat = jnp.tile(vp, (1, L))[:, :L * (2 * L - 1)]
        return flat.reshape(v.shape[0], L, 2 * L - 1)[:, :, L - 1:]

    return jnp.stack([toeplitz(t[:, :2 * L - 1]), toeplitz(t[:, L:])], axis=1)


def _attn(lq, qt, kd, vt, bias, g, bsz, seq):
    nq = seq // TQ
    return pl.pallas_call(
        _attn_kernel,
        grid=(bsz, DA_HEADS, nq),
        in_specs=[pl.BlockSpec((4, DA_QD), lambda b, h, i: (0, 0)),
                  pl.BlockSpec((LANES, TQ), lambda b, h, i: (h, b * nq + i)),
                  pl.BlockSpec((seq, LANES), lambda b, h, i: (b, h)),
                  pl.BlockSpec((LANES, seq), lambda b, h, i: (h, b)),
                  pl.BlockSpec((None, 2, TQ, TQ), lambda b, h, i: (h, 0, 0, 0)),
                  pl.BlockSpec((DA_VD, 1), lambda b, h, i: (0, 0))],
        out_specs=pl.BlockSpec((TQ, LANES), lambda b, h, i: (b * nq + i, h)),
        out_shape=jax.ShapeDtypeStruct((bsz * seq, D_DIFF), BF16),
        scratch_shapes=[pltpu.VMEM((DA_VD, 2 * TQ), F32),
                        pltpu.VMEM((1, 2 * TQ), F32),
                        pltpu.VMEM((1, 2 * TQ), F32)],
        compiler_params=_cparams(("parallel", "parallel", "arbitrary")),
        name="attn",
    )(lq, qt, kd, vt, bias, g)


def _outproj_kernel(x_ref, ym_ref, yd_ref, wo_ref, g2_ref, wr_ref, br_ref, tri_ref,
                    x1_ref, h2_ref, meta_ref, gates_ref, cnt_ref, run_ref):
    tm = TM_PROJ
    step = pl.program_id(0)

    @pl.when(step == 0)
    def _():
        run_ref[...] = jnp.zeros(run_ref.shape, F32)

    x1 = (x_ref[...]
          + jnp.dot(ym_ref[...], wo_ref[0:D_MLSTM, :], preferred_element_type=F32)
          + jnp.dot(yd_ref[...], wo_ref[D_MLSTM:, :], preferred_element_type=F32))
    x1_ref[...] = x1
    ms = jnp.mean(x1 * x1, axis=-1, keepdims=True)
    h2 = x1 * lax.rsqrt(ms + EPS) * g2_ref[...]
    h2_ref[...] = _to_token_major(h2)

    h_hi = h2.astype(BF16)
    h_lo = (h2 - h_hi.astype(F32)).astype(BF16)
    lhs = jnp.concatenate([h_hi, h_lo, h_hi], axis=1)
    logits = jnp.dot(lhs, wr_ref[...], preferred_element_type=F32) + br_ref[...]
    lt = logits.T

    row8 = lax.broadcasted_iota(jnp.int32, (SUBLANES, tm), 0)
    lg = jnp.where(row8 < N_GROUPS, lt[0:SUBLANES], NEG)
    gmax = jnp.max(lg, axis=0, keepdims=True)
    gidx = jnp.min(jnp.where(lg == gmax, row8, SUBLANES), axis=0, keepdims=True)
    pg_sel = 1.0 / jnp.sum(jnp.exp(lg - gmax), axis=0, keepdims=True)
    le = jnp.zeros((EXP_PER_GROUP, tm), F32)
    for g in range(N_GROUPS):
        le = jnp.where(gidx == g, lt[SUBLANES * (g + 1):SUBLANES * (g + 2)], le)
    v1 = jnp.max(le, axis=0, keepdims=True)
    i1 = jnp.min(jnp.where(le == v1, row8, SUBLANES), axis=0, keepdims=True)
    le2 = jnp.where(row8 == i1, NEG, le)
    v2 = jnp.max(le2, axis=0, keepdims=True)
    i2 = jnp.min(jnp.where(le2 == v2, row8, SUBLANES), axis=0, keepdims=True)
    e21 = jnp.exp(v2 - v1)
    pw1 = 1.0 / (1.0 + e21)
    gate1 = pg_sel * pw1
    gate2 = pg_sel * (e21 * pw1)
    eid1 = gidx * EXP_PER_GROUP + i1
    eid2 = gidx * EXP_PER_GROUP + i2

    row32 = lax.broadcasted_iota(jnp.int32, (N_EXPERTS, tm), 0)
    oh1 = row32 == eid1
    oh2 = row32 == eid2
    oh = jnp.where(oh1, 1.0, 0.0) + jnp.where(oh2, 1.0, 0.0)
    before = jnp.dot(oh.astype(BF16), tri_ref[...], preferred_element_type=F32) + run_ref[:, 0:1]
    rank1 = jnp.sum(jnp.where(oh1, before, 0.0), axis=0, keepdims=True).astype(jnp.int32)
    rank2 = jnp.sum(jnp.where(oh2, before, 0.0), axis=0, keepdims=True).astype(jnp.int32)
    run = run_ref[...] + jnp.sum(oh, axis=1, keepdims=True)
    run_ref[...] = run
    cnt_ref[...] = run.astype(jnp.int32)

    meta_ref[...] = jnp.where(row8 == 0, eid1, jnp.where(row8 == 1, eid2,
                              jnp.where(row8 == 2, rank1, jnp.where(row8 == 3, rank2, 0))))
    row128 = lax.broadcasted_iota(jnp.int32, (LANES, tm), 0)
    gt = jnp.where(row128 == 0, gate1, jnp.where(row128 == 1, gate2, 0.0))
    gates_ref[...] = gt.T


def _outproj(x2, ym, yd, wo, g2, wr, br):
    n = x2.shape[0]
    row = lambda i: (i, 0)
    fixed = lambda i: (0, 0)
    tri = jnp.triu(jnp.ones((TM_PROJ, TM_PROJ), F32), k=1).astype(BF16)
    return pl.pallas_call(
        _outproj_kernel,
        grid=(n // TM_PROJ,),
        in_specs=[pl.BlockSpec((TM_PROJ, D_MODEL), row),
                  pl.BlockSpec((TM_PROJ, D_MLSTM), row),
                  pl.BlockSpec((TM_PROJ, D_DIFF), row),
                  pl.BlockSpec((D_MODEL, D_MODEL), fixed),
                  pl.BlockSpec((1, D_MODEL), fixed),
                  pl.BlockSpec((3 * D_MODEL, LANES), fixed),
                  pl.BlockSpec((1, LANES), fixed),
                  pl.BlockSpec((TM_PROJ, TM_PROJ), fixed)],
        out_specs=[pl.BlockSpec((TM_PROJ, D_MODEL), row),
                   pl.BlockSpec((TM_PROJ,) + ROW_TILE, lambda i: (i, 0, 0)),
                   pl.BlockSpec((SUBLANES, TM_PROJ), lambda i: (0, i)),
                   pl.BlockSpec((TM_PROJ, LANES), row),
                   pl.BlockSpec((N_EXPERTS, LANES), fixed)],
        out_shape=[jax.ShapeDtypeStruct((n, D_MODEL), F32),
                   jax.ShapeDtypeStruct((n,) + ROW_TILE, F32),
                   jax.ShapeDtypeStruct((SUBLANES, n), jnp.int32),
                   jax.ShapeDtypeStruct((n, LANES), F32),
                   jax.ShapeDtypeStruct((N_EXPERTS, LANES), jnp.int32)],
        scratch_shapes=[pltpu.VMEM((N_EXPERTS, LANES), F32)],
        compiler_params=_cparams(("arbitrary",)),
        name="outproj_router",
    )(x2, ym, yd, wo, g2, wr, br, tri)


def _row_copy(src_ref, src_row, dst_ref, dst_row, sem):
    return pltpu.make_async_copy(src_ref.at[pl.ds(src_row, 1)], dst_ref.at[pl.ds(dst_row, 1)], sem)


def _dispatch_kernel(ends_ref, pos_ref, h_ref, xs_ref, zero_ref, sem, zsem):
    step = pl.program_id(0)

    @pl.when(step == 0)
    def _():
        zero_ref[...] = jnp.zeros(zero_ref.shape, F32)

        def tile_copy(e):
            start = pl.multiple_of(ends_ref[e + 1] - TM_MOE, TM_MOE)
            return pltpu.make_async_copy(zero_ref, xs_ref.at[pl.ds(start, TM_MOE)], zsem)

        def zstart(e, carry):
            @pl.when(ends_ref[e + 1] > ends_ref[e])
            def _():
                tile_copy(e).start()
            return carry

        def zwait(e, carry):
            @pl.when(ends_ref[e + 1] > ends_ref[e])
            def _():
                tile_copy(e).wait()
            return carry

        lax.fori_loop(0, N_EXPERTS, zstart, 0)
        lax.fori_loop(0, N_EXPERTS, zwait, 0)

        def tail_copy(t):
            return pltpu.make_async_copy(zero_ref, xs_ref.at[pl.ds(pl.multiple_of(t * TM_MOE, TM_MOE), TM_MOE)], zsem)

        def tstart(t, carry):
            tail_copy(t).start()
            return carry

        def twait(t, carry):
            tail_copy(t).wait()
            return carry

        first_tail = ends_ref[N_EXPERTS] // TM_MOE
        lax.fori_loop(first_tail, xs_ref.shape[0] // TM_MOE, tstart, 0)
        lax.fori_loop(first_tail, xs_ref.shape[0] // TM_MOE, twait, 0)

    def issue(r, carry):
        _row_copy(h_ref, r, xs_ref, pos_ref[0, r], sem).start()
        _row_copy(h_ref, r, xs_ref, pos_ref[0, TT + r], sem).start()
        return carry

    def drain(r, carry):
        _row_copy(h_ref, r, xs_ref, pos_ref[0, r], sem).wait()
        _row_copy(h_ref, r, xs_ref, pos_ref[0, TT + r], sem).wait()
        return carry

    lax.fori_loop(0, TT, issue, 0)
    lax.fori_loop(0, TT, drain, 0)


def _dispatch(ends, pos, h2, n_rows):
    n = h2.shape[0]
    return pl.pallas_call(
        _dispatch_kernel,
        grid_spec=pltpu.PrefetchScalarGridSpec(
            num_scalar_prefetch=1,
            grid=(n // TT,),
            in_specs=[pl.BlockSpec((None, 1, 2 * TT), lambda i, ends: (i, 0, 0), memory_space=pltpu.SMEM),
                      pl.BlockSpec((TT,) + ROW_TILE, lambda i, ends: (i, 0, 0))],
            out_specs=pl.BlockSpec(memory_space=pl.ANY),
            scratch_shapes=[pltpu.VMEM((TM_MOE,) + ROW_TILE, F32),
                            pltpu.SemaphoreType.DMA(()),
                            pltpu.SemaphoreType.DMA(())]),
        out_shape=jax.ShapeDtypeStruct((n_rows,) + ROW_TILE, F32),
        compiler_params=_cparams(("arbitrary",)),
        name="dispatch",
    )(ends, pos, h2)


def _moe_kernel(texp_ref, nact_ref, x_ref, wg_ref, wu_ref, wd_ref, y_ref):
    t = pl.program_id(0)

    @pl.when(t < nact_ref[0])
    def _():
        x = _from_token_major(x_ref[...]).astype(BF16)
        g = jnp.dot(x, wg_ref[...], preferred_element_type=F32)
        u = jnp.dot(x, wu_ref[...], preferred_element_type=F32)
        hid = (g * jax.nn.sigmoid(g) * u).astype(BF16)
        y_ref[...] = _to_token_major(jnp.dot(hid, wd_ref[...], preferred_element_type=F32))

    @pl.when(t >= nact_ref[0])
    def _():
        y_ref[...] = jnp.zeros(y_ref.shape, F32)


def _moe(tile_expert, n_active, xs, w_eg, w_eu, w_ed):
    n_tiles = xs.shape[0] // TM_MOE
    tile = lambda t, te, na: (jnp.minimum(t, na[0] - 1), 0, 0)
    out_tile = lambda t, te, na: (t, 0, 0)
    wsel = lambda t, te, na: (te[t], 0, 0)
    return pl.pallas_call(
        _moe_kernel,
        grid_spec=pltpu.PrefetchScalarGridSpec(
            num_scalar_prefetch=2,
            grid=(n_tiles,),
            in_specs=[pl.BlockSpec((TM_MOE,) + ROW_TILE, tile),
                      pl.BlockSpec((None, D_MODEL, D_FF_EXP), wsel),
                      pl.BlockSpec((None, D_MODEL, D_FF_EXP), wsel),
                      pl.BlockSpec((None, D_FF_EXP, D_MODEL), wsel)],
            out_specs=pl.BlockSpec((TM_MOE,) + ROW_TILE, out_tile)),
        out_shape=jax.ShapeDtypeStruct(xs.shape, F32),
        compiler_params=_cparams(("arbitrary",)),
        name="moe",
    )(tile_expert, n_active, xs, w_eg, w_eu, w_ed)


def _combine_kernel(pos_ref, x1_ref, gates_ref, gf_ref, ys_ref, o_ref, buf_ref, sem):
    def issue(r, carry):
        _row_copy(ys_ref, pos_ref[0, r], buf_ref.at[0], r, sem).start()
        _row_copy(ys_ref, pos_ref[0, TT + r], buf_ref.at[1], r, sem).start()
        return carry

    def drain(r, carry):
        _row_copy(ys_ref, pos_ref[0, r], buf_ref.at[0], r, sem).wait()
        _row_copy(ys_ref, pos_ref[0, TT + r], buf_ref.at[1], r, sem).wait()
        return carry

    lax.fori_loop(0, TT, issue, 0)
    lax.fori_loop(0, TT, drain, 0)
    gates = gates_ref[...]
    y = (x1_ref[...] + gates[:, 0:1] * _from_token_major(buf_ref[0])
         + gates[:, 1:2] * _from_token_major(buf_ref[1]))
    ms = jnp.mean(y * y, axis=-1, keepdims=True)
    o_ref[...] = y * lax.rsqrt(ms + EPS) * gf_ref[...]


def _combine(pos, x1, gates, gf, ys):
    n = x1.shape[0]
    return pl.pallas_call(
        _combine_kernel,
        grid=(n // TT,),
        in_specs=[pl.BlockSpec((None, 1, 2 * TT), lambda i: (i, 0, 0), memory_space=pltpu.SMEM),
                  pl.BlockSpec((TT, D_MODEL), lambda i: (i, 0)),
                  pl.BlockSpec((TT, LANES), lambda i: (i, 0)),
                  pl.BlockSpec((1, D_MODEL), lambda i: (0, 0)),
                  pl.BlockSpec(memory_space=pl.ANY)],
        out_specs=pl.BlockSpec((TT, D_MODEL), lambda i: (i, 0)),
        out_shape=jax.ShapeDtypeStruct((n, D_MODEL), F32),
        scratch_shapes=[pltpu.VMEM((2, TT) + ROW_TILE, F32),
                        pltpu.SemaphoreType.DMA(())],
        compiler_params=_cparams(("arbitrary",)),
        name="combine",
    )(pos, x1, gates, gf, ys)


def kernel(x, w_in, conv_w, conv_b, w_mq, w_mk, w_mgate, b_mgate, m_norm_g, m_skip, lambda_qk, da_norm_g,
           rel_bias, w_out, norm1_g, norm2_g, w_rg, b_rg, w_re, b_re, w_eg, w_eu, w_ed, normf_g):
    bsz, seq, _ = x.shape
    n = bsz * seq
    assert seq % TQ == 0 and seq % CHUNK == 0 and n % TM_PROJ == 0 and n % TT == 0
    l = 0
    x2 = x.reshape(n, D_MODEL)

    w_in_b = w_in[l].astype(BF16)
    wqv_t = jnp.stack([w_in_b[:, 3 * 512:4 * 512].T, w_in_b[:, 5 * 512:6 * 512].T])
    c, vm, z, qt, kd, vt = _inproj(x2, norm1_g[l][None, :], w_in_b, wqv_t)

    wg = jnp.zeros((3 * D_MLSTM, LANES), F32).at[:, :2 * ML_HEADS].set(w_mgate[l]).astype(BF16)
    bg = jnp.zeros((1, LANES), F32).at[0, :2 * ML_HEADS].set(b_mgate[l])
    y_m = _mlstm(c, vm, z, conv_w[l], conv_b[l][None, :], w_mq[l].astype(BF16), w_mk[l].astype(BF16),
                 wg, bg, m_norm_g[l][None, :], m_skip[l][None, :], bsz, seq)

    y_d = _attn(lambda_qk[l].astype(F32), qt, kd, vt, _rel_bias_tiles(rel_bias),
                da_norm_g[l][:, None], bsz, seq)

    wr = jnp.zeros((D_MODEL, LANES), F32)
    wr = wr.at[:, :N_GROUPS].set(w_rg[l])
    wr = wr.at[:, SUBLANES:SUBLANES + N_EXPERTS].set(w_re[l].reshape(D_MODEL, N_EXPERTS))
    wr_hi = wr.astype(BF16)
    wr_lo = (wr - wr_hi.astype(F32)).astype(BF16)
    wr3 = jnp.concatenate([wr_hi, wr_hi, wr_lo], axis=0)
    br = jnp.zeros((1, LANES), F32)
    br = br.at[0, :N_GROUPS].set(b_rg[l])
    br = br.at[0, SUBLANES:SUBLANES + N_EXPERTS].set(b_re[l].reshape(N_EXPERTS))
    x1, h2, meta, gates, counts = _outproj(x2, y_m, y_d, w_out[l].astype(BF16), norm2_g[l][None, :], wr3, br)

    cnt = counts[:, 0]
    tiles_e = (cnt + TM_MOE - 1) // TM_MOE
    ends_t = jnp.cumsum(tiles_e)
    ends = jnp.concatenate([jnp.zeros((1,), jnp.int32), ends_t * TM_MOE]).astype(jnp.int32)
    offs = ends[:-1]
    n_tiles = (2 * n) // TM_MOE + N_EXPERTS
    n_active = ends_t[-1:].astype(jnp.int32)
    tile_ids = jnp.minimum(jnp.arange(n_tiles, dtype=jnp.int32), n_active[0] - 1)
    tile_expert = jnp.sum(tile_ids[:, None] >= ends_t[None, :], axis=1).astype(jnp.int32)
    eids = jnp.arange(N_EXPERTS, dtype=jnp.int32)[:, None]
    pos1 = jnp.sum(jnp.where(meta[0][None, :] == eids, offs[:, None], 0), axis=0) + meta[2]
    pos2 = jnp.sum(jnp.where(meta[1][None, :] == eids, offs[:, None], 0), axis=0) + meta[3]
    pos = jnp.concatenate([pos1.reshape(n // TT, 1, TT), pos2.reshape(n // TT, 1, TT)], axis=2)

    xs = _dispatch(ends, pos, h2, n_tiles * TM_MOE)
    ys = _moe(tile_expert, n_active, xs, w_eg[l].astype(BF16), w_eu[l].astype(BF16), w_ed[l].astype(BF16))
    out = _combine(pos, x1, gates, normf_g[None, :], ys)
    return out.reshape(bsz, seq, D_MODEL)
```

```python
import functools
import math

import jax
import jax.numpy as jnp
from jax import lax
from jax.experimental import pallas as pl
from jax.experimental.pallas import tpu as pltpu

F32 = jnp.float32
BF16 = jnp.bfloat16

D_MODEL = 1024
D_MLSTM = 512
D_DIFF = 512
ML_HEADS = 4
ML_HD = 128
CONV_W = 4
CHUNK = 128
DA_HEADS = 4
DA_VD = 128
DA_QD = 64
REL_BUCKETS = 32
REL_MAX_DIST = 128
N_GROUPS = 4
EXP_PER_GROUP = 8
N_EXPERTS = 32
D_FF_EXP = 512
EPS = 1e-6
SUBLN_EPS = 1e-5
LAM_INIT = 0.8 - 0.6 * math.exp(-0.3 * 0)
NEG = -1e30
LOG2E = math.log2(math.e)

LANES = 128
SUBLANES = 8
MXU_COLS = 256

TM_PROJ = 512
TQ = 512
ACC_ROWS = DA_VD + 16
MLSTM_BATCH = 2
TM_MOE = 512
MOE_SUBTILES = 2
TT = 512
VMEM_LIMIT = 48 * 1024 * 1024


ROW_TILE = (D_MODEL // LANES, LANES)


def _cparams(sem):
    return pltpu.CompilerParams(dimension_semantics=sem, vmem_limit_bytes=VMEM_LIMIT)


def _to_token_major(x):
    return pltpu.einshape("a(bc)->abc", x, b=ROW_TILE[0])


def _from_token_major(x):
    return pltpu.einshape("abc->a(bc)", x)


def _inproj_kernel(x_ref, g_ref, w_ref, wqv_ref, c_ref, vm_ref, z_ref, q_ref, k_ref, v_ref):
    x = x_ref[...]
    ms = jnp.mean(x * x, axis=-1, keepdims=True)
    h = (x * lax.rsqrt(ms + EPS) * g_ref[...]).astype(BF16)
    outs = (c_ref, vm_ref, z_ref, None, k_ref, None)
    for j, o in enumerate(outs):
        if o is not None:
            r = jnp.dot(h, w_ref[:, j * 512:(j + 1) * 512], preferred_element_type=F32)
            o[...] = r.astype(o.dtype)
    nt = (((1,), (1,)), ((), ()))
    qt = lax.dot_general(wqv_ref[0], h, nt, preferred_element_type=F32)
    q_ref[...] = (qt * (DA_QD ** -0.5 * LOG2E)).astype(q_ref.dtype)
    v_ref[...] = lax.dot_general(wqv_ref[1], h, nt, preferred_element_type=F32).astype(v_ref.dtype)


def _inproj(x2, g1, w_in, wqv_t):
    n = x2.shape[0]
    row = lambda i: (i, 0)
    fixed = lambda i: (0, 0)
    sds = lambda dt: jax.ShapeDtypeStruct((n, 512), dt)
    sds_t = jax.ShapeDtypeStruct((512, n), BF16)
    blk = pl.BlockSpec((TM_PROJ, 512), row)
    blk_t = pl.BlockSpec((512, TM_PROJ), lambda i: (0, i))
    return pl.pallas_call(
        _inproj_kernel,
        grid=(n // TM_PROJ,),
        in_specs=[pl.BlockSpec((TM_PROJ, D_MODEL), row),
                  pl.BlockSpec((1, D_MODEL), fixed),
                  pl.BlockSpec((D_MODEL, 6 * 512), fixed),
                  pl.BlockSpec((2, 512, D_MODEL), lambda i: (0, 0, 0))],
        out_specs=[blk, blk, blk, blk_t, blk, blk_t],
        out_shape=[sds(F32), sds(BF16), sds(F32), sds_t, sds(BF16), sds_t],
        compiler_params=_cparams(("parallel",)),
        name="inproj",
    )(x2, g1, w_in, wqv_t)


def _mlstm_kernel(c_ref, vm_ref, z_ref, cw_ref, cb_ref, wq_ref, wk_ref, wg_ref, bg_ref,
                  ng_ref, sk_ref, tril_ref, y_ref, ext_ref, st_ref, m_ref):
    j = pl.program_id(1)

    @pl.when(j == 0)
    def _():
        ext_ref[:, 0:SUBLANES, :] = jnp.zeros((MLSTM_BATCH, SUBLANES, D_MLSTM), F32)
        st_ref[...] = jnp.zeros(st_ref.shape, F32)
        m_ref[...] = jnp.zeros(m_ref.shape, F32)

    for bb in range(MLSTM_BATCH):
        _mlstm_chunk(c_ref.at[bb], vm_ref.at[bb], z_ref.at[bb], cw_ref, cb_ref, wq_ref, wk_ref, wg_ref, bg_ref,
                     ng_ref, sk_ref, tril_ref, y_ref.at[bb], ext_ref.at[bb], st_ref.at[bb], m_ref.at[bb])


def _mlstm_chunk(c_ref, vm_ref, z_ref, cw_ref, cb_ref, wq_ref, wk_ref, wg_ref, bg_ref,
                 ng_ref, sk_ref, tril_ref, y_ref, ext_ref, st_ref, m_ref):
    L = CHUNK

    ext_ref[SUBLANES:SUBLANES + L, :] = c_ref[...]
    cw = cw_ref[...]
    conv = cb_ref[...] + cw[CONV_W - 1:CONV_W, :] * ext_ref[SUBLANES:SUBLANES + L, :]
    for k in range(1, CONV_W):
        conv = conv + cw[CONV_W - 1 - k:CONV_W - k, :] * ext_ref[SUBLANES - k:SUBLANES - k + L, :]
    ext_ref[0:SUBLANES, :] = ext_ref[L:L + SUBLANES, :]
    c_act = conv * jax.nn.sigmoid(conv)

    vm = vm_ref[...]
    qs, ks = [], []
    gates = jnp.dot(vm, wg_ref[2 * D_MLSTM:3 * D_MLSTM, :], preferred_element_type=F32) + bg_ref[...]
    for h in range(ML_HEADS):
        ch = c_act[:, h * ML_HD:(h + 1) * ML_HD].astype(BF16)
        qh = jnp.dot(ch, wq_ref[h], preferred_element_type=F32)
        kh = jnp.dot(ch, wk_ref[h], preferred_element_type=F32)
        qs.append(qh)
        ks.append(kh)
        gates = gates + jnp.dot(qh.astype(BF16), wg_ref[h * ML_HD:(h + 1) * ML_HD, :],
                                preferred_element_type=F32)
        gates = gates + jnp.dot(kh.astype(BF16), wg_ref[D_MLSTM + h * ML_HD:D_MLSTM + (h + 1) * ML_HD, :],
                                preferred_element_type=F32)

    logf = jnp.minimum(gates, 0.0) - jnp.log(1.0 + jnp.exp(-jnp.abs(gates)))
    bcum = jnp.dot(tril_ref[...], logf, preferred_element_type=F32, precision=lax.Precision.HIGHEST)
    gates_t = gates.T
    bcum_t = bcum.T
    row_i = lax.broadcasted_iota(jnp.int32, (L, L), 0)
    col_i = lax.broadcasted_iota(jnp.int32, (L, L), 1)
    causal = col_i <= row_i
    ones_col = jnp.where(lax.broadcasted_iota(jnp.int32, (L, LANES), 1) == 0, 1.0, 0.0).astype(F32)

    for h in range(ML_HEADS):
        q = qs[h].astype(BF16)
        k = (ks[h] * (ML_HD ** -0.5)).astype(BF16)
        v = vm[:, h * ML_HD:(h + 1) * ML_HD]
        i_col = gates[:, h:h + 1]
        i_row = gates_t[h:h + 1, :]
        b_col = bcum[:, ML_HEADS + h:ML_HEADS + h + 1]
        b_row = bcum_t[ML_HEADS + h:ML_HEADS + h + 1, :]
        m_prev = m_ref[h][0:1, 0:1]
        state = st_ref[h]

        inter = b_col + m_prev
        dmat = jnp.where(causal, b_col - b_row + i_row, NEG)
        m_t = jnp.maximum(inter, jnp.max(dmat, axis=-1, keepdims=True))
        dexp = jnp.exp(dmat - m_t)
        s = lax.dot_general(q, k, (((1,), (1,)), ((), ())), preferred_element_type=F32)
        w = (dexp * s).astype(BF16)
        sp = jnp.exp(inter - m_t)
        vaug = jnp.concatenate([v, ones_col.astype(BF16)], axis=1)
        r_inter = jnp.dot(q, state.astype(BF16), preferred_element_type=F32)
        r_intra = jnp.dot(w, vaug, preferred_element_type=F32)
        tot = sp * r_inter + r_intra
        num = tot[:, :ML_HD]
        den = tot[:, ML_HD:ML_HD + 1]
        hh = num / jnp.maximum(jnp.abs(den), jnp.exp(-m_t))

        b_end = b_col[L - 1:L, :]
        g_col = b_end - b_col + i_col
        m_new = jnp.maximum(b_end + m_prev, jnp.max(g_col, axis=0, keepdims=True))
        wk = jnp.exp(g_col - m_new)
        decay = jnp.exp(b_end + m_prev - m_new)
        vw = (jnp.concatenate([v.astype(F32), ones_col], axis=1) * wk).astype(BF16)
        upd = lax.dot_general(k, vw, (((0,), (0,)), ((), ())), preferred_element_type=F32)
        st_ref[h] = decay * state + upd
        m_ref[h] = jnp.broadcast_to(m_new, (SUBLANES, LANES))

        mu = jnp.mean(hh, axis=-1, keepdims=True)
        xc = hh - mu
        var = jnp.mean(xc * xc, axis=-1, keepdims=True)
        sl = slice(h * ML_HD, (h + 1) * ML_HD)
        hn = xc * lax.rsqrt(var + EPS) * ng_ref[:, sl]
        hn = hn + sk_ref[:, sl] * c_act[:, sl]
        o = jax.nn.sigmoid(z_ref[:, sl])
        y_ref[:, sl] = (o * hn).astype(y_ref.dtype)


def _mlstm(c, vm, z, conv_w, conv_b, wq, wk, wg, bg, ng, sk, bsz, seq):
    nc = seq // CHUNK
    f2 = lambda b, j: (0, 0)
    f3 = lambda b, j: (0, 0, 0)
    tril = jnp.tril(jnp.ones((CHUNK, CHUNK), F32))
    blk = pl.BlockSpec((MLSTM_BATCH, CHUNK, D_MLSTM), lambda b, j: (b, j, 0))
    c, vm, z = (a.reshape(bsz, seq, D_MLSTM) for a in (c, vm, z))
    y = pl.pallas_call(
        _mlstm_kernel,
        grid=(bsz // MLSTM_BATCH, nc),
        in_specs=[blk, blk, blk,
                  pl.BlockSpec((CONV_W, D_MLSTM), f2),
                  pl.BlockSpec((1, D_MLSTM), f2),
                  pl.BlockSpec((ML_HEADS, ML_HD, ML_HD), f3),
                  pl.BlockSpec((ML_HEADS, ML_HD, ML_HD), f3),
                  pl.BlockSpec((3 * D_MLSTM, LANES), f2),
                  pl.BlockSpec((1, LANES), f2),
                  pl.BlockSpec((1, D_MLSTM), f2),
                  pl.BlockSpec((1, D_MLSTM), f2),
                  pl.BlockSpec((CHUNK, CHUNK), f2)],
        out_specs=blk,
        out_shape=jax.ShapeDtypeStruct((bsz, seq, D_MLSTM), BF16),
        scratch_shapes=[pltpu.VMEM((MLSTM_BATCH, SUBLANES + CHUNK, D_MLSTM), F32),
                        pltpu.VMEM((MLSTM_BATCH, ML_HEADS, ML_HD, 2 * ML_HD), F32),
                        pltpu.VMEM((MLSTM_BATCH, ML_HEADS, SUBLANES, LANES), F32)],
        compiler_params=_cparams(("parallel", "arbitrary")),
        name="mlstm",
    )(c, vm, z, conv_w, conv_b, wq, wk, wg, bg, ng, sk, tril)
    return y.reshape(bsz * seq, D_MLSTM)


def _attn_kernel(lq_ref, qt_ref, k_ref, vt_ref, bias_ref, g_ref, o_ref,
                 acc_ref, m_ref, q2_ref, s_ref, p_ref, a_ref):
    tq = TQ
    i = pl.program_id(2)
    qt = qt_ref[...]
    row = lax.broadcasted_iota(jnp.int32, (LANES, tq), 0)
    zero = jnp.zeros_like(qt)
    q2_ref[:, 0:tq] = jnp.where(row < DA_QD, qt, zero)
    q2_ref[:, tq:2 * tq] = jnp.where(row >= DA_QD, qt, zero)
    acc_ref[...] = jnp.zeros(acc_ref.shape, F32)
    m_ref[...] = jnp.full(m_ref.shape, NEG, F32)

    p_ref[1] = jnp.zeros(p_ref.shape[1:], BF16)
    a_ref[1] = jnp.ones(a_ref.shape[1:], F32)

    def logits(kb, buf):
        start = pl.multiple_of(kb * tq, tq)
        s_ref[buf] = jnp.dot(k_ref[pl.ds(start, tq), :], q2_ref[...], preferred_element_type=F32)

    ones_rows = jnp.ones((ACC_ROWS - DA_VD, tq), BF16)

    def values(kb, buf):
        start = pl.multiple_of(jnp.maximum(kb, 0) * tq, tq)
        lhs = jnp.concatenate([vt_ref[:, pl.ds(start, tq)], ones_rows], axis=0)
        pv = jnp.dot(lhs, p_ref[buf], preferred_element_type=F32)
        acc_ref[...] = a_ref[buf] * acc_ref[...] + pv

    def softmax(buf, which_bias):
        s = s_ref[buf]
        if which_bias is not None:
            bias = bias_ref[which_bias]
            s = s + jnp.concatenate([bias, bias], axis=1)
        m_old = m_ref[...]
        m_new = jnp.maximum(m_old, jnp.max(s, axis=0, keepdims=True))
        m_ref[...] = m_new
        a_ref[buf] = jnp.exp2(m_old - m_new)
        p_ref[buf] = jnp.exp2(s - m_new).astype(BF16)

    def stage(kb, buf, has_next, which_bias):
        softmax(buf, which_bias)
        if has_next:
            logits(kb + 1, 1 - buf)
        values(kb - 1, 1 - buf)

    logits(0, 0)
    n_far = jnp.maximum(i - 1, 0)

    def far_pair(t, carry):
        stage(2 * t, 0, True, None)
        stage(2 * t + 1, 1, True, None)
        return carry

    lax.fori_loop(0, n_far // 2, far_pair, 0)
    kb0 = 2 * (n_far // 2)

    @pl.when(i == 0)
    def _():
        stage(0, 0, False, 0)
        values(0, 0)

    @pl.when(jnp.logical_and(i >= 1, n_far % 2 == 0))
    def _():
        stage(kb0, 0, True, 1)
        stage(kb0 + 1, 1, False, 0)
        values(kb0 + 1, 1)

    @pl.when(n_far % 2 == 1)
    def _():
        stage(kb0, 0, True, None)
        stage(kb0 + 1, 1, True, 1)
        stage(kb0 + 2, 0, False, 0)
        values(kb0 + 2, 0)

    o = acc_ref[0:DA_VD, :] * (1.0 / acc_ref[DA_VD:DA_VD + 1, :])
    lq = lq_ref[...]
    d01 = jnp.sum(lq[0:1, :] * lq[1:2, :], axis=-1, keepdims=True)
    d23 = jnp.sum(lq[2:3, :] * lq[3:4, :], axis=-1, keepdims=True)
    lam = jnp.exp(d01) - jnp.exp(d23) + LAM_INIT
    y = o[:, :tq] - lam * o[:, tq:]
    y = y * lax.rsqrt(jnp.mean(y * y, axis=0, keepdims=True) + SUBLN_EPS) * g_ref[...]
    o_ref[...] = (y * (1.0 - LAM_INIT)).T.astype(o_ref.dtype)


def _rel_bias_tiles(rel_bias):
    L = TQ
    max_exact = REL_BUCKETS // 2
    n = jnp.arange(-(L - 1), 2 * L, dtype=jnp.int32)
    nn = jnp.maximum(n, 0)
    large = max_exact + (jnp.log(jnp.maximum(nn, 1).astype(F32) / max_exact)
                         / math.log(REL_MAX_DIST / max_exact) * (REL_BUCKETS - max_exact)).astype(jnp.int32)
    large = jnp.minimum(large, REL_BUCKETS - 1)
    bucket = jnp.where(nn < max_exact, nn, large)
    rb = (rel_bias.astype(F32) - rel_bias[REL_BUCKETS - 1].astype(F32)[None, :]).T
    t = jnp.where(n[None, :] >= 0, rb[:, bucket] * LOG2E, NEG)

    def toeplitz(v):
        vp = jnp.pad(v, ((0, 0), (0, 1)))
        flat = jnp.tile(vp, (1, L))[:, :L * (2 * L - 1)]
        return flat.reshape(v.shape[0], L, 2 * L - 1)[:, :, L - 1:]

    return jnp.stack([toeplitz(t[:, :2 * L - 1]), toeplitz(t[:, L:])], axis=1)


def _attn(lq, qt, kd, vt, bias, g, bsz, seq):
    nq = seq // TQ
    return pl.pallas_call(
        _attn_kernel,
        grid=(bsz, DA_HEADS, nq),
        in_specs=[pl.BlockSpec((4, DA_QD), lambda b, h, i: (0, 0)),
                  pl.BlockSpec((LANES, TQ), lambda b, h, i: (h, b * nq + i)),
                  pl.BlockSpec((seq, LANES), lambda b, h, i: (b, h)),
                  pl.BlockSpec((LANES, seq), lambda b, h, i: (h, b)),
                  pl.BlockSpec((None, 2, TQ, TQ), lambda b, h, i: (h, 0, 0, 0)),
                  pl.BlockSpec((DA_VD, 1), lambda b, h, i: (0, 0))],
        out_specs=pl.BlockSpec((TQ, LANES), lambda b, h, i: (b * nq + i, h)),
        out_shape=jax.ShapeDtypeStruct((bsz * seq, D_DIFF), BF16),
        scratch_shapes=[pltpu.VMEM((ACC_ROWS, 2 * TQ), F32),
                        pltpu.VMEM((1, 2 * TQ), F32),
                        pltpu.VMEM((LANES, 2 * TQ), BF16),
                        pltpu.VMEM((2, TQ, 2 * TQ), F32),
                        pltpu.VMEM((2, TQ, 2 * TQ), BF16),
                        pltpu.VMEM((2, 1, 2 * TQ), F32)],
        compiler_params=_cparams(("parallel", "parallel", "arbitrary")),
        name="attn",
    )(lq, qt, kd, vt, bias, g)


def _outproj_kernel(x_ref, ym_ref, yd_ref, wo_ref, g2_ref, wr_ref, br_ref, tri_ref,
                    x1_ref, h2_ref, meta_ref, gates_ref, cnt_ref, run_ref):
    tm = TM_PROJ
    step = pl.program_id(0)

    @pl.when(step == 0)
    def _():
        run_ref[...] = jnp.zeros(run_ref.shape, F32)

    x1 = (x_ref[...]
          + jnp.dot(ym_ref[...], wo_ref[0:D_MLSTM, :], preferred_element_type=F32)
          + jnp.dot(yd_ref[...], wo_ref[D_MLSTM:, :], preferred_element_type=F32))
    x1_ref[...] = x1
    ms = jnp.mean(x1 * x1, axis=-1, keepdims=True)
    h2 = x1 * lax.rsqrt(ms + EPS) * g2_ref[...]
    h2_ref[...] = _to_token_major(h2)

    h_hi = h2.astype(BF16)
    h_lo = (h2 - h_hi.astype(F32)).astype(BF16)
    lhs = jnp.concatenate([h_hi, h_lo, h_hi], axis=1)
    logits = jnp.dot(lhs, wr_ref[...], preferred_element_type=F32) + br_ref[...]
    lt = logits.T

    row8 = lax.broadcasted_iota(jnp.int32, (SUBLANES, tm), 0)
    lg = jnp.where(row8 < N_GROUPS, lt[0:SUBLANES], NEG)
    gmax = jnp.max(lg, axis=0, keepdims=True)
    gidx = jnp.min(jnp.where(lg == gmax, row8, SUBLANES), axis=0, keepdims=True)
    pg_sel = 1.0 / jnp.sum(jnp.exp(lg - gmax), axis=0, keepdims=True)
    le = jnp.zeros((EXP_PER_GROUP, tm), F32)
    for g in range(N_GROUPS):
        le = jnp.where(gidx == g, lt[SUBLANES * (g + 1):SUBLANES * (g + 2)], le)
    v1 = jnp.max(le, axis=0, keepdims=True)
    i1 = jnp.min(jnp.where(le == v1, row8, SUBLANES), axis=0, keepdims=True)
    le2 = jnp.where(row8 == i1, NEG, le)
    v2 = jnp.max(le2, axis=0, keepdims=True)
    i2 = jnp.min(jnp.where(le2 == v2, row8, SUBLANES), axis=0, keepdims=True)
    e21 = jnp.exp(v2 - v1)
    pw1 = 1.0 / (1.0 + e21)
    gate1 = pg_sel * pw1
    gate2 = pg_sel * (e21 * pw1)
    eid1 = gidx * EXP_PER_GROUP + i1
    eid2 = gidx * EXP_PER_GROUP + i2

    row32 = lax.broadcasted_iota(jnp.int32, (N_EXPERTS, tm), 0)
    oh1 = row32 == eid1
    oh2 = row32 == eid2
    oh = jnp.where(oh1, 1.0, 0.0) + jnp.where(oh2, 1.0, 0.0)
    before = jnp.dot(oh.astype(BF16), tri_ref[...], preferred_element_type=F32) + run_ref[:, 0:1]
    rank1 = jnp.sum(jnp.where(oh1, before, 0.0), axis=0, keepdims=True).astype(jnp.int32)
    rank2 = jnp.sum(jnp.where(oh2, before, 0.0), axis=0, keepdims=True).astype(jnp.int32)
    run = run_ref[...] + jnp.sum(oh, axis=1, keepdims=True)
    run_ref[...] = run
    cnt_ref[...] = run.astype(jnp.int32)

    meta_ref[...] = jnp.where(row8 == 0, eid1, jnp.where(row8 == 1, eid2,
                              jnp.where(row8 == 2, rank1, jnp.where(row8 == 3, rank2, 0))))
    row128 = lax.broadcasted_iota(jnp.int32, (LANES, tm), 0)
    gt = jnp.where(row128 == 0, gate1, jnp.where(row128 == 1, gate2, 0.0))
    gates_ref[...] = gt.T


def _outproj(x2, ym, yd, wo, g2, wr, br):
    n = x2.shape[0]
    row = lambda i: (i, 0)
    fixed = lambda i: (0, 0)
    tri = jnp.triu(jnp.ones((TM_PROJ, TM_PROJ), F32), k=1).astype(BF16)
    return pl.pallas_call(
        _outproj_kernel,
        grid=(n // TM_PROJ,),
        in_specs=[pl.BlockSpec((TM_PROJ, D_MODEL), row),
                  pl.BlockSpec((TM_PROJ, D_MLSTM), row),
                  pl.BlockSpec((TM_PROJ, D_DIFF), row),
                  pl.BlockSpec((D_MODEL, D_MODEL), fixed),
                  pl.BlockSpec((1, D_MODEL), fixed),
                  pl.BlockSpec((3 * D_MODEL, LANES), fixed),
                  pl.BlockSpec((1, LANES), fixed),
                  pl.BlockSpec((TM_PROJ, TM_PROJ), fixed)],
        out_specs=[pl.BlockSpec((TM_PROJ, D_MODEL), row),
                   pl.BlockSpec((TM_PROJ,) + ROW_TILE, lambda i: (i, 0, 0)),
                   pl.BlockSpec((SUBLANES, TM_PROJ), lambda i: (0, i)),
                   pl.BlockSpec((TM_PROJ, LANES), row),
                   pl.BlockSpec((N_EXPERTS, LANES), fixed)],
        out_shape=[jax.ShapeDtypeStruct((n, D_MODEL), F32),
                   jax.ShapeDtypeStruct((n,) + ROW_TILE, F32),
                   jax.ShapeDtypeStruct((SUBLANES, n), jnp.int32),
                   jax.ShapeDtypeStruct((n, LANES), F32),
                   jax.ShapeDtypeStruct((N_EXPERTS, LANES), jnp.int32)],
        scratch_shapes=[pltpu.VMEM((N_EXPERTS, LANES), F32)],
        compiler_params=_cparams(("arbitrary",)),
        name="outproj_router",
    )(x2, ym, yd, wo, g2, wr, br, tri)


def _row_copy(src_ref, src_row, dst_ref, dst_row, sem):
    return pltpu.make_async_copy(src_ref.at[pl.ds(src_row, 1)], dst_ref.at[pl.ds(dst_row, 1)], sem)


def _dispatch_kernel(ends_ref, pos_ref, h_ref, xs_ref, zero_ref, hbuf_ref, sem, bsem, zsem):
    step = pl.program_id(0)

    @pl.when(step == 0)
    def _():
        zero_ref[...] = jnp.zeros(zero_ref.shape, F32)

        def tile_copy(e):
            start = pl.multiple_of(ends_ref[e + 1] - TM_MOE, TM_MOE)
            return pltpu.make_async_copy(zero_ref, xs_ref.at[pl.ds(start, TM_MOE)], zsem)

        def zstart(e, carry):
            @pl.when(ends_ref[e + 1] > ends_ref[e])
            def _():
                tile_copy(e).start()
            return carry

        def zwait(e, carry):
            @pl.when(ends_ref[e + 1] > ends_ref[e])
            def _():
                tile_copy(e).wait()
            return carry

        lax.fori_loop(0, N_EXPERTS, zstart, 0)
        lax.fori_loop(0, N_EXPERTS, zwait, 0)

        def tail_copy(t):
            return pltpu.make_async_copy(zero_ref, xs_ref.at[pl.ds(pl.multiple_of(t * TM_MOE, TM_MOE), TM_MOE)], zsem)

        def tstart(t, carry):
            tail_copy(t).start()
            return carry

        def twait(t, carry):
            tail_copy(t).wait()
            return carry

        first_tail = ends_ref[N_EXPERTS] // TM_MOE
        lax.fori_loop(first_tail, xs_ref.shape[0] // TM_MOE, tstart, 0)
        lax.fori_loop(first_tail, xs_ref.shape[0] // TM_MOE, twait, 0)

    last = pl.num_programs(0) - 1
    slot = step % 3
    par = step % 2

    def block_copy(i, s):
        return pltpu.make_async_copy(h_ref.at[pl.ds(pl.multiple_of(i * TT, TT), TT)], hbuf_ref.at[s], bsem.at[s])

    @pl.when(step == 0)
    def _():
        block_copy(0, 0).start()

    @pl.when(step < last)
    def _():
        block_copy(step + 1, (step + 1) % 3).start()

    block_copy(step, slot).wait()
    src = hbuf_ref.at[slot]

    def issue(r, carry):
        _row_copy(src, r, xs_ref, pos_ref[0, r], sem.at[par]).start()
        _row_copy(src, r, xs_ref, pos_ref[0, TT + r], sem.at[par]).start()
        return carry

    def drain(s):
        def body(r, carry):
            _row_copy(src, r, xs_ref, r, sem.at[s]).wait()
            _row_copy(src, r, xs_ref, r, sem.at[s]).wait()
            return carry
        lax.fori_loop(0, TT, body, 0)

    lax.fori_loop(0, TT, issue, 0)

    @pl.when(step > 0)
    def _():
        drain(1 - par)

    @pl.when(step == last)
    def _():
        drain(par)


def _dispatch(ends, pos, h2, n_rows):
    n = h2.shape[0]
    return pl.pallas_call(
        _dispatch_kernel,
        grid_spec=pltpu.PrefetchScalarGridSpec(
            num_scalar_prefetch=1,
            grid=(n // TT,),
            in_specs=[pl.BlockSpec((None, 1, 2 * TT), lambda i, ends: (i, 0, 0), memory_space=pltpu.SMEM),
                      pl.BlockSpec(memory_space=pl.ANY)],
            out_specs=pl.BlockSpec(memory_space=pl.ANY),
            scratch_shapes=[pltpu.VMEM((TM_MOE,) + ROW_TILE, F32),
                            pltpu.VMEM((3, TT) + ROW_TILE, F32),
                            pltpu.SemaphoreType.DMA((2,)),
                            pltpu.SemaphoreType.DMA((3,)),
                            pltpu.SemaphoreType.DMA(())]),
        out_shape=jax.ShapeDtypeStruct((n_rows,) + ROW_TILE, F32),
        compiler_params=_cparams(("arbitrary",)),
        name="dispatch",
    )(ends, pos, h2)


def _moe_kernel(texp_ref, nact_ref, x_ref, wg_ref, wu_ref, wd_ref, y_ref):
    t = pl.program_id(0)

    @pl.when(t < nact_ref[0])
    def _():
        half = TM_MOE // MOE_SUBTILES
        hids = []
        for a in range(MOE_SUBTILES):
            x = _from_token_major(x_ref[a * half:(a + 1) * half]).astype(BF16)
            g = jnp.dot(x, wg_ref[...], preferred_element_type=F32)
            u = jnp.dot(x, wu_ref[...], preferred_element_type=F32)
            hids.append((g * jax.nn.sigmoid(g) * u).astype(BF16))
        for a in range(MOE_SUBTILES):
            y = jnp.dot(hids[a], wd_ref[...], preferred_element_type=F32)
            y_ref[a * half:(a + 1) * half] = _to_token_major(y)

    @pl.when(t >= nact_ref[0])
    def _():
        y_ref[...] = jnp.zeros(y_ref.shape, F32)


def _moe(tile_expert, n_active, xs, w_eg, w_eu, w_ed):
    n_tiles = xs.shape[0] // TM_MOE
    tile = lambda t, te, na: (jnp.minimum(t, na[0] - 1), 0, 0)
    out_tile = lambda t, te, na: (t, 0, 0)
    wsel = lambda t, te, na: (te[t], 0, 0)
    return pl.pallas_call(
        _moe_kernel,
        grid_spec=pltpu.PrefetchScalarGridSpec(
            num_scalar_prefetch=2,
            grid=(n_tiles,),
            in_specs=[pl.BlockSpec((TM_MOE,) + ROW_TILE, tile),
                      pl.BlockSpec((None, D_MODEL, D_FF_EXP), wsel),
                      pl.BlockSpec((None, D_MODEL, D_FF_EXP), wsel),
                      pl.BlockSpec((None, D_FF_EXP, D_MODEL), wsel)],
            out_specs=pl.BlockSpec((TM_MOE,) + ROW_TILE, out_tile)),
        out_shape=jax.ShapeDtypeStruct(xs.shape, F32),
        compiler_params=_cparams(("arbitrary",)),
        name="moe",
    )(tile_expert, n_active, xs, w_eg, w_eu, w_ed)


def _combine_kernel(pos_ref, posn_ref, x1_ref, gates_ref, gf_ref, ys_ref, o_ref, buf_ref, sem):
    step = pl.program_id(0)
    slot = step % 2

    def gather(p_ref, s):
        def body(r, carry):
            _row_copy(ys_ref, p_ref[0, r], buf_ref.at[s, 0], r, sem.at[s]).start()
            _row_copy(ys_ref, p_ref[0, TT + r], buf_ref.at[s, 1], r, sem.at[s]).start()
            return carry
        lax.fori_loop(0, TT, body, 0)

    @pl.when(step == 0)
    def _():
        gather(pos_ref, 0)

    @pl.when(step < pl.num_programs(0) - 1)
    def _():
        gather(posn_ref, 1 - slot)

    def drain(r, carry):
        _row_copy(ys_ref, pos_ref[0, r], buf_ref.at[slot, 0], r, sem.at[slot]).wait()
        _row_copy(ys_ref, pos_ref[0, TT + r], buf_ref.at[slot, 1], r, sem.at[slot]).wait()
        return carry

    lax.fori_loop(0, TT, drain, 0)
    gates = gates_ref[...]
    y = (x1_ref[...] + gates[:, 0:1] * _from_token_major(buf_ref[slot, 0])
         + gates[:, 1:2] * _from_token_major(buf_ref[slot, 1]))
    ms = jnp.mean(y * y, axis=-1, keepdims=True)
    o_ref[...] = y * lax.rsqrt(ms + EPS) * gf_ref[...]


def _combine(pos, x1, gates, gf, ys):
    n = x1.shape[0]
    nblk = n // TT
    return pl.pallas_call(
        _combine_kernel,
        grid=(nblk,),
        in_specs=[pl.BlockSpec((None, 1, 2 * TT), lambda i: (i, 0, 0), memory_space=pltpu.SMEM),
                  pl.BlockSpec((None, 1, 2 * TT), lambda i: (jnp.minimum(i + 1, nblk - 1), 0, 0),
                               memory_space=pltpu.SMEM),
                  pl.BlockSpec((TT, D_MODEL), lambda i: (i, 0)),
                  pl.BlockSpec((TT, LANES), lambda i: (i, 0)),
                  pl.BlockSpec((1, D_MODEL), lambda i: (0, 0)),
                  pl.BlockSpec(memory_space=pl.ANY)],
        out_specs=pl.BlockSpec((TT, D_MODEL), lambda i: (i, 0)),
        out_shape=jax.ShapeDtypeStruct((n, D_MODEL), F32),
        scratch_shapes=[pltpu.VMEM((2, 2, TT) + ROW_TILE, F32),
                        pltpu.SemaphoreType.DMA((2,))],
        compiler_params=_cparams(("arbitrary",)),
        name="combine",
    )(pos, pos, x1, gates, gf, ys)


def kernel(x, w_in, conv_w, conv_b, w_mq, w_mk, w_mgate, b_mgate, m_norm_g, m_skip, lambda_qk, da_norm_g,
           rel_bias, w_out, norm1_g, norm2_g, w_rg, b_rg, w_re, b_re, w_eg, w_eu, w_ed, normf_g):
    bsz, seq, _ = x.shape
    n = bsz * seq
    assert seq % TQ == 0 and seq % CHUNK == 0 and n % TM_PROJ == 0 and n % TT == 0
    l = 0
    x2 = x.reshape(n, D_MODEL)

    w_in_b = w_in[l].astype(BF16)
    wqv_t = jnp.stack([w_in_b[:, 3 * 512:4 * 512].T, w_in_b[:, 5 * 512:6 * 512].T])
    c, vm, z, qt, kd, vt = _inproj(x2, norm1_g[l][None, :], w_in_b, wqv_t)

    wg = jnp.zeros((3 * D_MLSTM, LANES), F32).at[:, :2 * ML_HEADS].set(w_mgate[l]).astype(BF16)
    bg = jnp.zeros((1, LANES), F32).at[0, :2 * ML_HEADS].set(b_mgate[l])
    y_m = _mlstm(c, vm, z, conv_w[l], conv_b[l][None, :], w_mq[l].astype(BF16), w_mk[l].astype(BF16),
                 wg, bg, m_norm_g[l][None, :], m_skip[l][None, :], bsz, seq)

    y_d = _attn(lambda_qk[l].astype(F32), qt, kd, vt, _rel_bias_tiles(rel_bias),
                da_norm_g[l][:, None], bsz, seq)

    wr = jnp.zeros((D_MODEL, LANES), F32)
    wr = wr.at[:, :N_GROUPS].set(w_rg[l])
    wr = wr.at[:, SUBLANES:SUBLANES + N_EXPERTS].set(w_re[l].reshape(D_MODEL, N_EXPERTS))
    wr_hi = wr.astype(BF16)
    wr_lo = (wr - wr_hi.astype(F32)).astype(BF16)
    wr3 = jnp.concatenate([wr_hi, wr_hi, wr_lo], axis=0)
    br = jnp.zeros((1, LANES), F32)
    br = br.at[0, :N_GROUPS].set(b_rg[l])
    br = br.at[0, SUBLANES:SUBLANES + N_EXPERTS].set(b_re[l].reshape(N_EXPERTS))
    x1, h2, meta, gates, counts = _outproj(x2, y_m, y_d, w_out[l].astype(BF16), norm2_g[l][None, :], wr3, br)

    cnt = counts[:, 0]
    tiles_e = (cnt + TM_MOE - 1) // TM_MOE
    ends_t = jnp.cumsum(tiles_e)
    ends = jnp.concatenate([jnp.zeros((1,), jnp.int32), ends_t * TM_MOE]).astype(jnp.int32)
    offs = ends[:-1]
    n_tiles = (2 * n) // TM_MOE + N_EXPERTS
    n_active = ends_t[-1:].astype(jnp.int32)
    tile_ids = jnp.minimum(jnp.arange(n_tiles, dtype=jnp.int32), n_active[0] - 1)
    tile_expert = jnp.sum(tile_ids[:, None] >= ends_t[None, :], axis=1).astype(jnp.int32)
    eids = jnp.arange(N_EXPERTS, dtype=jnp.int32)[:, None]
    pos1 = jnp.sum(jnp.where(meta[0][None, :] == eids, offs[:, None], 0), axis=0) + meta[2]
    pos2 = jnp.sum(jnp.where(meta[1][None, :] == eids, offs[:, None], 0), axis=0) + meta[3]
    pos = jnp.concatenate([pos1.reshape(n // TT, 1, TT), pos2.reshape(n // TT, 1, TT)], axis=2)

    xs = _dispatch(ends, pos, h2, n_tiles * TM_MOE)
    ys = _moe(tile_expert, n_active, xs, w_eg[l].astype(BF16), w_eu[l].astype(BF16), w_ed[l].astype(BF16))
    out = _combine(pos, x1, gates, normf_g[None, :], ys)
    return out.reshape(bsz, seq, D_MODEL)
```

```python
import functools
import math

import jax
import jax.numpy as jnp
from jax import lax
from jax.experimental import pallas as pl
from jax.experimental.pallas import tpu as pltpu

F32 = jnp.float32
BF16 = jnp.bfloat16

D_MODEL = 1024
D_MLSTM = 512
D_DIFF = 512
ML_HEADS = 4
ML_HD = 128
CONV_W = 4
CHUNK = 128
DA_HEADS = 4
DA_VD = 128
DA_QD = 64
REL_BUCKETS = 32
REL_MAX_DIST = 128
N_GROUPS = 4
EXP_PER_GROUP = 8
N_EXPERTS = 32
D_FF_EXP = 512
EPS = 1e-6
SUBLN_EPS = 1e-5
LAM_INIT = 0.8 - 0.6 * math.exp(-0.3 * 0)
NEG = -1e30
LOG2E = math.log2(math.e)

LANES = 128
SUBLANES = 8
MXU_COLS = 256

TM_PROJ = 512
TQ = 512
ACC_ROWS = DA_VD + 16
MLSTM_BATCH = 4
TM_MOE = 512
MOE_SUBTILES = 2
TT = 512
VMEM_LIMIT = 48 * 1024 * 1024


ROW_TILE = (D_MODEL // LANES, LANES)


def _cparams(sem):
    return pltpu.CompilerParams(dimension_semantics=sem, vmem_limit_bytes=VMEM_LIMIT)


def _to_token_major(x):
    return pltpu.einshape("a(bc)->abc", x, b=ROW_TILE[0])


def _from_token_major(x):
    return pltpu.einshape("abc->a(bc)", x)


def _inproj_kernel(tiles_per_seq, x_ref, g_ref, w_ref, wqv_ref, cw_ref, cb_ref,
                   c_ref, vm_ref, z_ref, q_ref, k_ref, v_ref, ext_ref):
    x = x_ref[...]
    ms = jnp.mean(x * x, axis=-1, keepdims=True)
    h = (x * lax.rsqrt(ms + EPS) * g_ref[...]).astype(BF16)
    z_ref[...] = jnp.dot(h, w_ref[:, 2 * 512:3 * 512], preferred_element_type=F32)
    k_ref[...] = jnp.dot(h, w_ref[:, 4 * 512:5 * 512], preferred_element_type=F32).astype(k_ref.dtype)

    @pl.when(pl.program_id(0) % tiles_per_seq == 0)
    def _():
        ext_ref[0:SUBLANES, :] = jnp.zeros((SUBLANES, D_MLSTM), F32)

    tm = x.shape[0]
    ext_ref[SUBLANES:SUBLANES + tm, :] = jnp.dot(h, w_ref[:, 0:512], preferred_element_type=F32)
    cw = cw_ref[...]
    conv = cb_ref[...] + cw[CONV_W - 1:CONV_W, :] * ext_ref[SUBLANES:SUBLANES + tm, :]
    for k in range(1, CONV_W):
        conv = conv + cw[CONV_W - 1 - k:CONV_W - k, :] * ext_ref[SUBLANES - k:SUBLANES - k + tm, :]
    ext_ref[0:SUBLANES, :] = ext_ref[tm:tm + SUBLANES, :]
    c_ref[...] = conv * jax.nn.sigmoid(conv)
    nt = (((1,), (1,)), ((), ()))
    qt = lax.dot_general(wqv_ref[0], h, nt, preferred_element_type=F32)
    q_ref[...] = (qt * (DA_QD ** -0.5 * LOG2E)).astype(q_ref.dtype)
    v_ref[...] = lax.dot_general(wqv_ref[1], h, nt, preferred_element_type=F32).astype(v_ref.dtype)
    vm_ref[...] = lax.dot_general(wqv_ref[2], h, nt, preferred_element_type=F32).astype(vm_ref.dtype)


def _inproj(x2, g1, w_in, wqv_t, conv_w, conv_b, seq):
    n = x2.shape[0]
    assert seq % TM_PROJ == 0
    row = lambda i: (i, 0)
    fixed = lambda i: (0, 0)
    sds = lambda dt: jax.ShapeDtypeStruct((n, 512), dt)
    sds_t = jax.ShapeDtypeStruct((512, n), BF16)
    blk = pl.BlockSpec((TM_PROJ, 512), row)
    blk_t = pl.BlockSpec((512, TM_PROJ), lambda i: (0, i))
    return pl.pallas_call(
        functools.partial(_inproj_kernel, seq // TM_PROJ),
        grid=(n // TM_PROJ,),
        in_specs=[pl.BlockSpec((TM_PROJ, D_MODEL), row),
                  pl.BlockSpec((1, D_MODEL), fixed),
                  pl.BlockSpec((D_MODEL, 6 * 512), fixed),
                  pl.BlockSpec((3, 512, D_MODEL), lambda i: (0, 0, 0)),
                  pl.BlockSpec((CONV_W, D_MLSTM), fixed),
                  pl.BlockSpec((1, D_MLSTM), fixed)],
        out_specs=[blk, blk_t, blk, blk_t, blk, blk_t],
        out_shape=[sds(F32), sds_t, sds(F32), sds_t, sds(BF16), sds_t],
        scratch_shapes=[pltpu.VMEM((SUBLANES + TM_PROJ, D_MLSTM), F32)],
        compiler_params=_cparams(("arbitrary",)),
        name="inproj",
    )(x2, g1, w_in, wqv_t, conv_w, conv_b)


def _mlstm_kernel(*refs):
    c_ref = refs[0]
    vt_refs = refs[1:1 + MLSTM_BATCH]
    (z_ref, wq_ref, wk_ref, wg_ref, bg_ref,
     ng_ref, sk_ref, tril_ref, y_ref, st_ref, m_ref) = refs[1 + MLSTM_BATCH:]
    j = pl.program_id(1)

    @pl.when(j == 0)
    def _():
        st_ref[...] = jnp.zeros(st_ref.shape, F32)
        m_ref[...] = jnp.zeros(m_ref.shape, F32)

    seqs = range(MLSTM_BATCH)
    proj = [_mlstm_gates(c_ref.at[bb], vt_refs[bb], wq_ref, wk_ref, wg_ref, bg_ref, tril_ref) for bb in seqs]
    for h in range(ML_HEADS):
        for bb in seqs:
            _mlstm_head(h, proj[bb], z_ref.at[bb], ng_ref, sk_ref, y_ref.at[bb], st_ref.at[bb], m_ref.at[bb])


def _mlstm_gates(c_ref, vm_ref, wq_ref, wk_ref, wg_ref, bg_ref, tril_ref):
    L = CHUNK
    c_act = c_ref[...]

    nt = (((1,), (1,)), ((), ()))
    vt = vm_ref[...]
    qts, ks = [], []
    gates_t = jnp.dot(wg_ref[:, 2 * D_MLSTM:3 * D_MLSTM], vt, preferred_element_type=F32) + bg_ref[...]
    for h in range(ML_HEADS):
        hs = slice(h * ML_HD, (h + 1) * ML_HD)
        ch = c_act[:, hs].astype(BF16)
        qt = lax.dot_general(wq_ref[h], ch, nt, preferred_element_type=F32)
        kh = jnp.dot(ch, wk_ref[h], preferred_element_type=F32)
        qts.append(qt.astype(BF16))
        ks.append(kh)
        gates_t = gates_t + jnp.dot(wg_ref[:, hs], qts[h], preferred_element_type=F32)
        gates_t = gates_t + lax.dot_general(wg_ref[:, D_MLSTM + h * ML_HD:D_MLSTM + (h + 1) * ML_HD],
                                            kh.astype(BF16), nt, preferred_element_type=F32)

    logf_t = jnp.minimum(gates_t, 0.0) - jnp.log(1.0 + jnp.exp(-jnp.abs(gates_t)))
    bcum_t = jnp.dot(logf_t, tril_ref[...], preferred_element_type=F32, precision=lax.Precision.HIGHEST)
    a8 = gates_t[0:SUBLANES] - bcum_t[SUBLANES:2 * SUBLANES]
    a_cols = jnp.concatenate([a8, jnp.zeros((LANES - SUBLANES, L), F32)], axis=0).T
    return dict(c_act=c_act, vt=vt, qts=qts, ks=ks, gates_t=gates_t, bcum_t=bcum_t, a_cols=a_cols)


def _mlstm_head(h, p, z_ref, ng_ref, sk_ref, y_ref, st_ref, m_ref):
    L = CHUNK
    c_act, vt, gates_t, bcum_t, a_cols = p["c_act"], p["vt"], p["gates_t"], p["bcum_t"], p["a_cols"]
    s_idx = lax.broadcasted_iota(jnp.int32, (L, L), 0)
    t_idx = lax.broadcasted_iota(jnp.int32, (L, L), 1)
    causal = s_idx <= t_idx
    ones_row = jnp.where(lax.broadcasted_iota(jnp.int32, (ML_HD, L), 0) == 0, 1.0, 0.0).astype(BF16)
    sl = slice(h * ML_HD, (h + 1) * ML_HD)
    qt = p["qts"][h]
    k = (p["ks"][h] * (ML_HD ** -0.5)).astype(BF16)
    vaug_t = jnp.concatenate([vt[sl, :], ones_row], axis=0)
    i_row = gates_t[h:h + 1, :]
    b_row = bcum_t[SUBLANES + h:SUBLANES + h + 1, :]
    a_row = i_row - b_row
    m_prev = m_ref[h][0:1, 0:1]
    state_t = st_ref[h]

    e = jnp.where(causal, a_cols[:, h:h + 1], NEG)
    cm = jnp.max(e, axis=0, keepdims=True)
    mm = jnp.maximum(m_prev, cm)
    dexp = jnp.exp(e - mm)
    s_t = jnp.dot(k, qt, preferred_element_type=F32)
    w_t = (dexp * s_t).astype(BF16)
    sp = jnp.exp(m_prev - mm)
    r_inter = jnp.dot(state_t.astype(BF16), qt, preferred_element_type=F32)
    r_intra = jnp.dot(vaug_t, w_t, preferred_element_type=F32)
    tot = sp * r_inter + r_intra
    num = tot[:ML_HD]
    den = tot[ML_HD:ML_HD + 1]
    h_t = num / jnp.maximum(jnp.abs(den), jnp.exp(-(b_row + mm)))

    b_end = b_row[:, L - 1:L]
    mm_end = mm[:, L - 1:L]
    wk = jnp.exp(a_row - mm_end)
    decay = jnp.exp(m_prev - mm_end)
    vw = (vaug_t.astype(F32) * wk).astype(BF16)
    upd = jnp.dot(vw, k, preferred_element_type=F32)
    st_ref[h] = decay * state_t + upd
    m_ref[h] = jnp.broadcast_to(b_end + mm_end, (SUBLANES, LANES))

    mu = jnp.mean(h_t, axis=0, keepdims=True)
    xc = h_t - mu
    var = jnp.mean(xc * xc, axis=0, keepdims=True)
    hn = (xc * lax.rsqrt(var + EPS)).T * ng_ref[:, sl]
    hn = hn + sk_ref[:, sl] * c_act[:, sl]
    o = jax.nn.sigmoid(z_ref[:, sl])
    y_ref[:, sl] = (o * hn).astype(y_ref.dtype)


def _mlstm(c, vmt, z, wqt, wk, wgt, bgt, ng, sk, bsz, seq):
    nc = seq // CHUNK
    f2 = lambda b, j: (0, 0)
    f3 = lambda b, j: (0, 0, 0)
    triu = jnp.triu(jnp.ones((CHUNK, CHUNK), F32))
    blk = pl.BlockSpec((MLSTM_BATCH, CHUNK, D_MLSTM), lambda b, j: (b, j, 0))
    vt_specs = [pl.BlockSpec((D_MLSTM, CHUNK), functools.partial(lambda b, j, bb: (0, (b * MLSTM_BATCH + bb) * nc + j),
                                                                  bb=bb))
                for bb in range(MLSTM_BATCH)]
    c, z = (a.reshape(bsz, seq, D_MLSTM) for a in (c, z))
    y = pl.pallas_call(
        _mlstm_kernel,
        grid=(bsz // MLSTM_BATCH, nc),
        in_specs=[blk] + vt_specs + [
                  blk,
                  pl.BlockSpec((ML_HEADS, ML_HD, ML_HD), f3),
                  pl.BlockSpec((ML_HEADS, ML_HD, ML_HD), f3),
                  pl.BlockSpec((2 * SUBLANES, 3 * D_MLSTM), f2),
                  pl.BlockSpec((2 * SUBLANES, 1), f2),
                  pl.BlockSpec((1, D_MLSTM), f2),
                  pl.BlockSpec((1, D_MLSTM), f2),
                  pl.BlockSpec((CHUNK, CHUNK), f2)],
        out_specs=blk,
        out_shape=jax.ShapeDtypeStruct((bsz, seq, D_MLSTM), BF16),
        scratch_shapes=[pltpu.VMEM((MLSTM_BATCH, ML_HEADS, 2 * ML_HD, ML_HD), F32),
                        pltpu.VMEM((MLSTM_BATCH, ML_HEADS, SUBLANES, LANES), F32)],
        compiler_params=_cparams(("parallel", "arbitrary")),
        name="mlstm",
    )(c, *([vmt] * MLSTM_BATCH), z, wqt, wk, wgt, bgt, ng, sk, triu)
    return y.reshape(bsz * seq, D_MLSTM)


def _attn_kernel(lq_ref, qt_ref, k_ref, vt_ref, bias_ref, g_ref, o_ref,
                 acc_ref, m_ref, q2_ref, s_ref, p_ref, a_ref):
    tq = TQ
    i = pl.program_id(2)
    qt = qt_ref[...]
    row = lax.broadcasted_iota(jnp.int32, (LANES, tq), 0)
    zero = jnp.zeros_like(qt)
    q2_ref[:, 0:tq] = jnp.where(row < DA_QD, qt, zero)
    q2_ref[:, tq:2 * tq] = jnp.where(row >= DA_QD, qt, zero)
    acc_ref[...] = jnp.zeros(acc_ref.shape, F32)
    m_ref[...] = jnp.full(m_ref.shape, NEG, F32)

    p_ref[1] = jnp.zeros(p_ref.shape[1:], BF16)
    a_ref[1] = jnp.ones(a_ref.shape[1:], F32)

    def logits(kb, buf):
        start = pl.multiple_of(kb * tq, tq)
        s_ref[buf] = jnp.dot(k_ref[pl.ds(start, tq), :], q2_ref[...], preferred_element_type=F32)

    ones_rows = jnp.ones((ACC_ROWS - DA_VD, tq), BF16)

    def values(kb, buf):
        start = pl.multiple_of(jnp.maximum(kb, 0) * tq, tq)
        lhs = jnp.concatenate([vt_ref[:, pl.ds(start, tq)], ones_rows], axis=0)
        pv = jnp.dot(lhs, p_ref[buf], preferred_element_type=F32)
        acc_ref[...] = a_ref[buf] * acc_ref[...] + pv

    def softmax(buf, which_bias):
        s = s_ref[buf]
        if which_bias is not None:
            bias = bias_ref[which_bias]
            s = s + jnp.concatenate([bias, bias], axis=1)
        m_old = m_ref[...]
        m_new = jnp.maximum(m_old, jnp.max(s, axis=0, keepdims=True))
        m_ref[...] = m_new
        a_ref[buf] = jnp.exp2(m_old - m_new)
        p_ref[buf] = jnp.exp2(s - m_new).astype(BF16)

    def stage(kb, buf, has_next, which_bias):
        softmax(buf, which_bias)
        if has_next:
            logits(kb + 1, 1 - buf)
        values(kb - 1, 1 - buf)

    logits(0, 0)
    n_far = jnp.maximum(i - 1, 0)

    def far_pair(t, carry):
        stage(2 * t, 0, True, None)
        stage(2 * t + 1, 1, True, None)
        return carry

    lax.fori_loop(0, n_far // 2, far_pair, 0)
    kb0 = 2 * (n_far // 2)

    @pl.when(i == 0)
    def _():
        stage(0, 0, False, 0)
        values(0, 0)

    @pl.when(jnp.logical_and(i >= 1, n_far % 2 == 0))
    def _():
        stage(kb0, 0, True, 1)
        stage(kb0 + 1, 1, False, 0)
        values(kb0 + 1, 1)

    @pl.when(n_far % 2 == 1)
    def _():
        stage(kb0, 0, True, None)
        stage(kb0 + 1, 1, True, 1)
        stage(kb0 + 2, 0, False, 0)
        values(kb0 + 2, 0)

    o = acc_ref[0:DA_VD, :] * (1.0 / acc_ref[DA_VD:DA_VD + 1, :])
    lq = lq_ref[...]
    d01 = jnp.sum(lq[0:1, :] * lq[1:2, :], axis=-1, keepdims=True)
    d23 = jnp.sum(lq[2:3, :] * lq[3:4, :], axis=-1, keepdims=True)
    lam = jnp.exp(d01) - jnp.exp(d23) + LAM_INIT
    y = o[:, :tq] - lam * o[:, tq:]
    y = y * lax.rsqrt(jnp.mean(y * y, axis=0, keepdims=True) + SUBLN_EPS) * g_ref[...]
    o_ref[...] = (y * (1.0 - LAM_INIT)).T.astype(o_ref.dtype)


def _rel_bias_tiles(rel_bias):
    L = TQ
    max_exact = REL_BUCKETS // 2
    n = jnp.arange(-(L - 1), 2 * L, dtype=jnp.int32)
    nn = jnp.maximum(n, 0)
    large = max_exact + (jnp.log(jnp.maximum(nn, 1).astype(F32) / max_exact)
                         / math.log(REL_MAX_DIST / max_exact) * (REL_BUCKETS - max_exact)).astype(jnp.int32)
    large = jnp.minimum(large, REL_BUCKETS - 1)
    bucket = jnp.where(nn < max_exact, nn, large)
    rb = (rel_bias.astype(F32) - rel_bias[REL_BUCKETS - 1].astype(F32)[None, :]).T
    t = jnp.where(n[None, :] >= 0, rb[:, bucket] * LOG2E, NEG)

    def toeplitz(v):
        vp = jnp.pad(v, ((0, 0), (0, 1)))
        flat = jnp.tile(vp, (1, L))[:, :L * (2 * L - 1)]
        return flat.reshape(v.shape[0], L, 2 * L - 1)[:, :, L - 1:]

    return jnp.stack([toeplitz(t[:, :2 * L - 1]), toeplitz(t[:, L:])], axis=1)


def _attn(lq, qt, kd, vt, bias, g, bsz, seq):
    nq = seq // TQ
    return pl.pallas_call(
        _attn_kernel,
        grid=(bsz, DA_HEADS, nq),
        in_specs=[pl.BlockSpec((4, DA_QD), lambda b, h, i: (0, 0)),
                  pl.BlockSpec((LANES, TQ), lambda b, h, i: (h, b * nq + i)),
                  pl.BlockSpec((seq, LANES), lambda b, h, i: (b, h)),
                  pl.BlockSpec((LANES, seq), lambda b, h, i: (h, b)),
                  pl.BlockSpec((None, 2, TQ, TQ), lambda b, h, i: (h, 0, 0, 0)),
                  pl.BlockSpec((DA_VD, 1), lambda b, h, i: (0, 0))],
        out_specs=pl.BlockSpec((TQ, LANES), lambda b, h, i: (b * nq + i, h)),
        out_shape=jax.ShapeDtypeStruct((bsz * seq, D_DIFF), BF16),
        scratch_shapes=[pltpu.VMEM((ACC_ROWS, 2 * TQ), F32),
                        pltpu.VMEM((1, 2 * TQ), F32),
                        pltpu.VMEM((LANES, 2 * TQ), BF16),
                        pltpu.VMEM((2, TQ, 2 * TQ), F32),
                        pltpu.VMEM((2, TQ, 2 * TQ), BF16),
                        pltpu.VMEM((2, 1, 2 * TQ), F32)],
        compiler_params=_cparams(("parallel", "parallel", "arbitrary")),
        name="attn",
    )(lq, qt, kd, vt, bias, g)


def _outproj_kernel(x_ref, ym_ref, yd_ref, wo_ref, g2_ref, wr_ref, br_ref, tri_ref,
                    x1_ref, h2_ref, meta_ref, gates_ref, cnt_ref, run_ref):
    tm = TM_PROJ
    step = pl.program_id(0)

    @pl.when(step == 0)
    def _():
        run_ref[...] = jnp.zeros(run_ref.shape, F32)

    x1 = (x_ref[...]
          + jnp.dot(ym_ref[...], wo_ref[0:D_MLSTM, :], preferred_element_type=F32)
          + jnp.dot(yd_ref[...], wo_ref[D_MLSTM:, :], preferred_element_type=F32))
    x1_ref[...] = x1
    ms = jnp.mean(x1 * x1, axis=-1, keepdims=True)
    h2 = x1 * lax.rsqrt(ms + EPS) * g2_ref[...]
    h2_ref[...] = _to_token_major(h2)

    h_hi = h2.astype(BF16)
    h_lo = (h2 - h_hi.astype(F32)).astype(BF16)
    lhs = jnp.concatenate([h_hi, h_lo, h_hi], axis=1)
    logits = jnp.dot(lhs, wr_ref[...], preferred_element_type=F32) + br_ref[...]
    lt = logits.T

    row8 = lax.broadcasted_iota(jnp.int32, (SUBLANES, tm), 0)
    lg = jnp.where(row8 < N_GROUPS, lt[0:SUBLANES], NEG)
    gmax = jnp.max(lg, axis=0, keepdims=True)
    gidx = jnp.min(jnp.where(lg == gmax, row8, SUBLANES), axis=0, keepdims=True)
    pg_sel = 1.0 / jnp.sum(jnp.exp(lg - gmax), axis=0, keepdims=True)
    le = jnp.zeros((EXP_PER_GROUP, tm), F32)
    for g in range(N_GROUPS):
        le = jnp.where(gidx == g, lt[SUBLANES * (g + 1):SUBLANES * (g + 2)], le)
    v1 = jnp.max(le, axis=0, keepdims=True)
    i1 = jnp.min(jnp.where(le == v1, row8, SUBLANES), axis=0, keepdims=True)
    le2 = jnp.where(row8 == i1, NEG, le)
    v2 = jnp.max(le2, axis=0, keepdims=True)
    i2 = jnp.min(jnp.where(le2 == v2, row8, SUBLANES), axis=0, keepdims=True)
    e21 = jnp.exp(v2 - v1)
    pw1 = 1.0 / (1.0 + e21)
    gate1 = pg_sel * pw1
    gate2 = pg_sel * (e21 * pw1)
    eid1 = gidx * EXP_PER_GROUP + i1
    eid2 = gidx * EXP_PER_GROUP + i2

    row32 = lax.broadcasted_iota(jnp.int32, (N_EXPERTS, tm), 0)
    oh1 = row32 == eid1
    oh2 = row32 == eid2
    oh = jnp.where(oh1, 1.0, 0.0) + jnp.where(oh2, 1.0, 0.0)
    before = jnp.dot(oh.astype(BF16), tri_ref[...], preferred_element_type=F32) + run_ref[:, 0:1]
    rank1 = jnp.sum(jnp.where(oh1, before, 0.0), axis=0, keepdims=True).astype(jnp.int32)
    rank2 = jnp.sum(jnp.where(oh2, before, 0.0), axis=0, keepdims=True).astype(jnp.int32)
    run = run_ref[...] + jnp.sum(oh, axis=1, keepdims=True)
    run_ref[...] = run
    cnt_ref[...] = run.astype(jnp.int32)

    meta_ref[...] = jnp.where(row8 == 0, eid1, jnp.where(row8 == 1, eid2,
                              jnp.where(row8 == 2, rank1, jnp.where(row8 == 3, rank2, 0))))
    row128 = lax.broadcasted_iota(jnp.int32, (LANES, tm), 0)
    gt = jnp.where(row128 == 0, gate1, jnp.where(row128 == 1, gate2, 0.0))
    gates_ref[...] = gt.T


def _outproj(x2, ym, yd, wo, g2, wr, br):
    n = x2.shape[0]
    row = lambda i: (i, 0)
    fixed = lambda i: (0, 0)
    tri = jnp.triu(jnp.ones((TM_PROJ, TM_PROJ), F32), k=1).astype(BF16)
    return pl.pallas_call(
        _outproj_kernel,
        grid=(n // TM_PROJ,),
        in_specs=[pl.BlockSpec((TM_PROJ, D_MODEL), row),
                  pl.BlockSpec((TM_PROJ, D_MLSTM), row),
                  pl.BlockSpec((TM_PROJ, D_DIFF), row),
                  pl.BlockSpec((D_MODEL, D_MODEL), fixed),
                  pl.BlockSpec((1, D_MODEL), fixed),
                  pl.BlockSpec((3 * D_MODEL, LANES), fixed),
                  pl.BlockSpec((1, LANES), fixed),
                  pl.BlockSpec((TM_PROJ, TM_PROJ), fixed)],
        out_specs=[pl.BlockSpec((TM_PROJ, D_MODEL), row),
                   pl.BlockSpec((TM_PROJ,) + ROW_TILE, lambda i: (i, 0, 0)),
                   pl.BlockSpec((SUBLANES, TM_PROJ), lambda i: (0, i)),
                   pl.BlockSpec((TM_PROJ, LANES), row),
                   pl.BlockSpec((N_EXPERTS, LANES), fixed)],
        out_shape=[jax.ShapeDtypeStruct((n, D_MODEL), F32),
                   jax.ShapeDtypeStruct((n,) + ROW_TILE, F32),
                   jax.ShapeDtypeStruct((SUBLANES, n), jnp.int32),
                   jax.ShapeDtypeStruct((n, LANES), F32),
                   jax.ShapeDtypeStruct((N_EXPERTS, LANES), jnp.int32)],
        scratch_shapes=[pltpu.VMEM((N_EXPERTS, LANES), F32)],
        compiler_params=_cparams(("arbitrary",)),
        name="outproj_router",
    )(x2, ym, yd, wo, g2, wr, br, tri)


def _row_copy(src_ref, src_row, dst_ref, dst_row, sem):
    return pltpu.make_async_copy(src_ref.at[pl.ds(src_row, 1)], dst_ref.at[pl.ds(dst_row, 1)], sem)


def _dispatch_kernel(ends_ref, pos_ref, h_ref, xs_ref, zero_ref, hbuf_ref, sem, bsem, zsem):
    step = pl.program_id(0)

    @pl.when(step == 0)
    def _():
        zero_ref[...] = jnp.zeros(zero_ref.shape, F32)

        def tile_copy(e):
            start = pl.multiple_of(ends_ref[e + 1] - TM_MOE, TM_MOE)
            return pltpu.make_async_copy(zero_ref, xs_ref.at[pl.ds(start, TM_MOE)], zsem)

        def zstart(e, carry):
            @pl.when(ends_ref[e + 1] > ends_ref[e])
            def _():
                tile_copy(e).start()
            return carry

        def zwait(e, carry):
            @pl.when(ends_ref[e + 1] > ends_ref[e])
            def _():
                tile_copy(e).wait()
            return carry

        lax.fori_loop(0, N_EXPERTS, zstart, 0)
        lax.fori_loop(0, N_EXPERTS, zwait, 0)

        def tail_copy(t):
            return pltpu.make_async_copy(zero_ref, xs_ref.at[pl.ds(pl.multiple_of(t * TM_MOE, TM_MOE), TM_MOE)], zsem)

        def tstart(t, carry):
            tail_copy(t).start()
            return carry

        def twait(t, carry):
            tail_copy(t).wait()
            return carry

        first_tail = ends_ref[N_EXPERTS] // TM_MOE
        lax.fori_loop(first_tail, xs_ref.shape[0] // TM_MOE, tstart, 0)
        lax.fori_loop(first_tail, xs_ref.shape[0] // TM_MOE, twait, 0)

    last = pl.num_programs(0) - 1
    slot = step % 3
    par = step % 2

    def block_copy(i, s):
        return pltpu.make_async_copy(h_ref.at[pl.ds(pl.multiple_of(i * TT, TT), TT)], hbuf_ref.at[s], bsem.at[s])

    @pl.when(step == 0)
    def _():
        block_copy(0, 0).start()

    @pl.when(step < last)
    def _():
        block_copy(step + 1, (step + 1) % 3).start()

    block_copy(step, slot).wait()
    src = hbuf_ref.at[slot]

    def issue(r, carry):
        _row_copy(src, r, xs_ref, pos_ref[0, r], sem.at[par]).start(priority=0)
        _row_copy(src, r, xs_ref, pos_ref[0, TT + r], sem.at[par]).start(priority=1)
        return carry

    def drain(s):
        def body(r, carry):
            _row_copy(src, r, xs_ref, r, sem.at[s]).wait()
            _row_copy(src, r, xs_ref, r, sem.at[s]).wait()
            return carry
        lax.fori_loop(0, TT, body, 0)

    lax.fori_loop(0, TT, issue, 0)

    @pl.when(step > 0)
    def _():
        drain(1 - par)

    @pl.when(step == last)
    def _():
        drain(par)


def _dispatch(ends, pos, h2, n_rows):
    n = h2.shape[0]
    return pl.pallas_call(
        _dispatch_kernel,
        grid_spec=pltpu.PrefetchScalarGridSpec(
            num_scalar_prefetch=1,
            grid=(n // TT,),
            in_specs=[pl.BlockSpec((None, 1, 2 * TT), lambda i, ends: (i, 0, 0), memory_space=pltpu.SMEM),
                      pl.BlockSpec(memory_space=pl.ANY)],
            out_specs=pl.BlockSpec(memory_space=pl.ANY),
            scratch_shapes=[pltpu.VMEM((TM_MOE,) + ROW_TILE, F32),
                            pltpu.VMEM((3, TT) + ROW_TILE, F32),
                            pltpu.SemaphoreType.DMA((2,)),
                            pltpu.SemaphoreType.DMA((3,)),
                            pltpu.SemaphoreType.DMA(())]),
        out_shape=jax.ShapeDtypeStruct((n_rows,) + ROW_TILE, F32),
        compiler_params=_cparams(("arbitrary",)),
        name="dispatch",
    )(ends, pos, h2)


def _moe_kernel(texp_ref, nact_ref, x_ref, wg_ref, wu_ref, wd_ref, y_ref):
    t = pl.program_id(0)

    @pl.when(t < nact_ref[0])
    def _():
        half = TM_MOE // MOE_SUBTILES
        hids = []
        for a in range(MOE_SUBTILES):
            x = _from_token_major(x_ref[a * half:(a + 1) * half]).astype(BF16)
            g = jnp.dot(x, wg_ref[...], preferred_element_type=F32)
            u = jnp.dot(x, wu_ref[...], preferred_element_type=F32)
            hids.append((g * jax.nn.sigmoid(g) * u).astype(BF16))
        for a in range(MOE_SUBTILES):
            y = jnp.dot(hids[a], wd_ref[...], preferred_element_type=F32)
            y_ref[a * half:(a + 1) * half] = _to_token_major(y)

    @pl.when(t >= nact_ref[0])
    def _():
        y_ref[...] = jnp.zeros(y_ref.shape, F32)


def _moe(tile_expert, n_active, xs, w_eg, w_eu, w_ed):
    n_tiles = xs.shape[0] // TM_MOE
    tile = lambda t, te, na: (jnp.minimum(t, na[0] - 1), 0, 0)
    out_tile = lambda t, te, na: (t, 0, 0)
    wsel = lambda t, te, na: (te[t], 0, 0)
    return pl.pallas_call(
        _moe_kernel,
        grid_spec=pltpu.PrefetchScalarGridSpec(
            num_scalar_prefetch=2,
            grid=(n_tiles,),
            in_specs=[pl.BlockSpec((TM_MOE,) + ROW_TILE, tile),
                      pl.BlockSpec((None, D_MODEL, D_FF_EXP), wsel),
                      pl.BlockSpec((None, D_MODEL, D_FF_EXP), wsel),
                      pl.BlockSpec((None, D_FF_EXP, D_MODEL), wsel)],
            out_specs=pl.BlockSpec((TM_MOE,) + ROW_TILE, out_tile)),
        out_shape=jax.ShapeDtypeStruct(xs.shape, F32),
        compiler_params=_cparams(("arbitrary",)),
        name="moe",
    )(tile_expert, n_active, xs, w_eg, w_eu, w_ed)


def _combine_kernel(pos_ref, posn_ref, x1_ref, gates_ref, gf_ref, ys_ref, o_ref, buf_ref, sem):
    step = pl.program_id(0)
    slot = step % 2

    def gather(p_ref, s):
        def body(r, carry):
            _row_copy(ys_ref, p_ref[0, r], buf_ref.at[s, 0], r, sem.at[s]).start(priority=0)
            _row_copy(ys_ref, p_ref[0, TT + r], buf_ref.at[s, 1], r, sem.at[s]).start(priority=1)
            return carry
        lax.fori_loop(0, TT, body, 0)

    @pl.when(step == 0)
    def _():
        gather(pos_ref, 0)

    @pl.when(step < pl.num_programs(0) - 1)
    def _():
        gather(posn_ref, 1 - slot)

    def drain(r, carry):
        _row_copy(ys_ref, pos_ref[0, r], buf_ref.at[slot, 0], r, sem.at[slot]).wait()
        _row_copy(ys_ref, pos_ref[0, TT + r], buf_ref.at[slot, 1], r, sem.at[slot]).wait()
        return carry

    lax.fori_loop(0, TT, drain, 0)
    gates = gates_ref[...]
    y = (x1_ref[...] + gates[:, 0:1] * _from_token_major(buf_ref[slot, 0])
         + gates[:, 1:2] * _from_token_major(buf_ref[slot, 1]))
    ms = jnp.mean(y * y, axis=-1, keepdims=True)
    o_ref[...] = y * lax.rsqrt(ms + EPS) * gf_ref[...]


def _combine(pos, x1, gates, gf, ys):
    n = x1.shape[0]
    nblk = n // TT
    return pl.pallas_call(
        _combine_kernel,
        grid=(nblk,),
        in_specs=[pl.BlockSpec((None, 1, 2 * TT), lambda i: (i, 0, 0), memory_space=pltpu.SMEM),
                  pl.BlockSpec((None, 1, 2 * TT), lambda i: (jnp.minimum(i + 1, nblk - 1), 0, 0),
                               memory_space=pltpu.SMEM),
                  pl.BlockSpec((TT, D_MODEL), lambda i: (i, 0)),
                  pl.BlockSpec((TT, LANES), lambda i: (i, 0)),
                  pl.BlockSpec((1, D_MODEL), lambda i: (0, 0)),
                  pl.BlockSpec(memory_space=pl.ANY)],
        out_specs=pl.BlockSpec((TT, D_MODEL), lambda i: (i, 0)),
        out_shape=jax.ShapeDtypeStruct((n, D_MODEL), F32),
        scratch_shapes=[pltpu.VMEM((2, 2, TT) + ROW_TILE, F32),
                        pltpu.SemaphoreType.DMA((2,))],
        compiler_params=_cparams(("arbitrary",)),
        name="combine",
    )(pos, pos, x1, gates, gf, ys)


def kernel(x, w_in, conv_w, conv_b, w_mq, w_mk, w_mgate, b_mgate, m_norm_g, m_skip, lambda_qk, da_norm_g,
           rel_bias, w_out, norm1_g, norm2_g, w_rg, b_rg, w_re, b_re, w_eg, w_eu, w_ed, normf_g):
    bsz, seq, _ = x.shape
    n = bsz * seq
    assert seq % TQ == 0 and seq % CHUNK == 0 and n % TM_PROJ == 0 and n % TT == 0 and bsz % MLSTM_BATCH == 0
    l = 0
    x2 = x.reshape(n, D_MODEL)

    w_in_b = w_in[l].astype(BF16)
    wqv_t = jnp.stack([w_in_b[:, 3 * 512:4 * 512].T, w_in_b[:, 5 * 512:6 * 512].T,
                       w_in_b[:, 1 * 512:2 * 512].T])
    c, vmt, z, qt, kd, vt = _inproj(x2, norm1_g[l][None, :], w_in_b, wqv_t, conv_w[l], conv_b[l][None, :], seq)

    wgt = jnp.zeros((2 * SUBLANES, 3 * D_MLSTM), F32)
    wgt = wgt.at[0:ML_HEADS].set(w_mgate[l][:, :ML_HEADS].T)
    wgt = wgt.at[SUBLANES:SUBLANES + ML_HEADS].set(w_mgate[l][:, ML_HEADS:].T).astype(BF16)
    bgt = jnp.zeros((2 * SUBLANES, 1), F32)
    bgt = bgt.at[0:ML_HEADS, 0].set(b_mgate[l][:ML_HEADS])
    bgt = bgt.at[SUBLANES:SUBLANES + ML_HEADS, 0].set(b_mgate[l][ML_HEADS:])
    y_m = _mlstm(c, vmt, z, jnp.swapaxes(w_mq[l], 1, 2).astype(BF16),
                 w_mk[l].astype(BF16), wgt, bgt, m_norm_g[l][None, :], m_skip[l][None, :], bsz, seq)

    y_d = _attn(lambda_qk[l].astype(F32), qt, kd, vt, _rel_bias_tiles(rel_bias),
                da_norm_g[l][:, None], bsz, seq)

    wr = jnp.zeros((D_MODEL, LANES), F32)
    wr = wr.at[:, :N_GROUPS].set(w_rg[l])
    wr = wr.at[:, SUBLANES:SUBLANES + N_EXPERTS].set(w_re[l].reshape(D_MODEL, N_EXPERTS))
    wr_hi = wr.astype(BF16)
    wr_lo = (wr - wr_hi.astype(F32)).astype(BF16)
    wr3 = jnp.concatenate([wr_hi, wr_hi, wr_lo], axis=0)
    br = jnp.zeros((1, LANES), F32)
    br = br.at[0, :N_GROUPS].set(b_rg[l])
    br = br.at[0, SUBLANES:SUBLANES + N_EXPERTS].set(b_re[l].reshape(N_EXPERTS))
    x1, h2, meta, gates, counts = _outproj(x2, y_m, y_d, w_out[l].astype(BF16), norm2_g[l][None, :], wr3, br)

    cnt = counts[:, 0]
    tiles_e = (cnt + TM_MOE - 1) // TM_MOE
    ends_t = jnp.cumsum(tiles_e)
    ends = jnp.concatenate([jnp.zeros((1,), jnp.int32), ends_t * TM_MOE]).astype(jnp.int32)
    offs = ends[:-1]
    n_tiles = (2 * n) // TM_MOE + N_EXPERTS
    n_active = ends_t[-1:].astype(jnp.int32)
    tile_ids = jnp.minimum(jnp.arange(n_tiles, dtype=jnp.int32), n_active[0] - 1)
    tile_expert = jnp.sum(tile_ids[:, None] >= ends_t[None, :], axis=1).astype(jnp.int32)
    eids = jnp.arange(N_EXPERTS, dtype=jnp.int32)[:, None]
    pos1 = jnp.sum(jnp.where(meta[0][None, :] == eids, offs[:, None], 0), axis=0) + meta[2]
    pos2 = jnp.sum(jnp.where(meta[1][None, :] == eids, offs[:, None], 0), axis=0) + meta[3]
    pos = jnp.concatenate([pos1.reshape(n // TT, 1, TT), pos2.reshape(n // TT, 1, TT)], axis=2)

    xs = _dispatch(ends, pos, h2, n_tiles * TM_MOE)
    ys = _moe(tile_expert, n_active, xs, w_eg[l].astype(BF16), w_eu[l].astype(BF16), w_ed[l].astype(BF16))
    out = _combine(pos, x1, gates, normf_g[None, :], ys)
    return out.reshape(bsz, seq, D_MODEL)
```

```python
import functools
import math

import jax
import jax.numpy as jnp
from jax import lax
from jax.experimental import pallas as pl
from jax.experimental.pallas import tpu as pltpu

F32 = jnp.float32
BF16 = jnp.bfloat16

D_MODEL = 1024
D_MLSTM = 512
D_DIFF = 512
ML_HEADS = 4
ML_HD = 128
CONV_W = 4
CHUNK = 128
DA_HEADS = 4
DA_VD = 128
DA_QD = 64
REL_BUCKETS = 32
REL_MAX_DIST = 128
N_GROUPS = 4
EXP_PER_GROUP = 8
N_EXPERTS = 32
D_FF_EXP = 512
EPS = 1e-6
SUBLN_EPS = 1e-5
LAM_INIT = 0.8 - 0.6 * math.exp(-0.3 * 0)
NEG = -1e30
LOG2E = math.log2(math.e)

LANES = 128
SUBLANES = 8
MXU_COLS = 256

TM_PROJ = 512
TQ = 512
ACC_ROWS = DA_VD + 16
MLSTM_BATCH = 4
TM_MOE = 512
MOE_SUBTILES = 2
TT = 512
VMEM_LIMIT = 48 * 1024 * 1024


ROW_TILE = (D_MODEL // LANES, LANES)


def _cparams(sem):
    return pltpu.CompilerParams(dimension_semantics=sem, vmem_limit_bytes=VMEM_LIMIT)


def _to_token_major(x):
    return pltpu.einshape("a(bc)->abc", x, b=ROW_TILE[0])


def _from_token_major(x):
    return pltpu.einshape("abc->a(bc)", x)


def _inproj_kernel(tiles_per_seq, x_ref, g_ref, w_ref, wqv_ref, cw_ref, cb_ref,
                   c_ref, vm_ref, z_ref, q_ref, k_ref, v_ref, ext_ref):
    x = x_ref[...]
    ms = jnp.mean(x * x, axis=-1, keepdims=True)
    h = (x * lax.rsqrt(ms + EPS) * g_ref[...]).astype(BF16)
    z_ref[...] = jnp.dot(h, w_ref[:, 2 * 512:3 * 512], preferred_element_type=F32)
    k_ref[...] = jnp.dot(h, w_ref[:, 4 * 512:5 * 512], preferred_element_type=F32).astype(k_ref.dtype)

    @pl.when(pl.program_id(0) % tiles_per_seq == 0)
    def _():
        ext_ref[0:SUBLANES, :] = jnp.zeros((SUBLANES, D_MLSTM), F32)

    tm = x.shape[0]
    ext_ref[SUBLANES:SUBLANES + tm, :] = jnp.dot(h, w_ref[:, 0:512], preferred_element_type=F32)
    cw = cw_ref[...]
    conv = cb_ref[...] + cw[CONV_W - 1:CONV_W, :] * ext_ref[SUBLANES:SUBLANES + tm, :]
    for k in range(1, CONV_W):
        conv = conv + cw[CONV_W - 1 - k:CONV_W - k, :] * ext_ref[SUBLANES - k:SUBLANES - k + tm, :]
    ext_ref[0:SUBLANES, :] = ext_ref[tm:tm + SUBLANES, :]
    c_ref[...] = conv * jax.nn.sigmoid(conv)
    nt = (((1,), (1,)), ((), ()))
    qt = lax.dot_general(wqv_ref[0], h, nt, preferred_element_type=F32)
    q_ref[...] = (qt * (DA_QD ** -0.5 * LOG2E)).astype(q_ref.dtype)
    v_ref[...] = lax.dot_general(wqv_ref[1], h, nt, preferred_element_type=F32).astype(v_ref.dtype)
    vm_ref[...] = lax.dot_general(wqv_ref[2], h, nt, preferred_element_type=F32).astype(vm_ref.dtype)


def _inproj(x2, g1, w_in, wqv_t, conv_w, conv_b, seq):
    n = x2.shape[0]
    assert seq % TM_PROJ == 0
    row = lambda i: (i, 0)
    fixed = lambda i: (0, 0)
    sds = lambda dt: jax.ShapeDtypeStruct((n, 512), dt)
    sds_t = jax.ShapeDtypeStruct((512, n), BF16)
    blk = pl.BlockSpec((TM_PROJ, 512), row)
    blk_t = pl.BlockSpec((512, TM_PROJ), lambda i: (0, i))
    return pl.pallas_call(
        functools.partial(_inproj_kernel, seq // TM_PROJ),
        grid=(n // TM_PROJ,),
        in_specs=[pl.BlockSpec((TM_PROJ, D_MODEL), row),
                  pl.BlockSpec((1, D_MODEL), fixed),
                  pl.BlockSpec((D_MODEL, 6 * 512), fixed),
                  pl.BlockSpec((3, 512, D_MODEL), lambda i: (0, 0, 0)),
                  pl.BlockSpec((CONV_W, D_MLSTM), fixed),
                  pl.BlockSpec((1, D_MLSTM), fixed)],
        out_specs=[blk, blk_t, blk, blk_t, blk, blk_t],
        out_shape=[sds(F32), sds_t, sds(F32), sds_t, sds(BF16), sds_t],
        scratch_shapes=[pltpu.VMEM((SUBLANES + TM_PROJ, D_MLSTM), F32)],
        compiler_params=_cparams(("arbitrary",)),
        name="inproj",
    )(x2, g1, w_in, wqv_t, conv_w, conv_b)


def _mlstm_kernel(*refs):
    c_ref = refs[0]
    vt_refs = refs[1:1 + MLSTM_BATCH]
    (z_ref, wq_ref, wk_ref, wg_ref, bg_ref,
     ng_ref, sk_ref, tril_ref, y_ref, st_ref, m_ref) = refs[1 + MLSTM_BATCH:]
    j = pl.program_id(1)

    @pl.when(j == 0)
    def _():
        st_ref[...] = jnp.zeros(st_ref.shape, F32)
        m_ref[...] = jnp.zeros(m_ref.shape, F32)

    seqs = range(MLSTM_BATCH)
    proj = [_mlstm_gates(c_ref.at[bb], vt_refs[bb], wq_ref, wk_ref, wg_ref, bg_ref, tril_ref) for bb in seqs]
    for h in range(ML_HEADS):
        for bb in seqs:
            _mlstm_head(h, proj[bb], z_ref.at[bb], ng_ref, sk_ref, y_ref.at[bb], st_ref.at[bb], m_ref.at[bb])


def _mlstm_gates(c_ref, vm_ref, wq_ref, wk_ref, wg_ref, bg_ref, tril_ref):
    L = CHUNK
    c_act = c_ref[...]

    nt = (((1,), (1,)), ((), ()))
    vt = vm_ref[...]
    qts, ks = [], []
    gates_t = jnp.dot(wg_ref[:, 2 * D_MLSTM:3 * D_MLSTM], vt, preferred_element_type=F32) + bg_ref[...]
    for h in range(ML_HEADS):
        hs = slice(h * ML_HD, (h + 1) * ML_HD)
        ch = c_act[:, hs].astype(BF16)
        qt = lax.dot_general(wq_ref[h], ch, nt, preferred_element_type=F32)
        kh = jnp.dot(ch, wk_ref[h], preferred_element_type=F32)
        qts.append(qt.astype(BF16))
        ks.append(kh)
        gates_t = gates_t + jnp.dot(wg_ref[:, hs], qts[h], preferred_element_type=F32)
        gates_t = gates_t + lax.dot_general(wg_ref[:, D_MLSTM + h * ML_HD:D_MLSTM + (h + 1) * ML_HD],
                                            kh.astype(BF16), nt, preferred_element_type=F32)

    logf_t = jnp.minimum(gates_t, 0.0) - jnp.log(1.0 + jnp.exp(-jnp.abs(gates_t)))
    bcum_t = jnp.dot(logf_t, tril_ref[...], preferred_element_type=F32, precision=lax.Precision.HIGHEST)
    a8 = gates_t[0:SUBLANES] - bcum_t[SUBLANES:2 * SUBLANES]
    a_cols = jnp.concatenate([a8, jnp.zeros((LANES - SUBLANES, L), F32)], axis=0).T
    return dict(c_act=c_act, vt=vt, qts=qts, ks=ks, gates_t=gates_t, bcum_t=bcum_t, a_cols=a_cols)


def _mlstm_head(h, p, z_ref, ng_ref, sk_ref, y_ref, st_ref, m_ref):
    L = CHUNK
    c_act, vt, gates_t, bcum_t, a_cols = p["c_act"], p["vt"], p["gates_t"], p["bcum_t"], p["a_cols"]
    s_idx = lax.broadcasted_iota(jnp.int32, (L, L), 0)
    t_idx = lax.broadcasted_iota(jnp.int32, (L, L), 1)
    causal = s_idx <= t_idx
    ones_row = jnp.where(lax.broadcasted_iota(jnp.int32, (ML_HD, L), 0) == 0, 1.0, 0.0).astype(BF16)
    sl = slice(h * ML_HD, (h + 1) * ML_HD)
    qt = p["qts"][h]
    k = (p["ks"][h] * (ML_HD ** -0.5)).astype(BF16)
    vaug_t = jnp.concatenate([vt[sl, :], ones_row], axis=0)
    i_row = gates_t[h:h + 1, :]
    b_row = bcum_t[SUBLANES + h:SUBLANES + h + 1, :]
    a_row = i_row - b_row
    m_prev = m_ref[h][0:1, 0:1]
    state_t = st_ref[h]

    e = jnp.where(causal, a_cols[:, h:h + 1], NEG)
    cm = jnp.max(e, axis=0, keepdims=True)
    mm = jnp.maximum(m_prev, cm)
    dexp = jnp.exp(e - mm)
    s_t = jnp.dot(k, qt, preferred_element_type=F32)
    w_t = (dexp * s_t).astype(BF16)
    sp = jnp.exp(m_prev - mm)
    r_inter = jnp.dot(state_t.astype(BF16), qt, preferred_element_type=F32)
    r_intra = jnp.dot(vaug_t, w_t, preferred_element_type=F32)
    tot = sp * r_inter + r_intra
    num = tot[:ML_HD]
    den = tot[ML_HD:ML_HD + 1]
    h_t = num / jnp.maximum(jnp.abs(den), jnp.exp(-(b_row + mm)))

    b_end = b_row[:, L - 1:L]
    mm_end = mm[:, L - 1:L]
    wk = jnp.exp(a_row - mm_end)
    decay = jnp.exp(m_prev - mm_end)
    vw = (vaug_t.astype(F32) * wk).astype(BF16)
    upd = jnp.dot(vw, k, preferred_element_type=F32)
    st_ref[h] = decay * state_t + upd
    m_ref[h] = jnp.broadcast_to(b_end + mm_end, (SUBLANES, LANES))

    mu = jnp.mean(h_t, axis=0, keepdims=True)
    xc = h_t - mu
    var = jnp.mean(xc * xc, axis=0, keepdims=True)
    hn = (xc * lax.rsqrt(var + EPS)).T * ng_ref[:, sl]
    hn = hn + sk_ref[:, sl] * c_act[:, sl]
    o = jax.nn.sigmoid(z_ref[:, sl])
    y_ref[:, sl] = (o * hn).astype(y_ref.dtype)


def _mlstm(c, vmt, z, wqt, wk, wgt, bgt, ng, sk, bsz, seq):
    nc = seq // CHUNK
    f2 = lambda b, j: (0, 0)
    f3 = lambda b, j: (0, 0, 0)
    triu = jnp.triu(jnp.ones((CHUNK, CHUNK), F32))
    blk = pl.BlockSpec((MLSTM_BATCH, CHUNK, D_MLSTM), lambda b, j: (b, j, 0))
    vt_specs = [pl.BlockSpec((D_MLSTM, CHUNK), functools.partial(lambda b, j, bb: (0, (b * MLSTM_BATCH + bb) * nc + j),
                                                                  bb=bb))
                for bb in range(MLSTM_BATCH)]
    c, z = (a.reshape(bsz, seq, D_MLSTM) for a in (c, z))
    y = pl.pallas_call(
        _mlstm_kernel,
        grid=(bsz // MLSTM_BATCH, nc),
        in_specs=[blk] + vt_specs + [
                  blk,
                  pl.BlockSpec((ML_HEADS, ML_HD, ML_HD), f3),
                  pl.BlockSpec((ML_HEADS, ML_HD, ML_HD), f3),
                  pl.BlockSpec((2 * SUBLANES, 3 * D_MLSTM), f2),
                  pl.BlockSpec((2 * SUBLANES, 1), f2),
                  pl.BlockSpec((1, D_MLSTM), f2),
                  pl.BlockSpec((1, D_MLSTM), f2),
                  pl.BlockSpec((CHUNK, CHUNK), f2)],
        out_specs=blk,
        out_shape=jax.ShapeDtypeStruct((bsz, seq, D_MLSTM), BF16),
        scratch_shapes=[pltpu.VMEM((MLSTM_BATCH, ML_HEADS, 2 * ML_HD, ML_HD), F32),
                        pltpu.VMEM((MLSTM_BATCH, ML_HEADS, SUBLANES, LANES), F32)],
        compiler_params=_cparams(("parallel", "arbitrary")),
        name="mlstm",
    )(c, *([vmt] * MLSTM_BATCH), z, wqt, wk, wgt, bgt, ng, sk, triu)
    return y.reshape(bsz * seq, D_MLSTM)


def _attn_kernel(lq_ref, qt_ref, k_ref, vt_ref, bias_ref, g_ref, o_ref,
                 acc_ref, m_ref, q2_ref, s_ref, p_ref, a_ref):
    tq = TQ
    i = pl.program_id(2)
    qt = qt_ref[...]
    row = lax.broadcasted_iota(jnp.int32, (LANES, tq), 0)
    zero = jnp.zeros_like(qt)
    q2_ref[:, 0:tq] = jnp.where(row < DA_QD, qt, zero)
    q2_ref[:, tq:2 * tq] = jnp.where(row >= DA_QD, qt, zero)
    acc_ref[...] = jnp.zeros(acc_ref.shape, F32)
    m_ref[...] = jnp.full(m_ref.shape, NEG, F32)

    p_ref[1] = jnp.zeros(p_ref.shape[1:], BF16)
    a_ref[1] = jnp.ones(a_ref.shape[1:], F32)

    def logits(kb, buf):
        start = pl.multiple_of(kb * tq, tq)
        s_ref[buf] = jnp.dot(k_ref[pl.ds(start, tq), :], q2_ref[...], preferred_element_type=F32)

    ones_rows = jnp.ones((ACC_ROWS - DA_VD, tq), BF16)

    def values(kb, buf):
        start = pl.multiple_of(jnp.maximum(kb, 0) * tq, tq)
        lhs = jnp.concatenate([vt_ref[:, pl.ds(start, tq)], ones_rows], axis=0)
        pv = jnp.dot(lhs, p_ref[buf], preferred_element_type=F32)
        acc_ref[...] = a_ref[buf] * acc_ref[...] + pv

    def softmax(buf, which_bias):
        s = s_ref[buf]
        if which_bias is not None:
            bias = bias_ref[which_bias]
            s = s + jnp.concatenate([bias, bias], axis=1)
        m_old = m_ref[...]
        m_new = jnp.maximum(m_old, jnp.max(s, axis=0, keepdims=True))
        m_ref[...] = m_new
        a_ref[buf] = jnp.exp2(m_old - m_new)
        p_ref[buf] = jnp.exp2(s - m_new).astype(BF16)

    def stage(kb, buf, has_next, which_bias):
        softmax(buf, which_bias)
        if has_next:
            logits(kb + 1, 1 - buf)
        values(kb - 1, 1 - buf)

    logits(0, 0)
    n_far = jnp.maximum(i - 1, 0)

    def far_pair(t, carry):
        stage(2 * t, 0, True, None)
        stage(2 * t + 1, 1, True, None)
        return carry

    lax.fori_loop(0, n_far // 2, far_pair, 0)
    kb0 = 2 * (n_far // 2)

    @pl.when(i == 0)
    def _():
        stage(0, 0, False, 0)
        values(0, 0)

    @pl.when(jnp.logical_and(i >= 1, n_far % 2 == 0))
    def _():
        stage(kb0, 0, True, 1)
        stage(kb0 + 1, 1, False, 0)
        values(kb0 + 1, 1)

    @pl.when(n_far % 2 == 1)
    def _():
        stage(kb0, 0, True, None)
        stage(kb0 + 1, 1, True, 1)
        stage(kb0 + 2, 0, False, 0)
        values(kb0 + 2, 0)

    o = acc_ref[0:DA_VD, :] * (1.0 / acc_ref[DA_VD:DA_VD + 1, :])
    lq = lq_ref[...]
    d01 = jnp.sum(lq[0:1, :] * lq[1:2, :], axis=-1, keepdims=True)
    d23 = jnp.sum(lq[2:3, :] * lq[3:4, :], axis=-1, keepdims=True)
    lam = jnp.exp(d01) - jnp.exp(d23) + LAM_INIT
    y = o[:, :tq] - lam * o[:, tq:]
    y = y * lax.rsqrt(jnp.mean(y * y, axis=0, keepdims=True) + SUBLN_EPS) * g_ref[...]
    o_ref[...] = (y * (1.0 - LAM_INIT)).T.astype(o_ref.dtype)


def _rel_bias_tiles(rel_bias):
    L = TQ
    max_exact = REL_BUCKETS // 2
    n = jnp.arange(-(L - 1), 2 * L, dtype=jnp.int32)
    nn = jnp.maximum(n, 0)
    large = max_exact + (jnp.log(jnp.maximum(nn, 1).astype(F32) / max_exact)
                         / math.log(REL_MAX_DIST / max_exact) * (REL_BUCKETS - max_exact)).astype(jnp.int32)
    large = jnp.minimum(large, REL_BUCKETS - 1)
    bucket = jnp.where(nn < max_exact, nn, large)
    rb = (rel_bias.astype(F32) - rel_bias[REL_BUCKETS - 1].astype(F32)[None, :]).T
    t = jnp.where(n[None, :] >= 0, rb[:, bucket] * LOG2E, NEG)

    def toeplitz(v):
        vp = jnp.pad(v, ((0, 0), (0, 1)))
        flat = jnp.tile(vp, (1, L))[:, :L * (2 * L - 1)]
        return flat.reshape(v.shape[0], L, 2 * L - 1)[:, :, L - 1:]

    return jnp.stack([toeplitz(t[:, :2 * L - 1]), toeplitz(t[:, L:])], axis=1)


def _attn(lq, qt, kd, vt, bias, g, bsz, seq):
    nq = seq // TQ
    return pl.pallas_call(
        _attn_kernel,
        grid=(bsz, DA_HEADS, nq),
        in_specs=[pl.BlockSpec((4, DA_QD), lambda b, h, i: (0, 0)),
                  pl.BlockSpec((LANES, TQ), lambda b, h, i: (h, b * nq + i)),
                  pl.BlockSpec((seq, LANES), lambda b, h, i: (b, h)),
                  pl.BlockSpec((LANES, seq), lambda b, h, i: (h, b)),
                  pl.BlockSpec((None, 2, TQ, TQ), lambda b, h, i: (h, 0, 0, 0)),
                  pl.BlockSpec((DA_VD, 1), lambda b, h, i: (0, 0))],
        out_specs=pl.BlockSpec((TQ, LANES), lambda b, h, i: (b * nq + i, h)),
        out_shape=jax.ShapeDtypeStruct((bsz * seq, D_DIFF), BF16),
        scratch_shapes=[pltpu.VMEM((ACC_ROWS, 2 * TQ), F32),
                        pltpu.VMEM((1, 2 * TQ), F32),
                        pltpu.VMEM((LANES, 2 * TQ), BF16),
                        pltpu.VMEM((2, TQ, 2 * TQ), F32),
                        pltpu.VMEM((2, TQ, 2 * TQ), BF16),
                        pltpu.VMEM((2, 1, 2 * TQ), F32)],
        compiler_params=_cparams(("parallel", "parallel", "arbitrary")),
        name="attn",
    )(lq, qt, kd, vt, bias, g)


def _outproj_kernel(x_ref, ym_ref, yd_ref, wo_ref, g2_ref, wr_ref, br_ref, tri_ref,
                    x1_ref, h2_ref, meta_ref, gates_ref, cnt_ref, run_ref):
    tm = TM_PROJ
    step = pl.program_id(0)

    @pl.when(step == 0)
    def _():
        run_ref[...] = jnp.zeros(run_ref.shape, F32)

    x1 = (x_ref[...]
          + jnp.dot(ym_ref[...], wo_ref[0:D_MLSTM, :], preferred_element_type=F32)
          + jnp.dot(yd_ref[...], wo_ref[D_MLSTM:, :], preferred_element_type=F32))
    x1_ref[...] = x1
    ms = jnp.mean(x1 * x1, axis=-1, keepdims=True)
    h2 = x1 * lax.rsqrt(ms + EPS) * g2_ref[...]
    h2_ref[...] = _to_token_major(h2)

    h_hi = h2.astype(BF16)
    h_lo = (h2 - h_hi.astype(F32)).astype(BF16)
    lhs = jnp.concatenate([h_hi, h_lo, h_hi], axis=1)
    logits = jnp.dot(lhs, wr_ref[...], preferred_element_type=F32) + br_ref[...]
    lt = logits.T

    row8 = lax.broadcasted_iota(jnp.int32, (SUBLANES, tm), 0)
    lg = jnp.where(row8 < N_GROUPS, lt[0:SUBLANES], NEG)
    gmax = jnp.max(lg, axis=0, keepdims=True)
    gidx = jnp.min(jnp.where(lg == gmax, row8, SUBLANES), axis=0, keepdims=True)
    pg_sel = 1.0 / jnp.sum(jnp.exp(lg - gmax), axis=0, keepdims=True)
    le = jnp.zeros((EXP_PER_GROUP, tm), F32)
    for g in range(N_GROUPS):
        le = jnp.where(gidx == g, lt[SUBLANES * (g + 1):SUBLANES * (g + 2)], le)
    v1 = jnp.max(le, axis=0, keepdims=True)
    i1 = jnp.min(jnp.where(le == v1, row8, SUBLANES), axis=0, keepdims=True)
    le2 = jnp.where(row8 == i1, NEG, le)
    v2 = jnp.max(le2, axis=0, keepdims=True)
    i2 = jnp.min(jnp.where(le2 == v2, row8, SUBLANES), axis=0, keepdims=True)
    e21 = jnp.exp(v2 - v1)
    pw1 = 1.0 / (1.0 + e21)
    gate1 = pg_sel * pw1
    gate2 = pg_sel * (e21 * pw1)
    eid1 = gidx * EXP_PER_GROUP + i1
    eid2 = gidx * EXP_PER_GROUP + i2

    row32 = lax.broadcasted_iota(jnp.int32, (N_EXPERTS, tm), 0)
    oh1 = row32 == eid1
    oh2 = row32 == eid2
    oh = jnp.where(oh1, 1.0, 0.0) + jnp.where(oh2, 1.0, 0.0)
    before = jnp.dot(oh.astype(BF16), tri_ref[...], preferred_element_type=F32) + run_ref[:, 0:1]
    rank1 = jnp.sum(jnp.where(oh1, before, 0.0), axis=0, keepdims=True).astype(jnp.int32)
    rank2 = jnp.sum(jnp.where(oh2, before, 0.0), axis=0, keepdims=True).astype(jnp.int32)
    run = run_ref[...] + jnp.sum(oh, axis=1, keepdims=True)
    run_ref[...] = run
    cnt_ref[...] = run.astype(jnp.int32)

    meta_ref[...] = jnp.where(row8 == 0, eid1, jnp.where(row8 == 1, eid2,
                              jnp.where(row8 == 2, rank1, jnp.where(row8 == 3, rank2, 0))))
    row128 = lax.broadcasted_iota(jnp.int32, (LANES, tm), 0)
    gt = jnp.where(row128 == 0, gate1, jnp.where(row128 == 1, gate2, 0.0))
    gates_ref[...] = gt.T


def _outproj(x2, ym, yd, wo, g2, wr, br):
    n = x2.shape[0]
    row = lambda i: (i, 0)
    fixed = lambda i: (0, 0)
    tri = jnp.triu(jnp.ones((TM_PROJ, TM_PROJ), F32), k=1).astype(BF16)
    return pl.pallas_call(
        _outproj_kernel,
        grid=(n // TM_PROJ,),
        in_specs=[pl.BlockSpec((TM_PROJ, D_MODEL), row),
                  pl.BlockSpec((TM_PROJ, D_MLSTM), row),
                  pl.BlockSpec((TM_PROJ, D_DIFF), row),
                  pl.BlockSpec((D_MODEL, D_MODEL), fixed),
                  pl.BlockSpec((1, D_MODEL), fixed),
                  pl.BlockSpec((3 * D_MODEL, LANES), fixed),
                  pl.BlockSpec((1, LANES), fixed),
                  pl.BlockSpec((TM_PROJ, TM_PROJ), fixed)],
        out_specs=[pl.BlockSpec((TM_PROJ, D_MODEL), row),
                   pl.BlockSpec((TM_PROJ,) + ROW_TILE, lambda i: (i, 0, 0)),
                   pl.BlockSpec((SUBLANES, TM_PROJ), lambda i: (0, i)),
                   pl.BlockSpec((TM_PROJ, LANES), row),
                   pl.BlockSpec((N_EXPERTS, LANES), fixed)],
        out_shape=[jax.ShapeDtypeStruct((n, D_MODEL), F32),
                   jax.ShapeDtypeStruct((n,) + ROW_TILE, F32),
                   jax.ShapeDtypeStruct((SUBLANES, n), jnp.int32),
                   jax.ShapeDtypeStruct((n, LANES), F32),
                   jax.ShapeDtypeStruct((N_EXPERTS, LANES), jnp.int32)],
        scratch_shapes=[pltpu.VMEM((N_EXPERTS, LANES), F32)],
        compiler_params=_cparams(("arbitrary",)),
        name="outproj_router",
    )(x2, ym, yd, wo, g2, wr, br, tri)


def _row_copy(src_ref, src_row, dst_ref, dst_row, sem):
    return pltpu.make_async_copy(src_ref.at[pl.ds(src_row, 1)], dst_ref.at[pl.ds(dst_row, 1)], sem)


def _dispatch_kernel(ends_ref, pos_ref, h_ref, xs_ref, zero_ref, hbuf_ref, sem, bsem, zsem):
    step = pl.program_id(0)

    @pl.when(step == 0)
    def _():
        zero_ref[...] = jnp.zeros(zero_ref.shape, F32)

        def tile_copy(e):
            start = pl.multiple_of(ends_ref[e + 1] - TM_MOE, TM_MOE)
            return pltpu.make_async_copy(zero_ref, xs_ref.at[pl.ds(start, TM_MOE)], zsem)

        def zstart(e, carry):
            @pl.when(ends_ref[e + 1] > ends_ref[e])
            def _():
                tile_copy(e).start()
            return carry

        def zwait(e, carry):
            @pl.when(ends_ref[e + 1] > ends_ref[e])
            def _():
                tile_copy(e).wait()
            return carry

        lax.fori_loop(0, N_EXPERTS, zstart, 0)
        lax.fori_loop(0, N_EXPERTS, zwait, 0)

        def tail_copy(t):
            return pltpu.make_async_copy(zero_ref, xs_ref.at[pl.ds(pl.multiple_of(t * TM_MOE, TM_MOE), TM_MOE)], zsem)

        def tstart(t, carry):
            tail_copy(t).start()
            return carry

        def twait(t, carry):
            tail_copy(t).wait()
            return carry

        first_tail = ends_ref[N_EXPERTS] // TM_MOE
        lax.fori_loop(first_tail, xs_ref.shape[0] // TM_MOE, tstart, 0)
        lax.fori_loop(first_tail, xs_ref.shape[0] // TM_MOE, twait, 0)

    last = pl.num_programs(0) - 1
    slot = step % 3
    par = step % 2

    def block_copy(i, s):
        return pltpu.make_async_copy(h_ref.at[pl.ds(pl.multiple_of(i * TT, TT), TT)], hbuf_ref.at[s], bsem.at[s])

    @pl.when(step == 0)
    def _():
        block_copy(0, 0).start()

    @pl.when(step < last)
    def _():
        block_copy(step + 1, (step + 1) % 3).start()

    block_copy(step, slot).wait()
    src = hbuf_ref.at[slot]

    def issue(r, carry):
        _row_copy(src, r, xs_ref, pos_ref[0, r], sem.at[par]).start(priority=0)
        _row_copy(src, r, xs_ref, pos_ref[0, TT + r], sem.at[par]).start(priority=1)
        return carry

    def drain(s):
        for _ in range(2):
            pltpu.make_async_copy(src, xs_ref.at[pl.ds(0, TT)], sem.at[s]).wait()

    lax.fori_loop(0, TT, issue, 0)

    @pl.when(step > 0)
    def _():
        drain(1 - par)

    @pl.when(step == last)
    def _():
        drain(par)


def _dispatch(ends, pos, h2, n_rows):
    n = h2.shape[0]
    return pl.pallas_call(
        _dispatch_kernel,
        grid_spec=pltpu.PrefetchScalarGridSpec(
            num_scalar_prefetch=1,
            grid=(n // TT,),
            in_specs=[pl.BlockSpec((None, 1, 2 * TT), lambda i, ends: (i, 0, 0), memory_space=pltpu.SMEM),
                      pl.BlockSpec(memory_space=pl.ANY)],
            out_specs=pl.BlockSpec(memory_space=pl.ANY),
            scratch_shapes=[pltpu.VMEM((TM_MOE,) + ROW_TILE, F32),
                            pltpu.VMEM((3, TT) + ROW_TILE, F32),
                            pltpu.SemaphoreType.DMA((2,)),
                            pltpu.SemaphoreType.DMA((3,)),
                            pltpu.SemaphoreType.DMA(())]),
        out_shape=jax.ShapeDtypeStruct((n_rows,) + ROW_TILE, F32),
        compiler_params=_cparams(("arbitrary",)),
        name="dispatch",
    )(ends, pos, h2)


def _moe_kernel(texp_ref, nact_ref, x_ref, wg_ref, wu_ref, wd_ref, y_ref):
    t = pl.program_id(0)

    @pl.when(t < nact_ref[0])
    def _():
        half = TM_MOE // MOE_SUBTILES
        hids = []
        for a in range(MOE_SUBTILES):
            x = _from_token_major(x_ref[a * half:(a + 1) * half]).astype(BF16)
            g = jnp.dot(x, wg_ref[...], preferred_element_type=F32)
            u = jnp.dot(x, wu_ref[...], preferred_element_type=F32)
            hids.append((g * jax.nn.sigmoid(g) * u).astype(BF16))
        for a in range(MOE_SUBTILES):
            y = jnp.dot(hids[a], wd_ref[...], preferred_element_type=F32)
            y_ref[a * half:(a + 1) * half] = _to_token_major(y)

    @pl.when(t >= nact_ref[0])
    def _():
        y_ref[...] = jnp.zeros(y_ref.shape, F32)


def _moe(tile_expert, n_active, xs, w_eg, w_eu, w_ed):
    n_tiles = xs.shape[0] // TM_MOE
    tile = lambda t, te, na: (jnp.minimum(t, na[0] - 1), 0, 0)
    out_tile = lambda t, te, na: (t, 0, 0)
    wsel = lambda t, te, na: (te[t], 0, 0)
    return pl.pallas_call(
        _moe_kernel,
        grid_spec=pltpu.PrefetchScalarGridSpec(
            num_scalar_prefetch=2,
            grid=(n_tiles,),
            in_specs=[pl.BlockSpec((TM_MOE,) + ROW_TILE, tile),
                      pl.BlockSpec((None, D_MODEL, D_FF_EXP), wsel),
                      pl.BlockSpec((None, D_MODEL, D_FF_EXP), wsel),
                      pl.BlockSpec((None, D_FF_EXP, D_MODEL), wsel)],
            out_specs=pl.BlockSpec((TM_MOE,) + ROW_TILE, out_tile)),
        out_shape=jax.ShapeDtypeStruct(xs.shape, F32),
        compiler_params=_cparams(("arbitrary",)),
        name="moe",
    )(tile_expert, n_active, xs, w_eg, w_eu, w_ed)


def _combine_kernel(pos_ref, posn_ref, x1_ref, gates_ref, gf_ref, ys_ref, o_ref, buf_ref, sem):
    step = pl.program_id(0)
    slot = step % 2

    def gather(p_ref, s):
        def body(r, carry):
            _row_copy(ys_ref, p_ref[0, r], buf_ref.at[s, 0], r, sem.at[s]).start(priority=0)
            _row_copy(ys_ref, p_ref[0, TT + r], buf_ref.at[s, 1], r, sem.at[s]).start(priority=1)
            return carry
        lax.fori_loop(0, TT, body, 0)

    @pl.when(step == 0)
    def _():
        gather(pos_ref, 0)

    @pl.when(step < pl.num_programs(0) - 1)
    def _():
        gather(posn_ref, 1 - slot)

    for k in range(2):
        pltpu.make_async_copy(ys_ref.at[pl.ds(0, TT)], buf_ref.at[slot, k], sem.at[slot]).wait()
    gates = gates_ref[...]
    y = (x1_ref[...] + gates[:, 0:1] * _from_token_major(buf_ref[slot, 0])
         + gates[:, 1:2] * _from_token_major(buf_ref[slot, 1]))
    ms = jnp.mean(y * y, axis=-1, keepdims=True)
    o_ref[...] = y * lax.rsqrt(ms + EPS) * gf_ref[...]


def _combine(pos, x1, gates, gf, ys):
    n = x1.shape[0]
    nblk = n // TT
    return pl.pallas_call(
        _combine_kernel,
        grid=(nblk,),
        in_specs=[pl.BlockSpec((None, 1, 2 * TT), lambda i: (i, 0, 0), memory_space=pltpu.SMEM),
                  pl.BlockSpec((None, 1, 2 * TT), lambda i: (jnp.minimum(i + 1, nblk - 1), 0, 0),
                               memory_space=pltpu.SMEM),
                  pl.BlockSpec((TT, D_MODEL), lambda i: (i, 0)),
                  pl.BlockSpec((TT, LANES), lambda i: (i, 0)),
                  pl.BlockSpec((1, D_MODEL), lambda i: (0, 0)),
                  pl.BlockSpec(memory_space=pl.ANY)],
        out_specs=pl.BlockSpec((TT, D_MODEL), lambda i: (i, 0)),
        out_shape=jax.ShapeDtypeStruct((n, D_MODEL), F32),
        scratch_shapes=[pltpu.VMEM((2, 2, TT) + ROW_TILE, F32),
                        pltpu.SemaphoreType.DMA((2,))],
        compiler_params=_cparams(("arbitrary",)),
        name="combine",
    )(pos, pos, x1, gates, gf, ys)


def kernel(x, w_in, conv_w, conv_b, w_mq, w_mk, w_mgate, b_mgate, m_norm_g, m_skip, lambda_qk, da_norm_g,
           rel_bias, w_out, norm1_g, norm2_g, w_rg, b_rg, w_re, b_re, w_eg, w_eu, w_ed, normf_g):
    bsz, seq, _ = x.shape
    n = bsz * seq
    assert seq % TQ == 0 and seq % CHUNK == 0 and n % TM_PROJ == 0 and n % TT == 0 and bsz % MLSTM_BATCH == 0
    l = 0
    x2 = x.reshape(n, D_MODEL)

    w_in_b = w_in[l].astype(BF16)
    wqv_t = jnp.stack([w_in_b[:, 3 * 512:4 * 512].T, w_in_b[:, 5 * 512:6 * 512].T,
                       w_in_b[:, 1 * 512:2 * 512].T])
    c, vmt, z, qt, kd, vt = _inproj(x2, norm1_g[l][None, :], w_in_b, wqv_t, conv_w[l], conv_b[l][None, :], seq)

    wgt = jnp.zeros((2 * SUBLANES, 3 * D_MLSTM), F32)
    wgt = wgt.at[0:ML_HEADS].set(w_mgate[l][:, :ML_HEADS].T)
    wgt = wgt.at[SUBLANES:SUBLANES + ML_HEADS].set(w_mgate[l][:, ML_HEADS:].T).astype(BF16)
    bgt = jnp.zeros((2 * SUBLANES, 1), F32)
    bgt = bgt.at[0:ML_HEADS, 0].set(b_mgate[l][:ML_HEADS])
    bgt = bgt.at[SUBLANES:SUBLANES + ML_HEADS, 0].set(b_mgate[l][ML_HEADS:])
    y_m = _mlstm(c, vmt, z, jnp.swapaxes(w_mq[l], 1, 2).astype(BF16),
                 w_mk[l].astype(BF16), wgt, bgt, m_norm_g[l][None, :], m_skip[l][None, :], bsz, seq)

    y_d = _attn(lambda_qk[l].astype(F32), qt, kd, vt, _rel_bias_tiles(rel_bias),
                da_norm_g[l][:, None], bsz, seq)

    wr = jnp.zeros((D_MODEL, LANES), F32)
    wr = wr.at[:, :N_GROUPS].set(w_rg[l])
    wr = wr.at[:, SUBLANES:SUBLANES + N_EXPERTS].set(w_re[l].reshape(D_MODEL, N_EXPERTS))
    wr_hi = wr.astype(BF16)
    wr_lo = (wr - wr_hi.astype(F32)).astype(BF16)
    wr3 = jnp.concatenate([wr_hi, wr_hi, wr_lo], axis=0)
    br = jnp.zeros((1, LANES), F32)
    br = br.at[0, :N_GROUPS].set(b_rg[l])
    br = br.at[0, SUBLANES:SUBLANES + N_EXPERTS].set(b_re[l].reshape(N_EXPERTS))
    x1, h2, meta, gates, counts = _outproj(x2, y_m, y_d, w_out[l].astype(BF16), norm2_g[l][None, :], wr3, br)

    cnt = counts[:, 0]
    tiles_e = (cnt + TM_MOE - 1) // TM_MOE
    ends_t = jnp.cumsum(tiles_e)
    ends = jnp.concatenate([jnp.zeros((1,), jnp.int32), ends_t * TM_MOE]).astype(jnp.int32)
    offs = ends[:-1]
    n_tiles = (2 * n) // TM_MOE + N_EXPERTS
    n_active = ends_t[-1:].astype(jnp.int32)
    tile_ids = jnp.minimum(jnp.arange(n_tiles, dtype=jnp.int32), n_active[0] - 1)
    tile_expert = jnp.sum(tile_ids[:, None] >= ends_t[None, :], axis=1).astype(jnp.int32)
    eids = jnp.arange(N_EXPERTS, dtype=jnp.int32)[:, None]
    pos1 = jnp.sum(jnp.where(meta[0][None, :] == eids, offs[:, None], 0), axis=0) + meta[2]
    pos2 = jnp.sum(jnp.where(meta[1][None, :] == eids, offs[:, None], 0), axis=0) + meta[3]
    pos = jnp.concatenate([pos1.reshape(n // TT, 1, TT), pos2.reshape(n // TT, 1, TT)], axis=2)

    xs = _dispatch(ends, pos, h2, n_tiles * TM_MOE)
    ys = _moe(tile_expert, n_active, xs, w_eg[l].astype(BF16), w_eu[l].astype(BF16), w_ed[l].astype(BF16))
    out = _combine(pos, x1, gates, normf_g[None, :], ys)
    return out.reshape(bsz, seq, D_MODEL)
```

```python
import functools
import math

import jax
import jax.numpy as jnp
from jax import lax
from jax.experimental import pallas as pl
from jax.experimental.pallas import tpu as pltpu

F32 = jnp.float32
BF16 = jnp.bfloat16

D_MODEL = 1024
D_MLSTM = 512
D_DIFF = 512
ML_HEADS = 4
ML_HD = 128
CONV_W = 4
CHUNK = 128
DA_HEADS = 4
DA_VD = 128
DA_QD = 64
REL_BUCKETS = 32
REL_MAX_DIST = 128
N_GROUPS = 4
EXP_PER_GROUP = 8
N_EXPERTS = 32
D_FF_EXP = 512
EPS = 1e-6
SUBLN_EPS = 1e-5
LAM_INIT = 0.8 - 0.6 * math.exp(-0.3 * 0)
NEG = -1e30
LOG2E = math.log2(math.e)

LANES = 128
SUBLANES = 8
MXU_COLS = 256

TM_PROJ = 512
INPROJ_PARTS = 2
OUTPROJ_PARTS = 2
TQ = 512
ACC_ROWS = DA_VD + 16
MLSTM_BATCH = 4
TM_MOE = 512
MOE_SUBTILES = 2
TT = 512
VMEM_LIMIT = 48 * 1024 * 1024


ROW_TILE = (D_MODEL // LANES, LANES)


def _cparams(sem):
    return pltpu.CompilerParams(dimension_semantics=sem, vmem_limit_bytes=VMEM_LIMIT)


def _to_token_major(x):
    return pltpu.einshape("a(bc)->abc", x, b=ROW_TILE[0])


def _from_token_major(x):
    return pltpu.einshape("abc->a(bc)", x)


def _inproj_kernel(tiles_per_seq, x_ref, g_ref, w_ref, wqv_ref, cw_ref, cb_ref,
                   c_ref, vm_ref, z_ref, q_ref, k_ref, v_ref, ext_ref):
    @pl.when(pl.program_id(0) % tiles_per_seq == 0)
    def _():
        ext_ref[0:SUBLANES, :] = jnp.zeros((SUBLANES, D_MLSTM), F32)

    tm = TM_PROJ // INPROJ_PARTS
    nt = (((1,), (1,)), ((), ()))
    cw = cw_ref[...]

    def normed(a):
        x = x_ref[a * tm:(a + 1) * tm, :]
        ms = jnp.mean(x * x, axis=-1, keepdims=True)
        return (x * lax.rsqrt(ms + EPS) * g_ref[...]).astype(BF16)

    def conv_rows(a):
        r0 = SUBLANES + a * tm
        conv = cb_ref[...] + cw[CONV_W - 1:CONV_W, :] * ext_ref[r0:r0 + tm, :]
        for k in range(1, CONV_W):
            conv = conv + cw[CONV_W - 1 - k:CONV_W - k, :] * ext_ref[r0 - k:r0 - k + tm, :]
        c_ref[a * tm:(a + 1) * tm, :] = conv * jax.nn.sigmoid(conv)

    hs = []
    for a in range(INPROJ_PARTS):
        hs.append(normed(a))
        ext_ref[SUBLANES + a * tm:SUBLANES + (a + 1) * tm, :] = jnp.dot(hs[a], w_ref[:, 0:512],
                                                                          preferred_element_type=F32)
    for a in range(INPROJ_PARTS):
        h = hs[a]
        rs = slice(a * tm, (a + 1) * tm)
        z_ref[rs, :] = jnp.dot(h, w_ref[:, 2 * 512:3 * 512], preferred_element_type=F32)
        k_ref[rs, :] = jnp.dot(h, w_ref[:, 4 * 512:5 * 512], preferred_element_type=F32).astype(k_ref.dtype)
        qt = lax.dot_general(wqv_ref[0], h, nt, preferred_element_type=F32)
        q_ref[:, rs] = (qt * (DA_QD ** -0.5 * LOG2E)).astype(q_ref.dtype)
        v_ref[:, rs] = lax.dot_general(wqv_ref[1], h, nt, preferred_element_type=F32).astype(v_ref.dtype)
        vm_ref[:, rs] = lax.dot_general(wqv_ref[2], h, nt, preferred_element_type=F32).astype(vm_ref.dtype)
        conv_rows(a)
    ext_ref[0:SUBLANES, :] = ext_ref[TM_PROJ:TM_PROJ + SUBLANES, :]


def _inproj(x2, g1, w_in, wqv_t, conv_w, conv_b, seq):
    n = x2.shape[0]
    assert seq % TM_PROJ == 0
    row = lambda i: (i, 0)
    fixed = lambda i: (0, 0)
    sds = lambda dt: jax.ShapeDtypeStruct((n, 512), dt)
    sds_t = jax.ShapeDtypeStruct((512, n), BF16)
    blk = pl.BlockSpec((TM_PROJ, 512), row)
    blk_t = pl.BlockSpec((512, TM_PROJ), lambda i: (0, i))
    return pl.pallas_call(
        functools.partial(_inproj_kernel, seq // TM_PROJ),
        grid=(n // TM_PROJ,),
        in_specs=[pl.BlockSpec((TM_PROJ, D_MODEL), row),
                  pl.BlockSpec((1, D_MODEL), fixed),
                  pl.BlockSpec((D_MODEL, 6 * 512), fixed),
                  pl.BlockSpec((3, 512, D_MODEL), lambda i: (0, 0, 0)),
                  pl.BlockSpec((CONV_W, D_MLSTM), fixed),
                  pl.BlockSpec((1, D_MLSTM), fixed)],
        out_specs=[blk, blk_t, blk, blk_t, blk, blk_t],
        out_shape=[sds(F32), sds_t, sds(F32), sds_t, sds(BF16), sds_t],
        scratch_shapes=[pltpu.VMEM((SUBLANES + TM_PROJ, D_MLSTM), F32)],
        compiler_params=_cparams(("arbitrary",)),
        name="inproj",
    )(x2, g1, w_in, wqv_t, conv_w, conv_b)


def _mlstm_kernel(*refs):
    c_ref = refs[0]
    vt_refs = refs[1:1 + MLSTM_BATCH]
    (z_ref, wq_ref, wk_ref, wg_ref, bg_ref,
     ng_ref, sk_ref, tril_ref, y_ref, st_ref, m_ref) = refs[1 + MLSTM_BATCH:]
    j = pl.program_id(1)

    @pl.when(j == 0)
    def _():
        st_ref[...] = jnp.zeros(st_ref.shape, F32)
        m_ref[...] = jnp.zeros(m_ref.shape, F32)

    seqs = range(MLSTM_BATCH)
    proj = [_mlstm_gates(c_ref.at[bb], vt_refs[bb], wq_ref, wk_ref, wg_ref, bg_ref, tril_ref) for bb in seqs]
    for h in range(ML_HEADS):
        for bb in seqs:
            _mlstm_head(h, proj[bb], z_ref.at[bb], ng_ref, sk_ref, y_ref.at[bb], st_ref.at[bb], m_ref.at[bb])


def _mlstm_gates(c_ref, vm_ref, wq_ref, wk_ref, wg_ref, bg_ref, tril_ref):
    L = CHUNK
    c_act = c_ref[...]

    nt = (((1,), (1,)), ((), ()))
    vt = vm_ref[...]
    qts, ks = [], []
    gates_t = jnp.dot(wg_ref[:, 2 * D_MLSTM:3 * D_MLSTM], vt, preferred_element_type=F32) + bg_ref[...]
    for h in range(ML_HEADS):
        hs = slice(h * ML_HD, (h + 1) * ML_HD)
        ch = c_act[:, hs].astype(BF16)
        qt = lax.dot_general(wq_ref[h], ch, nt, preferred_element_type=F32)
        kh = jnp.dot(ch, wk_ref[h], preferred_element_type=F32)
        qts.append(qt.astype(BF16))
        ks.append(kh)
        gates_t = gates_t + jnp.dot(wg_ref[:, hs], qts[h], preferred_element_type=F32)
        gates_t = gates_t + lax.dot_general(wg_ref[:, D_MLSTM + h * ML_HD:D_MLSTM + (h + 1) * ML_HD],
                                            kh.astype(BF16), nt, preferred_element_type=F32)

    logf_t = jnp.minimum(gates_t, 0.0) - jnp.log(1.0 + jnp.exp(-jnp.abs(gates_t)))
    bcum_t = jnp.dot(logf_t, tril_ref[...], preferred_element_type=F32, precision=lax.Precision.HIGHEST)
    a8 = gates_t[0:SUBLANES] - bcum_t[SUBLANES:2 * SUBLANES]
    a_cols = jnp.concatenate([a8, jnp.zeros((LANES - SUBLANES, L), F32)], axis=0).T
    return dict(c_act=c_act, vt=vt, qts=qts, ks=ks, gates_t=gates_t, bcum_t=bcum_t, a_cols=a_cols)


def _mlstm_head(h, p, z_ref, ng_ref, sk_ref, y_ref, st_ref, m_ref):
    L = CHUNK
    c_act, vt, gates_t, bcum_t, a_cols = p["c_act"], p["vt"], p["gates_t"], p["bcum_t"], p["a_cols"]
    s_idx = lax.broadcasted_iota(jnp.int32, (L, L), 0)
    t_idx = lax.broadcasted_iota(jnp.int32, (L, L), 1)
    causal = s_idx <= t_idx
    ones_row = jnp.where(lax.broadcasted_iota(jnp.int32, (ML_HD, L), 0) == 0, 1.0, 0.0).astype(BF16)
    sl = slice(h * ML_HD, (h + 1) * ML_HD)
    qt = p["qts"][h]
    k = (p["ks"][h] * (ML_HD ** -0.5)).astype(BF16)
    vaug_t = jnp.concatenate([vt[sl, :], ones_row], axis=0)
    i_row = gates_t[h:h + 1, :]
    b_row = bcum_t[SUBLANES + h:SUBLANES + h + 1, :]
    a_row = i_row - b_row
    m_prev = m_ref[h][0:1, 0:1]
    state_t = st_ref[h]

    e = jnp.where(causal, a_cols[:, h:h + 1], NEG)
    cm = jnp.max(e, axis=0, keepdims=True)
    mm = jnp.maximum(m_prev, cm)
    dexp = jnp.exp(e - mm)
    s_t = jnp.dot(k, qt, preferred_element_type=F32)
    w_t = (dexp * s_t).astype(BF16)
    sp = jnp.exp(m_prev - mm)
    r_inter = jnp.dot(state_t.astype(BF16), qt, preferred_element_type=F32)
    r_intra = jnp.dot(vaug_t, w_t, preferred_element_type=F32)
    tot = sp * r_inter + r_intra
    num = tot[:ML_HD]
    den = tot[ML_HD:ML_HD + 1]
    h_t = num / jnp.maximum(jnp.abs(den), jnp.exp(-(b_row + mm)))

    b_end = b_row[:, L - 1:L]
    mm_end = mm[:, L - 1:L]
    wk = jnp.exp(a_row - mm_end)
    decay = jnp.exp(m_prev - mm_end)
    vw = (vaug_t.astype(F32) * wk).astype(BF16)
    upd = jnp.dot(vw, k, preferred_element_type=F32)
    st_ref[h] = decay * state_t + upd
    m_ref[h] = jnp.broadcast_to(b_end + mm_end, (SUBLANES, LANES))

    mu = jnp.mean(h_t, axis=0, keepdims=True)
    xc = h_t - mu
    var = jnp.mean(xc * xc, axis=0, keepdims=True)
    hn = (xc * lax.rsqrt(var + EPS)).T * ng_ref[:, sl]
    hn = hn + sk_ref[:, sl] * c_act[:, sl]
    o = jax.nn.sigmoid(z_ref[:, sl])
    y_ref[:, sl] = (o * hn).astype(y_ref.dtype)


def _mlstm(c, vmt, z, wqt, wk, wgt, bgt, ng, sk, bsz, seq):
    nc = seq // CHUNK
    f2 = lambda b, j: (0, 0)
    f3 = lambda b, j: (0, 0, 0)
    triu = jnp.triu(jnp.ones((CHUNK, CHUNK), F32))
    blk = pl.BlockSpec((MLSTM_BATCH, CHUNK, D_MLSTM), lambda b, j: (b, j, 0))
    vt_specs = [pl.BlockSpec((D_MLSTM, CHUNK), functools.partial(lambda b, j, bb: (0, (b * MLSTM_BATCH + bb) * nc + j),
                                                                  bb=bb))
                for bb in range(MLSTM_BATCH)]
    c, z = (a.reshape(bsz, seq, D_MLSTM) for a in (c, z))
    y = pl.pallas_call(
        _mlstm_kernel,
        grid=(bsz // MLSTM_BATCH, nc),
        in_specs=[blk] + vt_specs + [
                  blk,
                  pl.BlockSpec((ML_HEADS, ML_HD, ML_HD), f3),
                  pl.BlockSpec((ML_HEADS, ML_HD, ML_HD), f3),
                  pl.BlockSpec((2 * SUBLANES, 3 * D_MLSTM), f2),
                  pl.BlockSpec((2 * SUBLANES, 1), f2),
                  pl.BlockSpec((1, D_MLSTM), f2),
                  pl.BlockSpec((1, D_MLSTM), f2),
                  pl.BlockSpec((CHUNK, CHUNK), f2)],
        out_specs=blk,
        out_shape=jax.ShapeDtypeStruct((bsz, seq, D_MLSTM), BF16),
        scratch_shapes=[pltpu.VMEM((MLSTM_BATCH, ML_HEADS, 2 * ML_HD, ML_HD), F32),
                        pltpu.VMEM((MLSTM_BATCH, ML_HEADS, SUBLANES, LANES), F32)],
        compiler_params=_cparams(("parallel", "arbitrary")),
        name="mlstm",
    )(c, *([vmt] * MLSTM_BATCH), z, wqt, wk, wgt, bgt, ng, sk, triu)
    return y.reshape(bsz * seq, D_MLSTM)


def _attn_kernel(lq_ref, qt_ref, k_ref, vt_ref, bias_ref, g_ref, o_ref,
                 acc_ref, m_ref, q2_ref, s_ref, p_ref, a_ref):
    tq = TQ
    i = pl.program_id(2)
    qt = qt_ref[...]
    row = lax.broadcasted_iota(jnp.int32, (LANES, tq), 0)
    zero = jnp.zeros_like(qt)
    q2_ref[:, 0:tq] = jnp.where(row < DA_QD, qt, zero)
    q2_ref[:, tq:2 * tq] = jnp.where(row >= DA_QD, qt, zero)
    acc_ref[...] = jnp.zeros(acc_ref.shape, F32)
    m_ref[...] = jnp.full(m_ref.shape, NEG, F32)

    p_ref[1] = jnp.zeros(p_ref.shape[1:], BF16)
    a_ref[1] = jnp.ones(a_ref.shape[1:], F32)

    def logits(kb, buf):
        start = pl.multiple_of(kb * tq, tq)
        s_ref[buf] = jnp.dot(k_ref[pl.ds(start, tq), :], q2_ref[...], preferred_element_type=F32)

    ones_rows = jnp.ones((ACC_ROWS - DA_VD, tq), BF16)

    def values(kb, buf):
        start = pl.multiple_of(jnp.maximum(kb, 0) * tq, tq)
        lhs = jnp.concatenate([vt_ref[:, pl.ds(start, tq)], ones_rows], axis=0)
        pv = jnp.dot(lhs, p_ref[buf], preferred_element_type=F32)
        acc_ref[...] = a_ref[buf] * acc_ref[...] + pv

    def softmax(buf, which_bias):
        if which_bias == 1:
            ks = slice(tq - REL_MAX_DIST, tq)
            corner = bias_ref[1, ks, 0:REL_MAX_DIST]
            for c0 in (0, tq):
                s_ref[buf, ks, c0:c0 + REL_MAX_DIST] = s_ref[buf, ks, c0:c0 + REL_MAX_DIST] + corner
        s = s_ref[buf]
        if which_bias == 0:
            bias = bias_ref[0]
            s = s + jnp.concatenate([bias, bias], axis=1)
        m_old = m_ref[...]
        m_new = jnp.maximum(m_old, jnp.max(s, axis=0, keepdims=True))
        m_ref[...] = m_new
        a_ref[buf] = jnp.exp2(m_old - m_new)
        p_ref[buf] = jnp.exp2(s - m_new).astype(BF16)

    def stage(kb, buf, has_next, which_bias):
        softmax(buf, which_bias)
        if has_next:
            logits(kb + 1, 1 - buf)
        values(kb - 1, 1 - buf)

    logits(0, 0)
    n_far = jnp.maximum(i - 1, 0)

    def far_pair(t, carry):
        stage(2 * t, 0, True, None)
        stage(2 * t + 1, 1, True, None)
        return carry

    lax.fori_loop(0, n_far // 2, far_pair, 0)
    kb0 = 2 * (n_far // 2)

    @pl.when(i == 0)
    def _():
        stage(0, 0, False, 0)
        values(0, 0)

    @pl.when(jnp.logical_and(i >= 1, n_far % 2 == 0))
    def _():
        stage(kb0, 0, True, 1)
        stage(kb0 + 1, 1, False, 0)
        values(kb0 + 1, 1)

    @pl.when(n_far % 2 == 1)
    def _():
        stage(kb0, 0, True, None)
        stage(kb0 + 1, 1, True, 1)
        stage(kb0 + 2, 0, False, 0)
        values(kb0 + 2, 0)

    o = acc_ref[0:DA_VD, :] * (1.0 / acc_ref[DA_VD:DA_VD + 1, :])
    lq = lq_ref[...]
    d01 = jnp.sum(lq[0:1, :] * lq[1:2, :], axis=-1, keepdims=True)
    d23 = jnp.sum(lq[2:3, :] * lq[3:4, :], axis=-1, keepdims=True)
    lam = jnp.exp(d01) - jnp.exp(d23) + LAM_INIT
    y = o[:, :tq] - lam * o[:, tq:]
    y = y * lax.rsqrt(jnp.mean(y * y, axis=0, keepdims=True) + SUBLN_EPS) * g_ref[...]
    o_ref[...] = (y * (1.0 - LAM_INIT)).T.astype(o_ref.dtype)


def _rel_bias_tiles(rel_bias):
    L = TQ
    max_exact = REL_BUCKETS // 2
    n = jnp.arange(-(L - 1), 2 * L, dtype=jnp.int32)
    nn = jnp.maximum(n, 0)
    large = max_exact + (jnp.log(jnp.maximum(nn, 1).astype(F32) / max_exact)
                         / math.log(REL_MAX_DIST / max_exact) * (REL_BUCKETS - max_exact)).astype(jnp.int32)
    large = jnp.minimum(large, REL_BUCKETS - 1)
    bucket = jnp.where(nn < max_exact, nn, large)
    rb = (rel_bias.astype(F32) - rel_bias[REL_BUCKETS - 1].astype(F32)[None, :]).T
    t = jnp.where(n[None, :] >= 0, rb[:, bucket] * LOG2E, NEG)

    def toeplitz(v):
        vp = jnp.pad(v, ((0, 0), (0, 1)))
        flat = jnp.tile(vp, (1, L))[:, :L * (2 * L - 1)]
        return flat.reshape(v.shape[0], L, 2 * L - 1)[:, :, L - 1:]

    return jnp.stack([toeplitz(t[:, :2 * L - 1]), toeplitz(t[:, L:])], axis=1)


def _attn(lq, qt, kd, vt, bias, g, bsz, seq):
    nq = seq // TQ
    return pl.pallas_call(
        _attn_kernel,
        grid=(bsz, DA_HEADS, nq),
        in_specs=[pl.BlockSpec((4, DA_QD), lambda b, h, i: (0, 0)),
                  pl.BlockSpec((LANES, TQ), lambda b, h, i: (h, b * nq + i)),
                  pl.BlockSpec((seq, LANES), lambda b, h, i: (b, h)),
                  pl.BlockSpec((LANES, seq), lambda b, h, i: (h, b)),
                  pl.BlockSpec((None, 2, TQ, TQ), lambda b, h, i: (h, 0, 0, 0)),
                  pl.BlockSpec((DA_VD, 1), lambda b, h, i: (0, 0))],
        out_specs=pl.BlockSpec((TQ, LANES), lambda b, h, i: (b * nq + i, h)),
        out_shape=jax.ShapeDtypeStruct((bsz * seq, D_DIFF), BF16),
        scratch_shapes=[pltpu.VMEM((ACC_ROWS, 2 * TQ), F32),
                        pltpu.VMEM((1, 2 * TQ), F32),
                        pltpu.VMEM((LANES, 2 * TQ), BF16),
                        pltpu.VMEM((2, TQ, 2 * TQ), F32),
                        pltpu.VMEM((2, TQ, 2 * TQ), BF16),
                        pltpu.VMEM((2, 1, 2 * TQ), F32)],
        compiler_params=_cparams(("parallel", "parallel", "arbitrary")),
        name="attn",
    )(lq, qt, kd, vt, bias, g)


def _outproj_kernel(x_ref, ym_ref, yd_ref, wo_ref, g2_ref, wr_ref, br_ref, tri_ref,
                    x1_ref, h2_ref, meta_ref, gates_ref, cnt_ref, run_ref):
    step = pl.program_id(0)

    @pl.when(step == 0)
    def _():
        run_ref[...] = jnp.zeros(run_ref.shape, F32)

    tm = TM_PROJ // OUTPROJ_PARTS
    parts = [slice(a * tm, (a + 1) * tm) for a in range(OUTPROJ_PARTS)]
    x1s = [_outproj_residual(rs, x_ref, ym_ref, yd_ref, wo_ref, x1_ref) for rs in parts]
    lts = [_outproj_logits(rs, x1, g2_ref, wr_ref, br_ref, h2_ref) for rs, x1 in zip(parts, x1s)]
    for rs, lt in zip(parts, lts):
        _outproj_route(rs, lt, tri_ref, meta_ref, gates_ref, cnt_ref, run_ref)


def _outproj_residual(rs, x_ref, ym_ref, yd_ref, wo_ref, x1_ref):
    x1 = (x_ref[rs, :]
          + jnp.dot(ym_ref[rs, :], wo_ref[0:D_MLSTM, :], preferred_element_type=F32)
          + jnp.dot(yd_ref[rs, :], wo_ref[D_MLSTM:, :], preferred_element_type=F32))
    x1_ref[rs, :] = x1
    return x1


def _outproj_logits(rs, x1, g2_ref, wr_ref, br_ref, h2_ref):
    ms = jnp.mean(x1 * x1, axis=-1, keepdims=True)
    h2 = x1 * lax.rsqrt(ms + EPS) * g2_ref[...]
    h2_ref[rs] = _to_token_major(h2)
    h_hi = h2.astype(BF16)
    h_lo = (h2 - h_hi.astype(F32)).astype(BF16)
    lhs = jnp.concatenate([h_hi, h_lo, h_hi], axis=1)
    logits = jnp.dot(lhs, wr_ref[...], preferred_element_type=F32) + br_ref[...]
    return logits.T


def _outproj_route(rs, lt, tri_ref, meta_ref, gates_ref, cnt_ref, run_ref):
    tm = lt.shape[1]
    row8 = lax.broadcasted_iota(jnp.int32, (SUBLANES, tm), 0)
    lg = jnp.where(row8 < N_GROUPS, lt[0:SUBLANES], NEG)
    gmax = jnp.max(lg, axis=0, keepdims=True)
    gidx = jnp.min(jnp.where(lg == gmax, row8, SUBLANES), axis=0, keepdims=True)
    pg_sel = 1.0 / jnp.sum(jnp.exp(lg - gmax), axis=0, keepdims=True)
    le = jnp.zeros((EXP_PER_GROUP, tm), F32)
    for g in range(N_GROUPS):
        le = jnp.where(gidx == g, lt[SUBLANES * (g + 1):SUBLANES * (g + 2)], le)
    v1 = jnp.max(le, axis=0, keepdims=True)
    i1 = jnp.min(jnp.where(le == v1, row8, SUBLANES), axis=0, keepdims=True)
    le2 = jnp.where(row8 == i1, NEG, le)
    v2 = jnp.max(le2, axis=0, keepdims=True)
    i2 = jnp.min(jnp.where(le2 == v2, row8, SUBLANES), axis=0, keepdims=True)
    e21 = jnp.exp(v2 - v1)
    pw1 = 1.0 / (1.0 + e21)
    gate1 = pg_sel * pw1
    gate2 = pg_sel * (e21 * pw1)
    eid1 = gidx * EXP_PER_GROUP + i1
    eid2 = gidx * EXP_PER_GROUP + i2

    row32 = lax.broadcasted_iota(jnp.int32, (N_EXPERTS, tm), 0)
    oh1 = row32 == eid1
    oh2 = row32 == eid2
    oh = jnp.where(oh1, 1.0, 0.0) + jnp.where(oh2, 1.0, 0.0)
    before = jnp.dot(oh.astype(BF16), tri_ref[...], preferred_element_type=F32) + run_ref[:, 0:1]
    rank1 = jnp.sum(jnp.where(oh1, before, 0.0), axis=0, keepdims=True).astype(jnp.int32)
    rank2 = jnp.sum(jnp.where(oh2, before, 0.0), axis=0, keepdims=True).astype(jnp.int32)
    run = run_ref[...] + jnp.sum(oh, axis=1, keepdims=True)
    run_ref[...] = run
    cnt_ref[...] = run.astype(jnp.int32)

    meta_ref[:, rs] = jnp.where(row8 == 0, eid1, jnp.where(row8 == 1, eid2,
                                jnp.where(row8 == 2, rank1, jnp.where(row8 == 3, rank2, 0))))
    row128 = lax.broadcasted_iota(jnp.int32, (LANES, tm), 0)
    gt = jnp.where(row128 == 0, gate1, jnp.where(row128 == 1, gate2, 0.0))
    gates_ref[rs, :] = gt.T


def _outproj(x2, ym, yd, wo, g2, wr, br):
    n = x2.shape[0]
    row = lambda i: (i, 0)
    fixed = lambda i: (0, 0)
    part = TM_PROJ // OUTPROJ_PARTS
    tri = jnp.triu(jnp.ones((part, part), F32), k=1).astype(BF16)
    return pl.pallas_call(
        _outproj_kernel,
        grid=(n // TM_PROJ,),
        in_specs=[pl.BlockSpec((TM_PROJ, D_MODEL), row),
                  pl.BlockSpec((TM_PROJ, D_MLSTM), row),
                  pl.BlockSpec((TM_PROJ, D_DIFF), row),
                  pl.BlockSpec((D_MODEL, D_MODEL), fixed),
                  pl.BlockSpec((1, D_MODEL), fixed),
                  pl.BlockSpec((3 * D_MODEL, LANES), fixed),
                  pl.BlockSpec((1, LANES), fixed),
                  pl.BlockSpec((part, part), fixed)],
        out_specs=[pl.BlockSpec((TM_PROJ, D_MODEL), row),
                   pl.BlockSpec((TM_PROJ,) + ROW_TILE, lambda i: (i, 0, 0)),
                   pl.BlockSpec((SUBLANES, TM_PROJ), lambda i: (0, i)),
                   pl.BlockSpec((TM_PROJ, LANES), row),
                   pl.BlockSpec((N_EXPERTS, LANES), fixed)],
        out_shape=[jax.ShapeDtypeStruct((n, D_MODEL), F32),
                   jax.ShapeDtypeStruct((n,) + ROW_TILE, F32),
                   jax.ShapeDtypeStruct((SUBLANES, n), jnp.int32),
                   jax.ShapeDtypeStruct((n, LANES), F32),
                   jax.ShapeDtypeStruct((N_EXPERTS, LANES), jnp.int32)],
        scratch_shapes=[pltpu.VMEM((N_EXPERTS, LANES), F32)],
        compiler_params=_cparams(("arbitrary",)),
        name="outproj_router",
    )(x2, ym, yd, wo, g2, wr, br, tri)


def _row_copy(src_ref, src_row, dst_ref, dst_row, sem):
    return pltpu.make_async_copy(src_ref.at[pl.ds(src_row, 1)], dst_ref.at[pl.ds(dst_row, 1)], sem)


def _dispatch_kernel(ends_ref, pos_ref, h_ref, xs_ref, zero_ref, hbuf_ref, sem, bsem, zsem):
    step = pl.program_id(0)

    @pl.when(step == 0)
    def _():
        zero_ref[...] = jnp.zeros(zero_ref.shape, F32)

        def tile_copy(e):
            start = pl.multiple_of(ends_ref[e + 1] - TM_MOE, TM_MOE)
            return pltpu.make_async_copy(zero_ref, xs_ref.at[pl.ds(start, TM_MOE)], zsem)

        def zstart(e, carry):
            @pl.when(ends_ref[e + 1] > ends_ref[e])
            def _():
                tile_copy(e).start()
            return carry

        def zwait(e, carry):
            @pl.when(ends_ref[e + 1] > ends_ref[e])
            def _():
                tile_copy(e).wait()
            return carry

        lax.fori_loop(0, N_EXPERTS, zstart, 0)
        lax.fori_loop(0, N_EXPERTS, zwait, 0)

        def tail_copy(t):
            return pltpu.make_async_copy(zero_ref, xs_ref.at[pl.ds(pl.multiple_of(t * TM_MOE, TM_MOE), TM_MOE)], zsem)

        def tstart(t, carry):
            tail_copy(t).start()
            return carry

        def twait(t, carry):
            tail_copy(t).wait()
            return carry

        first_tail = ends_ref[N_EXPERTS] // TM_MOE
        lax.fori_loop(first_tail, xs_ref.shape[0] // TM_MOE, tstart, 0)
        lax.fori_loop(first_tail, xs_ref.shape[0] // TM_MOE, twait, 0)

    last = pl.num_programs(0) - 1
    slot = step % 3
    par = step % 2

    def block_copy(i, s):
        return pltpu.make_async_copy(h_ref.at[pl.ds(pl.multiple_of(i * TT, TT), TT)], hbuf_ref.at[s], bsem.at[s])

    @pl.when(step == 0)
    def _():
        block_copy(0, 0).start()

    @pl.when(step < last)
    def _():
        block_copy(step + 1, (step + 1) % 3).start()

    block_copy(step, slot).wait()
    src = hbuf_ref.at[slot]

    def issue(r, carry):
        _row_copy(src, r, xs_ref, pos_ref[0, r], sem.at[par]).start(priority=0)
        _row_copy(src, r, xs_ref, pos_ref[0, TT + r], sem.at[par]).start(priority=1)
        return carry

    def drain(s):
        for _ in range(2):
            pltpu.make_async_copy(src, xs_ref.at[pl.ds(0, TT)], sem.at[s]).wait()

    lax.fori_loop(0, TT, issue, 0)

    @pl.when(step > 0)
    def _():
        drain(1 - par)

    @pl.when(step == last)
    def _():
        drain(par)


def _dispatch(ends, pos, h2, n_rows):
    n = h2.shape[0]
    return pl.pallas_call(
        _dispatch_kernel,
        grid_spec=pltpu.PrefetchScalarGridSpec(
            num_scalar_prefetch=1,
            grid=(n // TT,),
            in_specs=[pl.BlockSpec((None, 1, 2 * TT), lambda i, ends: (i, 0, 0), memory_space=pltpu.SMEM),
                      pl.BlockSpec(memory_space=pl.ANY)],
            out_specs=pl.BlockSpec(memory_space=pl.ANY),
            scratch_shapes=[pltpu.VMEM((TM_MOE,) + ROW_TILE, F32),
                            pltpu.VMEM((3, TT) + ROW_TILE, F32),
                            pltpu.SemaphoreType.DMA((2,)),
                            pltpu.SemaphoreType.DMA((3,)),
                            pltpu.SemaphoreType.DMA(())]),
        out_shape=jax.ShapeDtypeStruct((n_rows,) + ROW_TILE, F32),
        compiler_params=_cparams(("arbitrary",)),
        name="dispatch",
    )(ends, pos, h2)


def _moe_kernel(texp_ref, nact_ref, x_ref, wg32_ref, wu32_ref, wd32_ref, y_ref, wg_ref, wu_ref, wd_ref):
    t = pl.program_id(0)

    @pl.when(jnp.logical_or(t == 0, texp_ref[t] != texp_ref[jnp.maximum(t - 1, 0)]))
    def _():
        wg_ref[...] = wg32_ref[...].astype(BF16)
        wu_ref[...] = wu32_ref[...].astype(BF16)
        wd_ref[...] = wd32_ref[...].astype(BF16)

    @pl.when(t < nact_ref[0])
    def _():
        half = TM_MOE // MOE_SUBTILES
        hids = []
        for a in range(MOE_SUBTILES):
            x = _from_token_major(x_ref[a * half:(a + 1) * half]).astype(BF16)
            g = jnp.dot(x, wg_ref[...], preferred_element_type=F32)
            u = jnp.dot(x, wu_ref[...], preferred_element_type=F32)
            hids.append((g * jax.nn.sigmoid(g) * u).astype(BF16))
        for a in range(MOE_SUBTILES):
            y = jnp.dot(hids[a], wd_ref[...], preferred_element_type=F32)
            y_ref[a * half:(a + 1) * half] = _to_token_major(y)

    @pl.when(t >= nact_ref[0])
    def _():
        y_ref[...] = jnp.zeros(y_ref.shape, F32)


def _moe(tile_expert, n_active, xs, w_eg, w_eu, w_ed):
    n_tiles = xs.shape[0] // TM_MOE
    tile = lambda t, te, na: (jnp.minimum(t, na[0] - 1), 0, 0)
    out_tile = lambda t, te, na: (t, 0, 0)
    wsel = lambda t, te, na: (te[t], 0, 0)
    return pl.pallas_call(
        _moe_kernel,
        grid_spec=pltpu.PrefetchScalarGridSpec(
            num_scalar_prefetch=2,
            grid=(n_tiles,),
            in_specs=[pl.BlockSpec((TM_MOE,) + ROW_TILE, tile),
                      pl.BlockSpec((None, D_MODEL, D_FF_EXP), wsel),
                      pl.BlockSpec((None, D_MODEL, D_FF_EXP), wsel),
                      pl.BlockSpec((None, D_FF_EXP, D_MODEL), wsel)],
            out_specs=pl.BlockSpec((TM_MOE,) + ROW_TILE, out_tile),
            scratch_shapes=[pltpu.VMEM((D_MODEL, D_FF_EXP), BF16),
                            pltpu.VMEM((D_MODEL, D_FF_EXP), BF16),
                            pltpu.VMEM((D_FF_EXP, D_MODEL), BF16)]),
        out_shape=jax.ShapeDtypeStruct(xs.shape, F32),
        compiler_params=_cparams(("arbitrary",)),
        name="moe",
    )(tile_expert, n_active, xs, w_eg, w_eu, w_ed)


def _combine_kernel(pos_ref, posn_ref, x1_ref, gates_ref, gf_ref, ys_ref, o_ref, buf_ref, sem):
    step = pl.program_id(0)
    slot = step % 2

    def gather(p_ref, s):
        def body(r, carry):
            _row_copy(ys_ref, p_ref[0, r], buf_ref.at[s, 0], r, sem.at[s]).start(priority=0)
            _row_copy(ys_ref, p_ref[0, TT + r], buf_ref.at[s, 1], r, sem.at[s]).start(priority=1)
            return carry
        lax.fori_loop(0, TT, body, 0)

    @pl.when(step == 0)
    def _():
        gather(pos_ref, 0)

    @pl.when(step < pl.num_programs(0) - 1)
    def _():
        gather(posn_ref, 1 - slot)

    for k in range(2):
        pltpu.make_async_copy(ys_ref.at[pl.ds(0, TT)], buf_ref.at[slot, k], sem.at[slot]).wait()
    gates = gates_ref[...]
    y = (x1_ref[...] + gates[:, 0:1] * _from_token_major(buf_ref[slot, 0])
         + gates[:, 1:2] * _from_token_major(buf_ref[slot, 1]))
    ms = jnp.mean(y * y, axis=-1, keepdims=True)
    o_ref[...] = y * lax.rsqrt(ms + EPS) * gf_ref[...]


def _combine(pos, x1, gates, gf, ys):
    n = x1.shape[0]
    nblk = n // TT
    return pl.pallas_call(
        _combine_kernel,
        grid=(nblk,),
        in_specs=[pl.BlockSpec((None, 1, 2 * TT), lambda i: (i, 0, 0), memory_space=pltpu.SMEM),
                  pl.BlockSpec((None, 1, 2 * TT), lambda i: (jnp.minimum(i + 1, nblk - 1), 0, 0),
                               memory_space=pltpu.SMEM),
                  pl.BlockSpec((TT, D_MODEL), lambda i: (i, 0)),
                  pl.BlockSpec((TT, LANES), lambda i: (i, 0)),
                  pl.BlockSpec((1, D_MODEL), lambda i: (0, 0)),
                  pl.BlockSpec(memory_space=pl.ANY)],
        out_specs=pl.BlockSpec((TT, D_MODEL), lambda i: (i, 0)),
        out_shape=jax.ShapeDtypeStruct((n, D_MODEL), F32),
        scratch_shapes=[pltpu.VMEM((2, 2, TT) + ROW_TILE, F32),
                        pltpu.SemaphoreType.DMA((2,))],
        compiler_params=_cparams(("arbitrary",)),
        name="combine",
    )(pos, pos, x1, gates, gf, ys)


def kernel(x, w_in, conv_w, conv_b, w_mq, w_mk, w_mgate, b_mgate, m_norm_g, m_skip, lambda_qk, da_norm_g,
           rel_bias, w_out, norm1_g, norm2_g, w_rg, b_rg, w_re, b_re, w_eg, w_eu, w_ed, normf_g):
    bsz, seq, _ = x.shape
    n = bsz * seq
    assert seq % TQ == 0 and seq % CHUNK == 0 and n % TM_PROJ == 0 and n % TT == 0 and bsz % MLSTM_BATCH == 0
    l = 0
    x2 = x.reshape(n, D_MODEL)

    w_in_b = w_in[l].astype(BF16)
    wqv_t = jnp.stack([w_in_b[:, 3 * 512:4 * 512].T, w_in_b[:, 5 * 512:6 * 512].T,
                       w_in_b[:, 1 * 512:2 * 512].T])
    c, vmt, z, qt, kd, vt = _inproj(x2, norm1_g[l][None, :], w_in_b, wqv_t, conv_w[l], conv_b[l][None, :], seq)

    wgt = jnp.zeros((2 * SUBLANES, 3 * D_MLSTM), F32)
    wgt = wgt.at[0:ML_HEADS].set(w_mgate[l][:, :ML_HEADS].T)
    wgt = wgt.at[SUBLANES:SUBLANES + ML_HEADS].set(w_mgate[l][:, ML_HEADS:].T).astype(BF16)
    bgt = jnp.zeros((2 * SUBLANES, 1), F32)
    bgt = bgt.at[0:ML_HEADS, 0].set(b_mgate[l][:ML_HEADS])
    bgt = bgt.at[SUBLANES:SUBLANES + ML_HEADS, 0].set(b_mgate[l][ML_HEADS:])
    y_m = _mlstm(c, vmt, z, jnp.swapaxes(w_mq[l], 1, 2).astype(BF16),
                 w_mk[l].astype(BF16), wgt, bgt, m_norm_g[l][None, :], m_skip[l][None, :], bsz, seq)

    y_d = _attn(lambda_qk[l].astype(F32), qt, kd, vt, _rel_bias_tiles(rel_bias),
                da_norm_g[l][:, None], bsz, seq)

    wr = jnp.zeros((D_MODEL, LANES), F32)
    wr = wr.at[:, :N_GROUPS].set(w_rg[l])
    wr = wr.at[:, SUBLANES:SUBLANES + N_EXPERTS].set(w_re[l].reshape(D_MODEL, N_EXPERTS))
    wr_hi = wr.astype(BF16)
    wr_lo = (wr - wr_hi.astype(F32)).astype(BF16)
    wr3 = jnp.concatenate([wr_hi, wr_hi, wr_lo], axis=0)
    br = jnp.zeros((1, LANES), F32)
    br = br.at[0, :N_GROUPS].set(b_rg[l])
    br = br.at[0, SUBLANES:SUBLANES + N_EXPERTS].set(b_re[l].reshape(N_EXPERTS))
    x1, h2, meta, gates, counts = _outproj(x2, y_m, y_d, w_out[l].astype(BF16), norm2_g[l][None, :], wr3, br)

    cnt = counts[:, 0]
    tiles_e = (cnt + TM_MOE - 1) // TM_MOE
    ends_t = jnp.cumsum(tiles_e)
    ends = jnp.concatenate([jnp.zeros((1,), jnp.int32), ends_t * TM_MOE]).astype(jnp.int32)
    offs = ends[:-1]
    n_tiles = (2 * n) // TM_MOE + N_EXPERTS
    n_active = ends_t[-1:].astype(jnp.int32)
    tile_ids = jnp.minimum(jnp.arange(n_tiles, dtype=jnp.int32), n_active[0] - 1)
    tile_expert = jnp.sum(tile_ids[:, None] >= ends_t[None, :], axis=1).astype(jnp.int32)
    eids = jnp.arange(N_EXPERTS, dtype=jnp.int32)[:, None]
    pos1 = jnp.sum(jnp.where(meta[0][None, :] == eids, offs[:, None], 0), axis=0) + meta[2]
    pos2 = jnp.sum(jnp.where(meta[1][None, :] == eids, offs[:, None], 0), axis=0) + meta[3]
    pos = jnp.concatenate([pos1.reshape(n // TT, 1, TT), pos2.reshape(n // TT, 1, TT)], axis=2)

    xs = _dispatch(ends, pos, h2, n_tiles * TM_MOE)
    ys = _moe(tile_expert, n_active, xs, w_eg[l], w_eu[l], w_ed[l])
    out = _combine(pos, x1, gates, normf_g[None, :], ys)
    return out.reshape(bsz, seq, D_MODEL)
```

```python
import functools
import math

import jax
import jax.numpy as jnp
from jax import lax
from jax.experimental import pallas as pl
from jax.experimental.pallas import tpu as pltpu

F32 = jnp.float32
BF16 = jnp.bfloat16

D_MODEL = 1024
D_MLSTM = 512
D_DIFF = 512
ML_HEADS = 4
ML_HD = 128
CONV_W = 4
CHUNK = 128
DA_HEADS = 4
DA_VD = 128
DA_QD = 64
REL_BUCKETS = 32
REL_MAX_DIST = 128
N_GROUPS = 4
EXP_PER_GROUP = 8
N_EXPERTS = 32
D_FF_EXP = 512
EPS = 1e-6
SUBLN_EPS = 1e-5
LAM_INIT = 0.8 - 0.6 * math.exp(-0.3 * 0)
NEG = -1e30
LOG2E = math.log2(math.e)

LANES = 128
SUBLANES = 8
MXU_COLS = 256

TM_PROJ = 512
INPROJ_PARTS = 2
OUTPROJ_PARTS = 2
TQ = 512
ACC_ROWS = DA_VD + 16
MLSTM_BATCH = 4
TM_MOE = 512
MOE_SUBTILES = 2
TT = 512
VMEM_LIMIT = 48 * 1024 * 1024


ROW_TILE = (D_MODEL // LANES, LANES)


def _cparams(sem):
    return pltpu.CompilerParams(dimension_semantics=sem, vmem_limit_bytes=VMEM_LIMIT)


def _to_token_major(x):
    return pltpu.einshape("a(bc)->abc", x, b=ROW_TILE[0])


def _from_token_major(x):
    return pltpu.einshape("abc->a(bc)", x)


def _inproj_kernel(tiles_per_seq, x_ref, g_ref, w_ref, wqv_ref, cw_ref, cb_ref,
                   c_ref, vm_ref, z_ref, q_ref, k_ref, v_ref, ext_ref):
    @pl.when(pl.program_id(0) % tiles_per_seq == 0)
    def _():
        ext_ref[0:SUBLANES, :] = jnp.zeros((SUBLANES, D_MLSTM), F32)

    tm = TM_PROJ // INPROJ_PARTS
    nt = (((1,), (1,)), ((), ()))
    cw = cw_ref[...]

    def normed(a):
        x = x_ref[a * tm:(a + 1) * tm, :]
        ms = jnp.mean(x * x, axis=-1, keepdims=True)
        return (x * lax.rsqrt(ms + EPS) * g_ref[...]).astype(BF16)

    def conv_rows(a):
        r0 = SUBLANES + a * tm
        conv = cb_ref[...] + cw[CONV_W - 1:CONV_W, :] * ext_ref[r0:r0 + tm, :]
        for k in range(1, CONV_W):
            conv = conv + cw[CONV_W - 1 - k:CONV_W - k, :] * ext_ref[r0 - k:r0 - k + tm, :]
        c_ref[a * tm:(a + 1) * tm, :] = conv * jax.nn.sigmoid(conv)

    hs = []
    for a in range(INPROJ_PARTS):
        hs.append(normed(a))
        ext_ref[SUBLANES + a * tm:SUBLANES + (a + 1) * tm, :] = jnp.dot(hs[a], w_ref[:, 0:512],
                                                                          preferred_element_type=F32)
    for a in range(INPROJ_PARTS):
        h = hs[a]
        rs = slice(a * tm, (a + 1) * tm)
        z_ref[rs, :] = jnp.dot(h, w_ref[:, 2 * 512:3 * 512], preferred_element_type=F32)
        k_ref[rs, :] = jnp.dot(h, w_ref[:, 4 * 512:5 * 512], preferred_element_type=F32).astype(k_ref.dtype)
        qt = lax.dot_general(wqv_ref[0], h, nt, preferred_element_type=F32)
        q_ref[:, rs] = (qt * (DA_QD ** -0.5 * LOG2E)).astype(q_ref.dtype)
        v_ref[:, rs] = lax.dot_general(wqv_ref[1], h, nt, preferred_element_type=F32).astype(v_ref.dtype)
        vm_ref[:, rs] = lax.dot_general(wqv_ref[2], h, nt, preferred_element_type=F32).astype(vm_ref.dtype)
        conv_rows(a)
    ext_ref[0:SUBLANES, :] = ext_ref[TM_PROJ:TM_PROJ + SUBLANES, :]


def _inproj(x2, g1, w_in, wqv_t, conv_w, conv_b, seq):
    n = x2.shape[0]
    assert seq % TM_PROJ == 0
    row = lambda i: (i, 0)
    fixed = lambda i: (0, 0)
    sds = lambda dt: jax.ShapeDtypeStruct((n, 512), dt)
    sds_t = jax.ShapeDtypeStruct((512, n), BF16)
    blk = pl.BlockSpec((TM_PROJ, 512), row)
    blk_t = pl.BlockSpec((512, TM_PROJ), lambda i: (0, i))
    return pl.pallas_call(
        functools.partial(_inproj_kernel, seq // TM_PROJ),
        grid=(n // TM_PROJ,),
        in_specs=[pl.BlockSpec((TM_PROJ, D_MODEL), row),
                  pl.BlockSpec((1, D_MODEL), fixed),
                  pl.BlockSpec((D_MODEL, 6 * 512), fixed),
                  pl.BlockSpec((3, 512, D_MODEL), lambda i: (0, 0, 0)),
                  pl.BlockSpec((CONV_W, D_MLSTM), fixed),
                  pl.BlockSpec((1, D_MLSTM), fixed)],
        out_specs=[blk, blk_t, blk, blk_t, blk, blk_t],
        out_shape=[sds(F32), sds_t, sds(F32), sds_t, sds(BF16), sds_t],
        scratch_shapes=[pltpu.VMEM((SUBLANES + TM_PROJ, D_MLSTM), F32)],
        compiler_params=_cparams(("arbitrary",)),
        name="inproj",
    )(x2, g1, w_in, wqv_t, conv_w, conv_b)


def _mlstm_kernel(*refs):
    c_ref = refs[0]
    vt_refs = refs[1:1 + MLSTM_BATCH]
    (z_ref, wq_ref, wk_ref, wg_ref, bg_ref,
     ng_ref, sk_ref, tril_ref, y_ref, st_ref, m_ref) = refs[1 + MLSTM_BATCH:]
    j = pl.program_id(1)

    @pl.when(j == 0)
    def _():
        st_ref[...] = jnp.zeros(st_ref.shape, F32)
        m_ref[...] = jnp.zeros(m_ref.shape, F32)

    seqs = range(MLSTM_BATCH)
    proj = [_mlstm_qk(c_ref.at[bb], vt_refs[bb], wq_ref, wk_ref) for bb in seqs]
    for bb in seqs:
        _mlstm_gate_logits(proj[bb], wg_ref, bg_ref)
    for bb in seqs:
        _mlstm_cumulate(proj[bb], tril_ref)
    pairs = [(h, bb) for h in range(ML_HEADS) for bb in seqs]
    prods = [_mlstm_head_products(h, proj[bb], st_ref.at[bb]) for h, bb in pairs]
    for (h, bb), hp in zip(pairs, prods):
        _mlstm_head(h, proj[bb], hp, z_ref.at[bb], ng_ref, sk_ref, y_ref.at[bb], st_ref.at[bb], m_ref.at[bb])


_NT = (((1,), (1,)), ((), ()))


def _mlstm_qk(c_ref, vm_ref, wq_ref, wk_ref):
    c_act = c_ref[...]
    vt = vm_ref[...]
    qts, ks = [], []
    for h in range(ML_HEADS):
        ch = c_act[:, h * ML_HD:(h + 1) * ML_HD].astype(BF16)
        qt = lax.dot_general(wq_ref[h], ch, _NT, preferred_element_type=F32)
        qts.append(qt.astype(BF16))
        ks.append(jnp.dot(ch, wk_ref[h], preferred_element_type=F32))
    return dict(c_act=c_act, vt=vt, qts=qts, ks=ks)


def _mlstm_gate_logits(p, wg_ref, bg_ref):
    gates_t = jnp.dot(wg_ref[:, 2 * D_MLSTM:3 * D_MLSTM], p["vt"], preferred_element_type=F32) + bg_ref[...]
    for h in range(ML_HEADS):
        hs = slice(h * ML_HD, (h + 1) * ML_HD)
        gates_t = gates_t + jnp.dot(wg_ref[:, hs], p["qts"][h], preferred_element_type=F32)
        gates_t = gates_t + lax.dot_general(wg_ref[:, D_MLSTM + h * ML_HD:D_MLSTM + (h + 1) * ML_HD],
                                            p["ks"][h].astype(BF16), _NT, preferred_element_type=F32)
    p["gates_t"] = gates_t


def _mlstm_cumulate(p, tril_ref):
    L = CHUNK
    gates_t = p["gates_t"]
    logf_t = jnp.minimum(gates_t, 0.0) - jnp.log(1.0 + jnp.exp(-jnp.abs(gates_t)))
    bcum_t = jnp.dot(logf_t, tril_ref[...], preferred_element_type=F32, precision=lax.Precision.HIGHEST)
    a8 = gates_t[0:SUBLANES] - bcum_t[SUBLANES:2 * SUBLANES]
    p["bcum_t"] = bcum_t
    p["a_cols"] = jnp.concatenate([a8, jnp.zeros((LANES - SUBLANES, L), F32)], axis=0).T


def _mlstm_head_products(h, p, st_ref):
    qt = p["qts"][h]
    k = (p["ks"][h] * (ML_HD ** -0.5)).astype(BF16)
    state_t = st_ref[h]
    s_t = jnp.dot(k, qt, preferred_element_type=F32)
    r_inter = jnp.dot(state_t.astype(BF16), qt, preferred_element_type=F32)
    return dict(k=k, state_t=state_t, s_t=s_t, r_inter=r_inter)


def _mlstm_head(h, p, hp, z_ref, ng_ref, sk_ref, y_ref, st_ref, m_ref):
    L = CHUNK
    c_act, vt, gates_t, bcum_t, a_cols = p["c_act"], p["vt"], p["gates_t"], p["bcum_t"], p["a_cols"]
    k, state_t, s_t, r_inter = hp["k"], hp["state_t"], hp["s_t"], hp["r_inter"]
    s_idx = lax.broadcasted_iota(jnp.int32, (L, L), 0)
    t_idx = lax.broadcasted_iota(jnp.int32, (L, L), 1)
    causal = s_idx <= t_idx
    ones_row = jnp.where(lax.broadcasted_iota(jnp.int32, (ML_HD, L), 0) == 0, 1.0, 0.0).astype(BF16)
    sl = slice(h * ML_HD, (h + 1) * ML_HD)
    vaug_t = jnp.concatenate([vt[sl, :], ones_row], axis=0)
    i_row = gates_t[h:h + 1, :]
    b_row = bcum_t[SUBLANES + h:SUBLANES + h + 1, :]
    a_row = i_row - b_row
    m_prev = m_ref[h][0:1, 0:1]

    e = jnp.where(causal, a_cols[:, h:h + 1], NEG)
    cm = jnp.max(e, axis=0, keepdims=True)
    mm = jnp.maximum(m_prev, cm)
    dexp = jnp.exp(e - mm)
    w_t = (dexp * s_t).astype(BF16)
    sp = jnp.exp(m_prev - mm)
    r_intra = jnp.dot(vaug_t, w_t, preferred_element_type=F32)
    tot = sp * r_inter + r_intra
    num = tot[:ML_HD]
    den = tot[ML_HD:ML_HD + 1]
    h_t = num / jnp.maximum(jnp.abs(den), jnp.exp(-(b_row + mm)))

    b_end = b_row[:, L - 1:L]
    mm_end = mm[:, L - 1:L]
    wk = jnp.exp(a_row - mm_end)
    decay = jnp.exp(m_prev - mm_end)
    vw = (vaug_t.astype(F32) * wk).astype(BF16)
    upd = jnp.dot(vw, k, preferred_element_type=F32)
    st_ref[h] = decay * state_t + upd
    m_ref[h] = jnp.broadcast_to(b_end + mm_end, (SUBLANES, LANES))

    mu = jnp.mean(h_t, axis=0, keepdims=True)
    xc = h_t - mu
    var = jnp.mean(xc * xc, axis=0, keepdims=True)
    hn = (xc * lax.rsqrt(var + EPS)).T * ng_ref[:, sl]
    hn = hn + sk_ref[:, sl] * c_act[:, sl]
    o = jax.nn.sigmoid(z_ref[:, sl])
    y_ref[:, sl] = (o * hn).astype(y_ref.dtype)


def _mlstm(c, vmt, z, wqt, wk, wgt, bgt, ng, sk, bsz, seq):
    nc = seq // CHUNK
    f2 = lambda b, j: (0, 0)
    f3 = lambda b, j: (0, 0, 0)
    triu = jnp.triu(jnp.ones((CHUNK, CHUNK), F32))
    blk = pl.BlockSpec((MLSTM_BATCH, CHUNK, D_MLSTM), lambda b, j: (b, j, 0))
    vt_specs = [pl.BlockSpec((D_MLSTM, CHUNK), functools.partial(lambda b, j, bb: (0, (b * MLSTM_BATCH + bb) * nc + j),
                                                                  bb=bb))
                for bb in range(MLSTM_BATCH)]
    c, z = (a.reshape(bsz, seq, D_MLSTM) for a in (c, z))
    y = pl.pallas_call(
        _mlstm_kernel,
        grid=(bsz // MLSTM_BATCH, nc),
        in_specs=[blk] + vt_specs + [
                  blk,
                  pl.BlockSpec((ML_HEADS, ML_HD, ML_HD), f3),
                  pl.BlockSpec((ML_HEADS, ML_HD, ML_HD), f3),
                  pl.BlockSpec((2 * SUBLANES, 3 * D_MLSTM), f2),
                  pl.BlockSpec((2 * SUBLANES, 1), f2),
                  pl.BlockSpec((1, D_MLSTM), f2),
                  pl.BlockSpec((1, D_MLSTM), f2),
                  pl.BlockSpec((CHUNK, CHUNK), f2)],
        out_specs=blk,
        out_shape=jax.ShapeDtypeStruct((bsz, seq, D_MLSTM), BF16),
        scratch_shapes=[pltpu.VMEM((MLSTM_BATCH, ML_HEADS, 2 * ML_HD, ML_HD), F32),
                        pltpu.VMEM((MLSTM_BATCH, ML_HEADS, SUBLANES, LANES), F32)],
        compiler_params=_cparams(("parallel", "arbitrary")),
        name="mlstm",
    )(c, *([vmt] * MLSTM_BATCH), z, wqt, wk, wgt, bgt, ng, sk, triu)
    return y.reshape(bsz * seq, D_MLSTM)


def _attn_kernel(lq_ref, qt_ref, k_ref, vt_ref, bias_ref, g_ref, o_ref,
                 acc_ref, m_ref, q2_ref, s_ref, p_ref, a_ref):
    tq = TQ
    i = pl.program_id(2)
    qt = qt_ref[...]
    row = lax.broadcasted_iota(jnp.int32, (LANES, tq), 0)
    zero = jnp.zeros_like(qt)
    q2_ref[:, 0:tq] = jnp.where(row < DA_QD, qt, zero)
    q2_ref[:, tq:2 * tq] = jnp.where(row >= DA_QD, qt, zero)
    acc_ref[...] = jnp.zeros(acc_ref.shape, F32)
    m_ref[...] = jnp.full(m_ref.shape, NEG, F32)

    p_ref[1] = jnp.zeros(p_ref.shape[1:], BF16)
    a_ref[1] = jnp.ones(a_ref.shape[1:], F32)

    def logits(kb, buf):
        start = pl.multiple_of(kb * tq, tq)
        s_ref[buf] = jnp.dot(k_ref[pl.ds(start, tq), :], q2_ref[...], preferred_element_type=F32)

    ones_rows = jnp.ones((ACC_ROWS - DA_VD, tq), BF16)

    def values(kb, buf):
        start = pl.multiple_of(jnp.maximum(kb, 0) * tq, tq)
        lhs = jnp.concatenate([vt_ref[:, pl.ds(start, tq)], ones_rows], axis=0)
        pv = jnp.dot(lhs, p_ref[buf], preferred_element_type=F32)
        acc_ref[...] = a_ref[buf] * acc_ref[...] + pv

    def softmax(buf, which_bias):
        if which_bias == 1:
            ks = slice(tq - REL_MAX_DIST, tq)
            corner = bias_ref[1, ks, 0:REL_MAX_DIST]
            for c0 in (0, tq):
                s_ref[buf, ks, c0:c0 + REL_MAX_DIST] = s_ref[buf, ks, c0:c0 + REL_MAX_DIST] + corner
        s = s_ref[buf]
        if which_bias == 0:
            bias = bias_ref[0]
            s = s + jnp.concatenate([bias, bias], axis=1)
        m_old = m_ref[...]
        m_new = jnp.maximum(m_old, jnp.max(s, axis=0, keepdims=True))
        m_ref[...] = m_new
        a_ref[buf] = jnp.exp2(m_old - m_new)
        p_ref[buf] = jnp.exp2(s - m_new).astype(BF16)

    def stage(kb, buf, has_next, which_bias):
        softmax(buf, which_bias)
        if has_next:
            logits(kb + 1, 1 - buf)
        values(kb - 1, 1 - buf)

    logits(0, 0)
    n_far = jnp.maximum(i - 1, 0)

    def far_pair(t, carry):
        stage(2 * t, 0, True, None)
        stage(2 * t + 1, 1, True, None)
        return carry

    lax.fori_loop(0, n_far // 2, far_pair, 0)
    kb0 = 2 * (n_far // 2)

    @pl.when(i == 0)
    def _():
        stage(0, 0, False, 0)
        values(0, 0)

    @pl.when(jnp.logical_and(i >= 1, n_far % 2 == 0))
    def _():
        stage(kb0, 0, True, 1)
        stage(kb0 + 1, 1, False, 0)
        values(kb0 + 1, 1)

    @pl.when(n_far % 2 == 1)
    def _():
        stage(kb0, 0, True, None)
        stage(kb0 + 1, 1, True, 1)
        stage(kb0 + 2, 0, False, 0)
        values(kb0 + 2, 0)

    o = acc_ref[0:DA_VD, :] * (1.0 / acc_ref[DA_VD:DA_VD + 1, :])
    lq = lq_ref[...]
    d01 = jnp.sum(lq[0:1, :] * lq[1:2, :], axis=-1, keepdims=True)
    d23 = jnp.sum(lq[2:3, :] * lq[3:4, :], axis=-1, keepdims=True)
    lam = jnp.exp(d01) - jnp.exp(d23) + LAM_INIT
    y = o[:, :tq] - lam * o[:, tq:]
    y = y * lax.rsqrt(jnp.mean(y * y, axis=0, keepdims=True) + SUBLN_EPS) * g_ref[...]
    o_ref[...] = (y * (1.0 - LAM_INIT)).T.astype(o_ref.dtype)


def _rel_bias_tiles(rel_bias):
    L = TQ
    max_exact = REL_BUCKETS // 2
    n = jnp.arange(-(L - 1), 2 * L, dtype=jnp.int32)
    nn = jnp.maximum(n, 0)
    large = max_exact + (jnp.log(jnp.maximum(nn, 1).astype(F32) / max_exact)
                         / math.log(REL_MAX_DIST / max_exact) * (REL_BUCKETS - max_exact)).astype(jnp.int32)
    large = jnp.minimum(large, REL_BUCKETS - 1)
    bucket = jnp.where(nn < max_exact, nn, large)
    rb = (rel_bias.astype(F32) - rel_bias[REL_BUCKETS - 1].astype(F32)[None, :]).T
    t = jnp.where(n[None, :] >= 0, rb[:, bucket] * LOG2E, NEG)

    def toeplitz(v):
        vp = jnp.pad(v, ((0, 0), (0, 1)))
        flat = jnp.tile(vp, (1, L))[:, :L * (2 * L - 1)]
        return flat.reshape(v.shape[0], L, 2 * L - 1)[:, :, L - 1:]

    return jnp.stack([toeplitz(t[:, :2 * L - 1]), toeplitz(t[:, L:])], axis=1)


def _attn(lq, qt, kd, vt, bias, g, bsz, seq):
    nq = seq // TQ
    return pl.pallas_call(
        _attn_kernel,
        grid=(bsz, DA_HEADS, nq),
        in_specs=[pl.BlockSpec((4, DA_QD), lambda b, h, i: (0, 0)),
                  pl.BlockSpec((LANES, TQ), lambda b, h, i: (h, b * nq + i)),
                  pl.BlockSpec((seq, LANES), lambda b, h, i: (b, h)),
                  pl.BlockSpec((LANES, seq), lambda b, h, i: (h, b)),
                  pl.BlockSpec((None, 2, TQ, TQ), lambda b, h, i: (h, 0, 0, 0)),
                  pl.BlockSpec((DA_VD, 1), lambda b, h, i: (0, 0))],
        out_specs=pl.BlockSpec((TQ, LANES), lambda b, h, i: (b * nq + i, h)),
        out_shape=jax.ShapeDtypeStruct((bsz * seq, D_DIFF), BF16),
        scratch_shapes=[pltpu.VMEM((ACC_ROWS, 2 * TQ), F32),
                        pltpu.VMEM((1, 2 * TQ), F32),
                        pltpu.VMEM((LANES, 2 * TQ), BF16),
                        pltpu.VMEM((2, TQ, 2 * TQ), F32),
                        pltpu.VMEM((2, TQ, 2 * TQ), BF16),
                        pltpu.VMEM((2, 1, 2 * TQ), F32)],
        compiler_params=_cparams(("parallel", "parallel", "arbitrary")),
        name="attn",
    )(lq, qt, kd, vt, bias, g)


def _outproj_kernel(x_ref, ym_ref, yd_ref, wo_ref, g2_ref, wr_ref, br_ref, tri_ref,
                    x1_ref, h2_ref, meta_ref, gates_ref, cnt_ref, run_ref):
    step = pl.program_id(0)

    @pl.when(step == 0)
    def _():
        run_ref[...] = jnp.zeros(run_ref.shape, F32)

    tm = TM_PROJ // OUTPROJ_PARTS
    parts = [slice(a * tm, (a + 1) * tm) for a in range(OUTPROJ_PARTS)]
    x1s = [_outproj_residual(rs, x_ref, ym_ref, yd_ref, wo_ref, x1_ref) for rs in parts]
    lts = [_outproj_logits(rs, x1, g2_ref, wr_ref, br_ref, h2_ref) for rs, x1 in zip(parts, x1s)]
    for rs, lt in zip(parts, lts):
        _outproj_route(rs, lt, tri_ref, meta_ref, gates_ref, cnt_ref, run_ref)


def _outproj_residual(rs, x_ref, ym_ref, yd_ref, wo_ref, x1_ref):
    x1 = (x_ref[rs, :]
          + jnp.dot(ym_ref[rs, :], wo_ref[0:D_MLSTM, :], preferred_element_type=F32)
          + jnp.dot(yd_ref[rs, :], wo_ref[D_MLSTM:, :], preferred_element_type=F32))
    x1_ref[rs, :] = x1
    return x1


def _outproj_logits(rs, x1, g2_ref, wr_ref, br_ref, h2_ref):
    ms = jnp.mean(x1 * x1, axis=-1, keepdims=True)
    h2 = x1 * lax.rsqrt(ms + EPS) * g2_ref[...]
    h2_ref[rs] = _to_token_major(h2)
    h_hi = h2.astype(BF16)
    h_lo = (h2 - h_hi.astype(F32)).astype(BF16)
    lhs = jnp.concatenate([h_hi, h_lo, h_hi], axis=1)
    logits = jnp.dot(lhs, wr_ref[...], preferred_element_type=F32) + br_ref[...]
    return logits.T


def _outproj_route(rs, lt, tri_ref, meta_ref, gates_ref, cnt_ref, run_ref):
    tm = lt.shape[1]
    row8 = lax.broadcasted_iota(jnp.int32, (SUBLANES, tm), 0)
    lg = jnp.where(row8 < N_GROUPS, lt[0:SUBLANES], NEG)
    gmax = jnp.max(lg, axis=0, keepdims=True)
    gidx = jnp.min(jnp.where(lg == gmax, row8, SUBLANES), axis=0, keepdims=True)
    pg_sel = 1.0 / jnp.sum(jnp.exp(lg - gmax), axis=0, keepdims=True)
    le = jnp.zeros((EXP_PER_GROUP, tm), F32)
    for g in range(N_GROUPS):
        le = jnp.where(gidx == g, lt[SUBLANES * (g + 1):SUBLANES * (g + 2)], le)
    v1 = jnp.max(le, axis=0, keepdims=True)
    i1 = jnp.min(jnp.where(le == v1, row8, SUBLANES), axis=0, keepdims=True)
    le2 = jnp.where(row8 == i1, NEG, le)
    v2 = jnp.max(le2, axis=0, keepdims=True)
    i2 = jnp.min(jnp.where(le2 == v2, row8, SUBLANES), axis=0, keepdims=True)
    e21 = jnp.exp(v2 - v1)
    pw1 = 1.0 / (1.0 + e21)
    gate1 = pg_sel * pw1
    gate2 = pg_sel * (e21 * pw1)
    eid1 = gidx * EXP_PER_GROUP + i1
    eid2 = gidx * EXP_PER_GROUP + i2

    row32 = lax.broadcasted_iota(jnp.int32, (N_EXPERTS, tm), 0)
    oh1 = row32 == eid1
    oh2 = row32 == eid2
    oh = jnp.where(oh1, 1.0, 0.0) + jnp.where(oh2, 1.0, 0.0)
    before = jnp.dot(oh.astype(BF16), tri_ref[...], preferred_element_type=F32) + run_ref[:, 0:1]
    rank1 = jnp.sum(jnp.where(oh1, before, 0.0), axis=0, keepdims=True).astype(jnp.int32)
    rank2 = jnp.sum(jnp.where(oh2, before, 0.0), axis=0, keepdims=True).astype(jnp.int32)
    run = run_ref[...] + jnp.sum(oh, axis=1, keepdims=True)
    run_ref[...] = run
    cnt_ref[...] = run.astype(jnp.int32)

    meta_ref[:, rs] = jnp.where(row8 == 0, eid1, jnp.where(row8 == 1, eid2,
                                jnp.where(row8 == 2, rank1, jnp.where(row8 == 3, rank2, 0))))
    row128 = lax.broadcasted_iota(jnp.int32, (LANES, tm), 0)
    gt = jnp.where(row128 == 0, gate1, jnp.where(row128 == 1, gate2, 0.0))
    gates_ref[rs, :] = gt.T


def _outproj(x2, ym, yd, wo, g2, wr, br):
    n = x2.shape[0]
    row = lambda i: (i, 0)
    fixed = lambda i: (0, 0)
    part = TM_PROJ // OUTPROJ_PARTS
    tri = jnp.triu(jnp.ones((part, part), F32), k=1).astype(BF16)
    return pl.pallas_call(
        _outproj_kernel,
        grid=(n // TM_PROJ,),
        in_specs=[pl.BlockSpec((TM_PROJ, D_MODEL), row),
                  pl.BlockSpec((TM_PROJ, D_MLSTM), row),
                  pl.BlockSpec((TM_PROJ, D_DIFF), row),
                  pl.BlockSpec((D_MODEL, D_MODEL), fixed),
                  pl.BlockSpec((1, D_MODEL), fixed),
                  pl.BlockSpec((3 * D_MODEL, LANES), fixed),
                  pl.BlockSpec((1, LANES), fixed),
                  pl.BlockSpec((part, part), fixed)],
        out_specs=[pl.BlockSpec((TM_PROJ, D_MODEL), row),
                   pl.BlockSpec((TM_PROJ,) + ROW_TILE, lambda i: (i, 0, 0)),
                   pl.BlockSpec((SUBLANES, TM_PROJ), lambda i: (0, i)),
                   pl.BlockSpec((TM_PROJ, LANES), row),
                   pl.BlockSpec((N_EXPERTS, LANES), fixed)],
        out_shape=[jax.ShapeDtypeStruct((n, D_MODEL), F32),
                   jax.ShapeDtypeStruct((n,) + ROW_TILE, F32),
                   jax.ShapeDtypeStruct((SUBLANES, n), jnp.int32),
                   jax.ShapeDtypeStruct((n, LANES), F32),
                   jax.ShapeDtypeStruct((N_EXPERTS, LANES), jnp.int32)],
        scratch_shapes=[pltpu.VMEM((N_EXPERTS, LANES), F32)],
        compiler_params=_cparams(("arbitrary",)),
        name="outproj_router",
    )(x2, ym, yd, wo, g2, wr, br, tri)


def _row_copy(src_ref, src_row, dst_ref, dst_row, sem):
    return pltpu.make_async_copy(src_ref.at[pl.ds(src_row, 1)], dst_ref.at[pl.ds(dst_row, 1)], sem)


def _dispatch_kernel(ends_ref, pos_ref, h_ref, xs_ref, zero_ref, hbuf_ref, sem, bsem, zsem):
    step = pl.program_id(0)

    @pl.when(step == 0)
    def _():
        zero_ref[...] = jnp.zeros(zero_ref.shape, F32)

        def tile_copy(e):
            start = pl.multiple_of(ends_ref[e + 1] - TM_MOE, TM_MOE)
            return pltpu.make_async_copy(zero_ref, xs_ref.at[pl.ds(start, TM_MOE)], zsem)

        def zstart(e, carry):
            @pl.when(ends_ref[e + 1] > ends_ref[e])
            def _():
                tile_copy(e).start()
            return carry

        def zwait(e, carry):
            @pl.when(ends_ref[e + 1] > ends_ref[e])
            def _():
                tile_copy(e).wait()
            return carry

        lax.fori_loop(0, N_EXPERTS, zstart, 0)
        lax.fori_loop(0, N_EXPERTS, zwait, 0)

        def tail_copy(t):
            return pltpu.make_async_copy(zero_ref, xs_ref.at[pl.ds(pl.multiple_of(t * TM_MOE, TM_MOE), TM_MOE)], zsem)

        def tstart(t, carry):
            tail_copy(t).start()
            return carry

        def twait(t, carry):
            tail_copy(t).wait()
            return carry

        first_tail = ends_ref[N_EXPERTS] // TM_MOE
        lax.fori_loop(first_tail, xs_ref.shape[0] // TM_MOE, tstart, 0)
        lax.fori_loop(first_tail, xs_ref.shape[0] // TM_MOE, twait, 0)

    last = pl.num_programs(0) - 1
    slot = step % 3
    par = step % 2

    def block_copy(i, s):
        return pltpu.make_async_copy(h_ref.at[pl.ds(pl.multiple_of(i * TT, TT), TT)], hbuf_ref.at[s], bsem.at[s])

    @pl.when(step == 0)
    def _():
        block_copy(0, 0).start()

    @pl.when(step < last)
    def _():
        block_copy(step + 1, (step + 1) % 3).start()

    block_copy(step, slot).wait()
    src = hbuf_ref.at[slot]

    def issue(r, carry):
        _row_copy(src, r, xs_ref, pos_ref[0, r], sem.at[par]).start(priority=0)
        _row_copy(src, r, xs_ref, pos_ref[0, TT + r], sem.at[par]).start(priority=1)
        return carry

    def drain(s):
        for _ in range(2):
            pltpu.make_async_copy(src, xs_ref.at[pl.ds(0, TT)], sem.at[s]).wait()

    lax.fori_loop(0, TT, issue, 0)

    @pl.when(step > 0)
    def _():
        drain(1 - par)

    @pl.when(step == last)
    def _():
        drain(par)


def _dispatch(ends, pos, h2, n_rows):
    n = h2.shape[0]
    return pl.pallas_call(
        _dispatch_kernel,
        grid_spec=pltpu.PrefetchScalarGridSpec(
            num_scalar_prefetch=1,
            grid=(n // TT,),
            in_specs=[pl.BlockSpec((None, 1, 2 * TT), lambda i, ends: (i, 0, 0), memory_space=pltpu.SMEM),
                      pl.BlockSpec(memory_space=pl.ANY)],
            out_specs=pl.BlockSpec(memory_space=pl.ANY),
            scratch_shapes=[pltpu.VMEM((TM_MOE,) + ROW_TILE, F32),
                            pltpu.VMEM((3, TT) + ROW_TILE, F32),
                            pltpu.SemaphoreType.DMA((2,)),
                            pltpu.SemaphoreType.DMA((3,)),
                            pltpu.SemaphoreType.DMA(())]),
        out_shape=jax.ShapeDtypeStruct((n_rows,) + ROW_TILE, F32),
        compiler_params=_cparams(("arbitrary",)),
        name="dispatch",
    )(ends, pos, h2)


def _moe_kernel(texp_ref, nact_ref, x_ref, wg32_ref, wu32_ref, wd32_ref, y_ref, wg_ref, wu_ref, wd_ref):
    t = pl.program_id(0)

    @pl.when(jnp.logical_or(t == 0, texp_ref[t] != texp_ref[jnp.maximum(t - 1, 0)]))
    def _():
        wg_ref[...] = wg32_ref[...].astype(BF16)
        wu_ref[...] = wu32_ref[...].astype(BF16)
        wd_ref[...] = wd32_ref[...].astype(BF16)

    @pl.when(t < nact_ref[0])
    def _():
        half = TM_MOE // MOE_SUBTILES
        hids = []
        for a in range(MOE_SUBTILES):
            x = _from_token_major(x_ref[a * half:(a + 1) * half]).astype(BF16)
            g = jnp.dot(x, wg_ref[...], preferred_element_type=F32)
            u = jnp.dot(x, wu_ref[...], preferred_element_type=F32)
            hids.append((g * jax.nn.sigmoid(g) * u).astype(BF16))
        for a in range(MOE_SUBTILES):
            y = jnp.dot(hids[a], wd_ref[...], preferred_element_type=F32)
            y_ref[a * half:(a + 1) * half] = _to_token_major(y)

    @pl.when(t >= nact_ref[0])
    def _():
        y_ref[...] = jnp.zeros(y_ref.shape, F32)


def _moe(tile_expert, n_active, xs, w_eg, w_eu, w_ed):
    n_tiles = xs.shape[0] // TM_MOE
    tile = lambda t, te, na: (jnp.minimum(t, na[0] - 1), 0, 0)
    out_tile = lambda t, te, na: (t, 0, 0)
    wsel = lambda t, te, na: (te[t], 0, 0)
    return pl.pallas_call(
        _moe_kernel,
        grid_spec=pltpu.PrefetchScalarGridSpec(
            num_scalar_prefetch=2,
            grid=(n_tiles,),
            in_specs=[pl.BlockSpec((TM_MOE,) + ROW_TILE, tile),
                      pl.BlockSpec((None, D_MODEL, D_FF_EXP), wsel),
                      pl.BlockSpec((None, D_MODEL, D_FF_EXP), wsel),
                      pl.BlockSpec((None, D_FF_EXP, D_MODEL), wsel)],
            out_specs=pl.BlockSpec((TM_MOE,) + ROW_TILE, out_tile),
            scratch_shapes=[pltpu.VMEM((D_MODEL, D_FF_EXP), BF16),
                            pltpu.VMEM((D_MODEL, D_FF_EXP), BF16),
                            pltpu.VMEM((D_FF_EXP, D_MODEL), BF16)]),
        out_shape=jax.ShapeDtypeStruct(xs.shape, F32),
        compiler_params=_cparams(("arbitrary",)),
        name="moe",
    )(tile_expert, n_active, xs, w_eg, w_eu, w_ed)


def _combine_kernel(pos_ref, posn_ref, x1_ref, gates_ref, gf_ref, ys_ref, o_ref, buf_ref, sem):
    step = pl.program_id(0)
    slot = step % 2

    def gather(p_ref, s):
        def body(r, carry):
            _row_copy(ys_ref, p_ref[0, r], buf_ref.at[s, 0], r, sem.at[s]).start(priority=0)
            _row_copy(ys_ref, p_ref[0, TT + r], buf_ref.at[s, 1], r, sem.at[s]).start(priority=1)
            return carry
        lax.fori_loop(0, TT, body, 0)

    @pl.when(step == 0)
    def _():
        gather(pos_ref, 0)

    @pl.when(step < pl.num_programs(0) - 1)
    def _():
        gather(posn_ref, 1 - slot)

    for k in range(2):
        pltpu.make_async_copy(ys_ref.at[pl.ds(0, TT)], buf_ref.at[slot, k], sem.at[slot]).wait()
    gates = gates_ref[...]
    y = (x1_ref[...] + gates[:, 0:1] * _from_token_major(buf_ref[slot, 0])
         + gates[:, 1:2] * _from_token_major(buf_ref[slot, 1]))
    ms = jnp.mean(y * y, axis=-1, keepdims=True)
    o_ref[...] = y * lax.rsqrt(ms + EPS) * gf_ref[...]


def _combine(pos, x1, gates, gf, ys):
    n = x1.shape[0]
    nblk = n // TT
    return pl.pallas_call(
        _combine_kernel,
        grid=(nblk,),
        in_specs=[pl.BlockSpec((None, 1, 2 * TT), lambda i: (i, 0, 0), memory_space=pltpu.SMEM),
                  pl.BlockSpec((None, 1, 2 * TT), lambda i: (jnp.minimum(i + 1, nblk - 1), 0, 0),
                               memory_space=pltpu.SMEM),
                  pl.BlockSpec((TT, D_MODEL), lambda i: (i, 0)),
                  pl.BlockSpec((TT, LANES), lambda i: (i, 0)),
                  pl.BlockSpec((1, D_MODEL), lambda i: (0, 0)),
                  pl.BlockSpec(memory_space=pl.ANY)],
        out_specs=pl.BlockSpec((TT, D_MODEL), lambda i: (i, 0)),
        out_shape=jax.ShapeDtypeStruct((n, D_MODEL), F32),
        scratch_shapes=[pltpu.VMEM((2, 2, TT) + ROW_TILE, F32),
                        pltpu.SemaphoreType.DMA((2,))],
        compiler_params=_cparams(("arbitrary",)),
        name="combine",
    )(pos, pos, x1, gates, gf, ys)


def kernel(x, w_in, conv_w, conv_b, w_mq, w_mk, w_mgate, b_mgate, m_norm_g, m_skip, lambda_qk, da_norm_g,
           rel_bias, w_out, norm1_g, norm2_g, w_rg, b_rg, w_re, b_re, w_eg, w_eu, w_ed, normf_g):
    bsz, seq, _ = x.shape
    n = bsz * seq
    assert seq % TQ == 0 and seq % CHUNK == 0 and n % TM_PROJ == 0 and n % TT == 0 and bsz % MLSTM_BATCH == 0
    l = 0
    x2 = x.reshape(n, D_MODEL)

    w_in_b = w_in[l].astype(BF16)
    wqv_t = jnp.stack([w_in_b[:, 3 * 512:4 * 512].T, w_in_b[:, 5 * 512:6 * 512].T,
                       w_in_b[:, 1 * 512:2 * 512].T])
    c, vmt, z, qt, kd, vt = _inproj(x2, norm1_g[l][None, :], w_in_b, wqv_t, conv_w[l], conv_b[l][None, :], seq)

    wgt = jnp.zeros((2 * SUBLANES, 3 * D_MLSTM), F32)
    wgt = wgt.at[0:ML_HEADS].set(w_mgate[l][:, :ML_HEADS].T)
    wgt = wgt.at[SUBLANES:SUBLANES + ML_HEADS].set(w_mgate[l][:, ML_HEADS:].T).astype(BF16)
    bgt = jnp.zeros((2 * SUBLANES, 1), F32)
    bgt = bgt.at[0:ML_HEADS, 0].set(b_mgate[l][:ML_HEADS])
    bgt = bgt.at[SUBLANES:SUBLANES + ML_HEADS, 0].set(b_mgate[l][ML_HEADS:])
    y_m = _mlstm(c, vmt, z, jnp.swapaxes(w_mq[l], 1, 2).astype(BF16),
                 w_mk[l].astype(BF16), wgt, bgt, m_norm_g[l][None, :], m_skip[l][None, :], bsz, seq)

    y_d = _attn(lambda_qk[l].astype(F32), qt, kd, vt, _rel_bias_tiles(rel_bias),
                da_norm_g[l][:, None], bsz, seq)

    wr = jnp.zeros((D_MODEL, LANES), F32)
    wr = wr.at[:, :N_GROUPS].set(w_rg[l])
    wr = wr.at[:, SUBLANES:SUBLANES + N_EXPERTS].set(w_re[l].reshape(D_MODEL, N_EXPERTS))
    wr_hi = wr.astype(BF16)
    wr_lo = (wr - wr_hi.astype(F32)).astype(BF16)
    wr3 = jnp.concatenate([wr_hi, wr_hi, wr_lo], axis=0)
    br = jnp.zeros((1, LANES), F32)
    br = br.at[0, :N_GROUPS].set(b_rg[l])
    br = br.at[0, SUBLANES:SUBLANES + N_EXPERTS].set(b_re[l].reshape(N_EXPERTS))
    x1, h2, meta, gates, counts = _outproj(x2, y_m, y_d, w_out[l].astype(BF16), norm2_g[l][None, :], wr3, br)

    cnt = counts[:, 0]
    tiles_e = (cnt + TM_MOE - 1) // TM_MOE
    ends_t = jnp.cumsum(tiles_e)
    ends = jnp.concatenate([jnp.zeros((1,), jnp.int32), ends_t * TM_MOE]).astype(jnp.int32)
    offs = ends[:-1]
    n_tiles = (2 * n) // TM_MOE + N_EXPERTS
    n_active = ends_t[-1:].astype(jnp.int32)
    tile_ids = jnp.minimum(jnp.arange(n_tiles, dtype=jnp.int32), n_active[0] - 1)
    tile_expert = jnp.sum(tile_ids[:, None] >= ends_t[None, :], axis=1).astype(jnp.int32)
    eids = jnp.arange(N_EXPERTS, dtype=jnp.int32)[:, None]
    pos1 = jnp.sum(jnp.where(meta[0][None, :] == eids, offs[:, None], 0), axis=0) + meta[2]
    pos2 = jnp.sum(jnp.where(meta[1][None, :] == eids, offs[:, None], 0), axis=0) + meta[3]
    pos = jnp.concatenate([pos1.reshape(n // TT, 1, TT), pos2.reshape(n // TT, 1, TT)], axis=2)

    xs = _dispatch(ends, pos, h2, n_tiles * TM_MOE)
    ys = _moe(tile_expert, n_active, xs, w_eg[l], w_eu[l], w_ed[l])
    out = _combine(pos, x1, gates, normf_g[None, :], ys)
    return out.reshape(bsz, seq, D_MODEL)
```

```python
import functools
import math

import jax
import jax.numpy as jnp
from jax import lax
from jax.experimental import pallas as pl
from jax.experimental.pallas import tpu as pltpu

F32 = jnp.float32
BF16 = jnp.bfloat16

D_MODEL = 1024
D_MLSTM = 512
D_DIFF = 512
ML_HEADS = 4
ML_HD = 128
CONV_W = 4
CHUNK = 128
DA_HEADS = 4
DA_VD = 128
DA_QD = 64
REL_BUCKETS = 32
REL_MAX_DIST = 128
N_GROUPS = 4
EXP_PER_GROUP = 8
N_EXPERTS = 32
D_FF_EXP = 512
EPS = 1e-6
SUBLN_EPS = 1e-5
LAM_INIT = 0.8 - 0.6 * math.exp(-0.3 * 0)
NEG = -1e30
LOG2E = math.log2(math.e)

LANES = 128
SUBLANES = 8
MXU_COLS = 256

TM_PROJ = 512
INPROJ_PARTS = 2
OUTPROJ_PARTS = 2
TQ = 512
ACC_ROWS = DA_VD + 16
ATTN_HEADS = 2
MLSTM_BATCH = 4
TM_MOE = 512
MOE_SUBTILES = 2
TT = 512
VMEM_LIMIT = 48 * 1024 * 1024


ROW_TILE = (D_MODEL // LANES, LANES)


def _cparams(sem):
    return pltpu.CompilerParams(dimension_semantics=sem, vmem_limit_bytes=VMEM_LIMIT)


def _to_token_major(x):
    return pltpu.einshape("a(bc)->abc", x, b=ROW_TILE[0])


def _from_token_major(x):
    return pltpu.einshape("abc->a(bc)", x)


def _inproj_kernel(tiles_per_seq, x_ref, g_ref, w_ref, wqv_ref, cw_ref, cb_ref,
                   c_ref, vm_ref, z_ref, q_ref, k_ref, v_ref, ext_ref):
    @pl.when(pl.program_id(0) % tiles_per_seq == 0)
    def _():
        ext_ref[0:SUBLANES, :] = jnp.zeros((SUBLANES, D_MLSTM), F32)

    tm = TM_PROJ // INPROJ_PARTS
    nt = (((1,), (1,)), ((), ()))
    cw = cw_ref[...]

    def normed(a):
        x = x_ref[a * tm:(a + 1) * tm, :]
        ms = jnp.mean(x * x, axis=-1, keepdims=True)
        return (x * lax.rsqrt(ms + EPS) * g_ref[...]).astype(BF16)

    def conv_rows(a):
        r0 = SUBLANES + a * tm
        conv = cb_ref[...] + cw[CONV_W - 1:CONV_W, :] * ext_ref[r0:r0 + tm, :]
        for k in range(1, CONV_W):
            conv = conv + cw[CONV_W - 1 - k:CONV_W - k, :] * ext_ref[r0 - k:r0 - k + tm, :]
        c_ref[a * tm:(a + 1) * tm, :] = conv * jax.nn.sigmoid(conv)

    hs = []
    for a in range(INPROJ_PARTS):
        hs.append(normed(a))
        ext_ref[SUBLANES + a * tm:SUBLANES + (a + 1) * tm, :] = jnp.dot(hs[a], w_ref[:, 0:512],
                                                                          preferred_element_type=F32)
    for a in range(INPROJ_PARTS):
        h = hs[a]
        rs = slice(a * tm, (a + 1) * tm)
        z_ref[rs, :] = jnp.dot(h, w_ref[:, 2 * 512:3 * 512], preferred_element_type=F32)
        k_ref[rs, :] = jnp.dot(h, w_ref[:, 4 * 512:5 * 512], preferred_element_type=F32).astype(k_ref.dtype)
        qt = lax.dot_general(wqv_ref[0], h, nt, preferred_element_type=F32)
        q_ref[:, rs] = (qt * (DA_QD ** -0.5 * LOG2E)).astype(q_ref.dtype)
        v_ref[:, rs] = lax.dot_general(wqv_ref[1], h, nt, preferred_element_type=F32).astype(v_ref.dtype)
        vm_ref[:, rs] = lax.dot_general(wqv_ref[2], h, nt, preferred_element_type=F32).astype(vm_ref.dtype)
        conv_rows(a)
    ext_ref[0:SUBLANES, :] = ext_ref[TM_PROJ:TM_PROJ + SUBLANES, :]


def _inproj(x2, g1, w_in, wqv_t, conv_w, conv_b, seq):
    n = x2.shape[0]
    assert seq % TM_PROJ == 0
    row = lambda i: (i, 0)
    fixed = lambda i: (0, 0)
    sds = lambda dt: jax.ShapeDtypeStruct((n, 512), dt)
    sds_t = jax.ShapeDtypeStruct((512, n), BF16)
    blk = pl.BlockSpec((TM_PROJ, 512), row)
    blk_t = pl.BlockSpec((512, TM_PROJ), lambda i: (0, i))
    return pl.pallas_call(
        functools.partial(_inproj_kernel, seq // TM_PROJ),
        grid=(n // TM_PROJ,),
        in_specs=[pl.BlockSpec((TM_PROJ, D_MODEL), row),
                  pl.BlockSpec((1, D_MODEL), fixed),
                  pl.BlockSpec((D_MODEL, 6 * 512), fixed),
                  pl.BlockSpec((3, 512, D_MODEL), lambda i: (0, 0, 0)),
                  pl.BlockSpec((CONV_W, D_MLSTM), fixed),
                  pl.BlockSpec((1, D_MLSTM), fixed)],
        out_specs=[blk, blk_t, blk, blk_t, blk, blk_t],
        out_shape=[sds(F32), sds_t, sds(F32), sds_t, sds(BF16), sds_t],
        scratch_shapes=[pltpu.VMEM((SUBLANES + TM_PROJ, D_MLSTM), F32)],
        compiler_params=_cparams(("arbitrary",)),
        name="inproj",
    )(x2, g1, w_in, wqv_t, conv_w, conv_b)


def _mlstm_kernel(*refs):
    c_ref = refs[0]
    vt_refs = refs[1:1 + MLSTM_BATCH]
    (z_ref, wq_ref, wk_ref, wg_ref, bg_ref,
     ng_ref, sk_ref, tril_ref, y_ref, st_ref, m_ref) = refs[1 + MLSTM_BATCH:]
    j = pl.program_id(1)

    @pl.when(j == 0)
    def _():
        st_ref[...] = jnp.zeros(st_ref.shape, F32)
        m_ref[...] = jnp.zeros(m_ref.shape, F32)

    seqs = range(MLSTM_BATCH)
    proj = [_mlstm_qk(c_ref.at[bb], vt_refs[bb], wq_ref, wk_ref) for bb in seqs]
    for bb in seqs:
        _mlstm_gate_logits(proj[bb], wg_ref, bg_ref)
    for bb in seqs:
        _mlstm_cumulate(proj[bb], tril_ref)
    pairs = [(h, bb) for h in range(ML_HEADS) for bb in seqs]
    prods = [_mlstm_head_products(h, proj[bb], st_ref.at[bb]) for h, bb in pairs]
    for (h, bb), hp in zip(pairs, prods):
        _mlstm_head(h, proj[bb], hp, z_ref.at[bb], ng_ref, sk_ref, y_ref.at[bb], st_ref.at[bb], m_ref.at[bb])


_NT = (((1,), (1,)), ((), ()))


def _mlstm_qk(c_ref, vm_ref, wq_ref, wk_ref):
    c_act = c_ref[...]
    vt = vm_ref[...]
    qts, ks = [], []
    for h in range(ML_HEADS):
        ch = c_act[:, h * ML_HD:(h + 1) * ML_HD].astype(BF16)
        qt = lax.dot_general(wq_ref[h], ch, _NT, preferred_element_type=F32)
        qts.append(qt.astype(BF16))
        ks.append(jnp.dot(ch, wk_ref[h], preferred_element_type=F32))
    return dict(c_act=c_act, vt=vt, qts=qts, ks=ks)


def _mlstm_gate_logits(p, wg_ref, bg_ref):
    gates_t = jnp.dot(wg_ref[:, 2 * D_MLSTM:3 * D_MLSTM], p["vt"], preferred_element_type=F32) + bg_ref[...]
    for h in range(ML_HEADS):
        hs = slice(h * ML_HD, (h + 1) * ML_HD)
        gates_t = gates_t + jnp.dot(wg_ref[:, hs], p["qts"][h], preferred_element_type=F32)
        gates_t = gates_t + lax.dot_general(wg_ref[:, D_MLSTM + h * ML_HD:D_MLSTM + (h + 1) * ML_HD],
                                            p["ks"][h].astype(BF16), _NT, preferred_element_type=F32)
    p["gates_t"] = gates_t


def _mlstm_cumulate(p, tril_ref):
    L = CHUNK
    gates_t = p["gates_t"]
    logf_t = jnp.minimum(gates_t, 0.0) - jnp.log(1.0 + jnp.exp(-jnp.abs(gates_t)))
    bcum_t = jnp.dot(logf_t, tril_ref[...], preferred_element_type=F32, precision=lax.Precision.HIGHEST)
    a8 = gates_t[0:SUBLANES] - bcum_t[SUBLANES:2 * SUBLANES]
    p["bcum_t"] = bcum_t
    p["a_cols"] = jnp.concatenate([a8, jnp.zeros((LANES - SUBLANES, L), F32)], axis=0).T


def _mlstm_head_products(h, p, st_ref):
    qt = p["qts"][h]
    k = (p["ks"][h] * (ML_HD ** -0.5)).astype(BF16)
    state_t = st_ref[h]
    s_t = jnp.dot(k, qt, preferred_element_type=F32)
    r_inter = jnp.dot(state_t.astype(BF16), qt, preferred_element_type=F32)
    return dict(k=k, state_t=state_t, s_t=s_t, r_inter=r_inter)


def _mlstm_head(h, p, hp, z_ref, ng_ref, sk_ref, y_ref, st_ref, m_ref):
    L = CHUNK
    c_act, vt, gates_t, bcum_t, a_cols = p["c_act"], p["vt"], p["gates_t"], p["bcum_t"], p["a_cols"]
    k, state_t, s_t, r_inter = hp["k"], hp["state_t"], hp["s_t"], hp["r_inter"]
    s_idx = lax.broadcasted_iota(jnp.int32, (L, L), 0)
    t_idx = lax.broadcasted_iota(jnp.int32, (L, L), 1)
    causal = s_idx <= t_idx
    ones_row = jnp.where(lax.broadcasted_iota(jnp.int32, (ML_HD, L), 0) == 0, 1.0, 0.0).astype(BF16)
    sl = slice(h * ML_HD, (h + 1) * ML_HD)
    vaug_t = jnp.concatenate([vt[sl, :], ones_row], axis=0)
    i_row = gates_t[h:h + 1, :]
    b_row = bcum_t[SUBLANES + h:SUBLANES + h + 1, :]
    a_row = i_row - b_row
    m_prev = m_ref[h][0:1, 0:1]

    e = jnp.where(causal, a_cols[:, h:h + 1], NEG)
    cm = jnp.max(e, axis=0, keepdims=True)
    mm = jnp.maximum(m_prev, cm)
    dexp = jnp.exp(e - mm)
    w_t = (dexp * s_t).astype(BF16)
    sp = jnp.exp(m_prev - mm)
    r_intra = jnp.dot(vaug_t, w_t, preferred_element_type=F32)
    tot = sp * r_inter + r_intra
    num = tot[:ML_HD]
    den = tot[ML_HD:ML_HD + 1]
    h_t = num / jnp.maximum(jnp.abs(den), jnp.exp(-(b_row + mm)))

    b_end = b_row[:, L - 1:L]
    mm_end = mm[:, L - 1:L]
    wk = jnp.exp(a_row - mm_end)
    decay = jnp.exp(m_prev - mm_end)
    vw = (vaug_t.astype(F32) * wk).astype(BF16)
    upd = jnp.dot(vw, k, preferred_element_type=F32)
    st_ref[h] = decay * state_t + upd
    m_ref[h] = jnp.broadcast_to(b_end + mm_end, (SUBLANES, LANES))

    mu = jnp.mean(h_t, axis=0, keepdims=True)
    xc = h_t - mu
    var = jnp.mean(xc * xc, axis=0, keepdims=True)
    hn = (xc * lax.rsqrt(var + EPS)).T * ng_ref[:, sl]
    hn = hn + sk_ref[:, sl] * c_act[:, sl]
    o = jax.nn.sigmoid(z_ref[:, sl])
    y_ref[:, sl] = (o * hn).astype(y_ref.dtype)


def _mlstm(c, vmt, z, wqt, wk, wgt, bgt, ng, sk, bsz, seq):
    nc = seq // CHUNK
    f2 = lambda b, j: (0, 0)
    f3 = lambda b, j: (0, 0, 0)
    triu = jnp.triu(jnp.ones((CHUNK, CHUNK), F32))
    blk = pl.BlockSpec((MLSTM_BATCH, CHUNK, D_MLSTM), lambda b, j: (b, j, 0))
    vt_specs = [pl.BlockSpec((D_MLSTM, CHUNK), functools.partial(lambda b, j, bb: (0, (b * MLSTM_BATCH + bb) * nc + j),
                                                                  bb=bb))
                for bb in range(MLSTM_BATCH)]
    c, z = (a.reshape(bsz, seq, D_MLSTM) for a in (c, z))
    y = pl.pallas_call(
        _mlstm_kernel,
        grid=(bsz // MLSTM_BATCH, nc),
        in_specs=[blk] + vt_specs + [
                  blk,
                  pl.BlockSpec((ML_HEADS, ML_HD, ML_HD), f3),
                  pl.BlockSpec((ML_HEADS, ML_HD, ML_HD), f3),
                  pl.BlockSpec((2 * SUBLANES, 3 * D_MLSTM), f2),
                  pl.BlockSpec((2 * SUBLANES, 1), f2),
                  pl.BlockSpec((1, D_MLSTM), f2),
                  pl.BlockSpec((1, D_MLSTM), f2),
                  pl.BlockSpec((CHUNK, CHUNK), f2)],
        out_specs=blk,
        out_shape=jax.ShapeDtypeStruct((bsz, seq, D_MLSTM), BF16),
        scratch_shapes=[pltpu.VMEM((MLSTM_BATCH, ML_HEADS, 2 * ML_HD, ML_HD), F32),
                        pltpu.VMEM((MLSTM_BATCH, ML_HEADS, SUBLANES, LANES), F32)],
        compiler_params=_cparams(("parallel", "arbitrary")),
        name="mlstm",
    )(c, *([vmt] * MLSTM_BATCH), z, wqt, wk, wgt, bgt, ng, sk, triu)
    return y.reshape(bsz * seq, D_MLSTM)


def _attn_kernel(lq_ref, qt_ref, k_ref, vt_ref, bias_ref, g_ref, o_ref,
                 acc_ref, m_ref, q2_ref, s_ref, p_ref, a_ref):
    tq = TQ
    i = pl.program_id(2)
    heads = range(ATTN_HEADS)
    row = lax.broadcasted_iota(jnp.int32, (LANES, tq), 0)
    for hh in heads:
        qt = qt_ref[hh * LANES:(hh + 1) * LANES, :]
        zero = jnp.zeros_like(qt)
        q2_ref[hh, :, 0:tq] = jnp.where(row < DA_QD, qt, zero)
        q2_ref[hh, :, tq:2 * tq] = jnp.where(row >= DA_QD, qt, zero)
    acc_ref[...] = jnp.zeros(acc_ref.shape, F32)
    m_ref[...] = jnp.full(m_ref.shape, NEG, F32)

    p_ref[:, 1] = jnp.zeros((ATTN_HEADS,) + p_ref.shape[2:], BF16)
    a_ref[:, 1] = jnp.ones((ATTN_HEADS,) + a_ref.shape[2:], F32)

    def logits(hh, kb, buf):
        start = pl.multiple_of(kb * tq, tq)
        s_ref[hh, buf] = jnp.dot(k_ref[pl.ds(start, tq), hh * LANES:(hh + 1) * LANES], q2_ref[hh],
                                 preferred_element_type=F32)

    ones_rows = jnp.ones((ACC_ROWS - DA_VD, tq), BF16)

    def values(hh, kb, buf):
        start = pl.multiple_of(jnp.maximum(kb, 0) * tq, tq)
        lhs = jnp.concatenate([vt_ref[hh * LANES:(hh + 1) * LANES, pl.ds(start, tq)], ones_rows], axis=0)
        pv = jnp.dot(lhs, p_ref[hh, buf], preferred_element_type=F32)
        acc_ref[hh] = a_ref[hh, buf] * acc_ref[hh] + pv

    def softmax(hh, buf, which_bias):
        if which_bias == 1:
            ks = slice(tq - REL_MAX_DIST, tq)
            corner = bias_ref[hh, 1, ks, 0:REL_MAX_DIST]
            for c0 in (0, tq):
                s_ref[hh, buf, ks, c0:c0 + REL_MAX_DIST] = s_ref[hh, buf, ks, c0:c0 + REL_MAX_DIST] + corner
        s = s_ref[hh, buf]
        if which_bias == 0:
            bias = bias_ref[hh, 0]
            s = s + jnp.concatenate([bias, bias], axis=1)
        m_old = m_ref[hh]
        m_new = jnp.maximum(m_old, jnp.max(s, axis=0, keepdims=True))
        m_ref[hh] = m_new
        a_ref[hh, buf] = jnp.exp2(m_old - m_new)
        p_ref[hh, buf] = jnp.exp2(s - m_new).astype(BF16)

    def all_logits(kb, buf):
        for hh in heads:
            logits(hh, kb, buf)

    def all_values(kb, buf):
        for hh in heads:
            values(hh, kb, buf)

    def stage(kb, buf, has_next, which_bias):
        for hh in heads:
            softmax(hh, buf, which_bias)
        if has_next:
            all_logits(kb + 1, 1 - buf)
        all_values(kb - 1, 1 - buf)

    all_logits(0, 0)
    n_far = jnp.maximum(i - 1, 0)

    def far_pair(t, carry):
        stage(2 * t, 0, True, None)
        stage(2 * t + 1, 1, True, None)
        return carry

    lax.fori_loop(0, n_far // 2, far_pair, 0)
    kb0 = 2 * (n_far // 2)

    @pl.when(i == 0)
    def _():
        stage(0, 0, False, 0)
        all_values(0, 0)

    @pl.when(jnp.logical_and(i >= 1, n_far % 2 == 0))
    def _():
        stage(kb0, 0, True, 1)
        stage(kb0 + 1, 1, False, 0)
        all_values(kb0 + 1, 1)

    @pl.when(n_far % 2 == 1)
    def _():
        stage(kb0, 0, True, None)
        stage(kb0 + 1, 1, True, 1)
        stage(kb0 + 2, 0, False, 0)
        all_values(kb0 + 2, 0)

    lq = lq_ref[...]
    d01 = jnp.sum(lq[0:1, :] * lq[1:2, :], axis=-1, keepdims=True)
    d23 = jnp.sum(lq[2:3, :] * lq[3:4, :], axis=-1, keepdims=True)
    lam = jnp.exp(d01) - jnp.exp(d23) + LAM_INIT
    for hh in heads:
        o = acc_ref[hh, 0:DA_VD, :] * (1.0 / acc_ref[hh, DA_VD:DA_VD + 1, :])
        y = o[:, :tq] - lam * o[:, tq:]
        y = y * lax.rsqrt(jnp.mean(y * y, axis=0, keepdims=True) + SUBLN_EPS) * g_ref[...]
        o_ref[:, hh * LANES:(hh + 1) * LANES] = (y * (1.0 - LAM_INIT)).T.astype(o_ref.dtype)


def _rel_bias_tiles(rel_bias):
    L = TQ
    max_exact = REL_BUCKETS // 2
    n = jnp.arange(-(L - 1), 2 * L, dtype=jnp.int32)
    nn = jnp.maximum(n, 0)
    large = max_exact + (jnp.log(jnp.maximum(nn, 1).astype(F32) / max_exact)
                         / math.log(REL_MAX_DIST / max_exact) * (REL_BUCKETS - max_exact)).astype(jnp.int32)
    large = jnp.minimum(large, REL_BUCKETS - 1)
    bucket = jnp.where(nn < max_exact, nn, large)
    rb = (rel_bias.astype(F32) - rel_bias[REL_BUCKETS - 1].astype(F32)[None, :]).T
    t = jnp.where(n[None, :] >= 0, rb[:, bucket] * LOG2E, NEG)

    def toeplitz(v):
        vp = jnp.pad(v, ((0, 0), (0, 1)))
        flat = jnp.tile(vp, (1, L))[:, :L * (2 * L - 1)]
        return flat.reshape(v.shape[0], L, 2 * L - 1)[:, :, L - 1:]

    return jnp.stack([toeplitz(t[:, :2 * L - 1]), toeplitz(t[:, L:])], axis=1)


def _attn(lq, qt, kd, vt, bias, g, bsz, seq):
    nq = seq // TQ
    return pl.pallas_call(
        _attn_kernel,
        grid=(bsz, DA_HEADS // ATTN_HEADS, nq),
        in_specs=[pl.BlockSpec((4, DA_QD), lambda b, h, i: (0, 0)),
                  pl.BlockSpec((ATTN_HEADS * LANES, TQ), lambda b, h, i: (h, b * nq + i)),
                  pl.BlockSpec((seq, ATTN_HEADS * LANES), lambda b, h, i: (b, h)),
                  pl.BlockSpec((ATTN_HEADS * LANES, seq), lambda b, h, i: (h, b)),
                  pl.BlockSpec((ATTN_HEADS, 2, TQ, TQ), lambda b, h, i: (h, 0, 0, 0)),
                  pl.BlockSpec((DA_VD, 1), lambda b, h, i: (0, 0))],
        out_specs=pl.BlockSpec((TQ, ATTN_HEADS * LANES), lambda b, h, i: (b * nq + i, h)),
        out_shape=jax.ShapeDtypeStruct((bsz * seq, D_DIFF), BF16),
        scratch_shapes=[pltpu.VMEM((ATTN_HEADS, ACC_ROWS, 2 * TQ), F32),
                        pltpu.VMEM((ATTN_HEADS, 1, 2 * TQ), F32),
                        pltpu.VMEM((ATTN_HEADS, LANES, 2 * TQ), BF16),
                        pltpu.VMEM((ATTN_HEADS, 2, TQ, 2 * TQ), F32),
                        pltpu.VMEM((ATTN_HEADS, 2, TQ, 2 * TQ), BF16),
                        pltpu.VMEM((ATTN_HEADS, 2, 1, 2 * TQ), F32)],
        compiler_params=_cparams(("parallel", "parallel", "arbitrary")),
        name="attn",
    )(lq, qt, kd, vt, bias, g)


def _outproj_kernel(x_ref, ym_ref, yd_ref, wo_ref, g2_ref, wr_ref, br_ref, tri_ref,
                    x1_ref, h2_ref, meta_ref, gates_ref, cnt_ref, run_ref):
    step = pl.program_id(0)

    @pl.when(step == 0)
    def _():
        run_ref[...] = jnp.zeros(run_ref.shape, F32)

    tm = TM_PROJ // OUTPROJ_PARTS
    parts = [slice(a * tm, (a + 1) * tm) for a in range(OUTPROJ_PARTS)]
    x1s = [_outproj_residual(rs, x_ref, ym_ref, yd_ref, wo_ref, x1_ref) for rs in parts]
    lts = [_outproj_logits(rs, x1, g2_ref, wr_ref, br_ref, h2_ref) for rs, x1 in zip(parts, x1s)]
    for rs, lt in zip(parts, lts):
        _outproj_route(rs, lt, tri_ref, meta_ref, gates_ref, cnt_ref, run_ref)


def _outproj_residual(rs, x_ref, ym_ref, yd_ref, wo_ref, x1_ref):
    x1 = (x_ref[rs, :]
          + jnp.dot(ym_ref[rs, :], wo_ref[0:D_MLSTM, :], preferred_element_type=F32)
          + jnp.dot(yd_ref[rs, :], wo_ref[D_MLSTM:, :], preferred_element_type=F32))
    x1_ref[rs, :] = x1
    return x1


def _outproj_logits(rs, x1, g2_ref, wr_ref, br_ref, h2_ref):
    ms = jnp.mean(x1 * x1, axis=-1, keepdims=True)
    h2 = x1 * lax.rsqrt(ms + EPS) * g2_ref[...]
    h2_ref[rs] = _to_token_major(h2)
    h_hi = h2.astype(BF16)
    h_lo = (h2 - h_hi.astype(F32)).astype(BF16)
    lhs = jnp.concatenate([h_hi, h_lo, h_hi], axis=1)
    logits = jnp.dot(lhs, wr_ref[...], preferred_element_type=F32) + br_ref[...]
    return logits.T


def _outproj_route(rs, lt, tri_ref, meta_ref, gates_ref, cnt_ref, run_ref):
    tm = lt.shape[1]
    row8 = lax.broadcasted_iota(jnp.int32, (SUBLANES, tm), 0)
    lg = jnp.where(row8 < N_GROUPS, lt[0:SUBLANES], NEG)
    gmax = jnp.max(lg, axis=0, keepdims=True)
    gidx = jnp.min(jnp.where(lg == gmax, row8, SUBLANES), axis=0, keepdims=True)
    pg_sel = 1.0 / jnp.sum(jnp.exp(lg - gmax), axis=0, keepdims=True)
    le = jnp.zeros((EXP_PER_GROUP, tm), F32)
    for g in range(N_GROUPS):
        le = jnp.where(gidx == g, lt[SUBLANES * (g + 1):SUBLANES * (g + 2)], le)
    v1 = jnp.max(le, axis=0, keepdims=True)
    i1 = jnp.min(jnp.where(le == v1, row8, SUBLANES), axis=0, keepdims=True)
    le2 = jnp.where(row8 == i1, NEG, le)
    v2 = jnp.max(le2, axis=0, keepdims=True)
    i2 = jnp.min(jnp.where(le2 == v2, row8, SUBLANES), axis=0, keepdims=True)
    e21 = jnp.exp(v2 - v1)
    pw1 = 1.0 / (1.0 + e21)
    gate1 = pg_sel * pw1
    gate2 = pg_sel * (e21 * pw1)
    eid1 = gidx * EXP_PER_GROUP + i1
    eid2 = gidx * EXP_PER_GROUP + i2

    row32 = lax.broadcasted_iota(jnp.int32, (N_EXPERTS, tm), 0)
    oh1 = row32 == eid1
    oh2 = row32 == eid2
    oh = jnp.where(oh1, 1.0, 0.0) + jnp.where(oh2, 1.0, 0.0)
    before = jnp.dot(oh.astype(BF16), tri_ref[...], preferred_element_type=F32) + run_ref[:, 0:1]
    rank1 = jnp.sum(jnp.where(oh1, before, 0.0), axis=0, keepdims=True).astype(jnp.int32)
    rank2 = jnp.sum(jnp.where(oh2, before, 0.0), axis=0, keepdims=True).astype(jnp.int32)
    run = run_ref[...] + jnp.sum(oh, axis=1, keepdims=True)
    run_ref[...] = run
    cnt_ref[...] = run.astype(jnp.int32)

    meta_ref[:, rs] = jnp.where(row8 == 0, eid1, jnp.where(row8 == 1, eid2,
                                jnp.where(row8 == 2, rank1, jnp.where(row8 == 3, rank2, 0))))
    row128 = lax.broadcasted_iota(jnp.int32, (LANES, tm), 0)
    gt = jnp.where(row128 == 0, gate1, jnp.where(row128 == 1, gate2, 0.0))
    gates_ref[rs, :] = gt.T


def _outproj(x2, ym, yd, wo, g2, wr, br):
    n = x2.shape[0]
    row = lambda i: (i, 0)
    fixed = lambda i: (0, 0)
    part = TM_PROJ // OUTPROJ_PARTS
    tri = jnp.triu(jnp.ones((part, part), F32), k=1).astype(BF16)
    return pl.pallas_call(
        _outproj_kernel,
        grid=(n // TM_PROJ,),
        in_specs=[pl.BlockSpec((TM_PROJ, D_MODEL), row),
                  pl.BlockSpec((TM_PROJ, D_MLSTM), row),
                  pl.BlockSpec((TM_PROJ, D_DIFF), row),
                  pl.BlockSpec((D_MODEL, D_MODEL), fixed),
                  pl.BlockSpec((1, D_MODEL), fixed),
                  pl.BlockSpec((3 * D_MODEL, LANES), fixed),
                  pl.BlockSpec((1, LANES), fixed),
                  pl.BlockSpec((part, part), fixed)],
        out_specs=[pl.BlockSpec((TM_PROJ, D_MODEL), row),
                   pl.BlockSpec((TM_PROJ,) + ROW_TILE, lambda i: (i, 0, 0)),
                   pl.BlockSpec((SUBLANES, TM_PROJ), lambda i: (0, i)),
                   pl.BlockSpec((TM_PROJ, LANES), row),
                   pl.BlockSpec((N_EXPERTS, LANES), fixed)],
        out_shape=[jax.ShapeDtypeStruct((n, D_MODEL), F32),
                   jax.ShapeDtypeStruct((n,) + ROW_TILE, F32),
                   jax.ShapeDtypeStruct((SUBLANES, n), jnp.int32),
                   jax.ShapeDtypeStruct((n, LANES), F32),
                   jax.ShapeDtypeStruct((N_EXPERTS, LANES), jnp.int32)],
        scratch_shapes=[pltpu.VMEM((N_EXPERTS, LANES), F32)],
        compiler_params=_cparams(("arbitrary",)),
        name="outproj_router",
    )(x2, ym, yd, wo, g2, wr, br, tri)


def _row_copy(src_ref, src_row, dst_ref, dst_row, sem):
    return pltpu.make_async_copy(src_ref.at[pl.ds(src_row, 1)], dst_ref.at[pl.ds(dst_row, 1)], sem)


def _dispatch_kernel(ends_ref, pos_ref, h_ref, xs_ref, zero_ref, hbuf_ref, sem, bsem, zsem):
    step = pl.program_id(0)

    @pl.when(step == 0)
    def _():
        zero_ref[...] = jnp.zeros(zero_ref.shape, F32)

        def tile_copy(e):
            start = pl.multiple_of(ends_ref[e + 1] - TM_MOE, TM_MOE)
            return pltpu.make_async_copy(zero_ref, xs_ref.at[pl.ds(start, TM_MOE)], zsem)

        def zstart(e, carry):
            @pl.when(ends_ref[e + 1] > ends_ref[e])
            def _():
                tile_copy(e).start()
            return carry

        def zwait(e, carry):
            @pl.when(ends_ref[e + 1] > ends_ref[e])
            def _():
                tile_copy(e).wait()
            return carry

        lax.fori_loop(0, N_EXPERTS, zstart, 0)
        lax.fori_loop(0, N_EXPERTS, zwait, 0)

        def tail_copy(t):
            return pltpu.make_async_copy(zero_ref, xs_ref.at[pl.ds(pl.multiple_of(t * TM_MOE, TM_MOE), TM_MOE)], zsem)

        def tstart(t, carry):
            tail_copy(t).start()
            return carry

        def twait(t, carry):
            tail_copy(t).wait()
            return carry

        first_tail = ends_ref[N_EXPERTS] // TM_MOE
        lax.fori_loop(first_tail, xs_ref.shape[0] // TM_MOE, tstart, 0)
        lax.fori_loop(first_tail, xs_ref.shape[0] // TM_MOE, twait, 0)

    last = pl.num_programs(0) - 1
    slot = step % 3
    par = step % 2

    def block_copy(i, s):
        return pltpu.make_async_copy(h_ref.at[pl.ds(pl.multiple_of(i * TT, TT), TT)], hbuf_ref.at[s], bsem.at[s])

    @pl.when(step == 0)
    def _():
        block_copy(0, 0).start()

    @pl.when(step < last)
    def _():
        block_copy(step + 1, (step + 1) % 3).start()

    block_copy(step, slot).wait()
    src = hbuf_ref.at[slot]

    def issue(r, carry):
        _row_copy(src, r, xs_ref, pos_ref[0, r], sem.at[par]).start(priority=0)
        _row_copy(src, r, xs_ref, pos_ref[0, TT + r], sem.at[par]).start(priority=1)
        return carry

    def drain(s):
        for _ in range(2):
            pltpu.make_async_copy(src, xs_ref.at[pl.ds(0, TT)], sem.at[s]).wait()

    lax.fori_loop(0, TT, issue, 0)

    @pl.when(step > 0)
    def _():
        drain(1 - par)

    @pl.when(step == last)
    def _():
        drain(par)


def _dispatch(ends, pos, h2, n_rows):
    n = h2.shape[0]
    return pl.pallas_call(
        _dispatch_kernel,
        grid_spec=pltpu.PrefetchScalarGridSpec(
            num_scalar_prefetch=1,
            grid=(n // TT,),
            in_specs=[pl.BlockSpec((None, 1, 2 * TT), lambda i, ends: (i, 0, 0), memory_space=pltpu.SMEM),
                      pl.BlockSpec(memory_space=pl.ANY)],
            out_specs=pl.BlockSpec(memory_space=pl.ANY),
            scratch_shapes=[pltpu.VMEM((TM_MOE,) + ROW_TILE, F32),
                            pltpu.VMEM((3, TT) + ROW_TILE, F32),
                            pltpu.SemaphoreType.DMA((2,)),
                            pltpu.SemaphoreType.DMA((3,)),
                            pltpu.SemaphoreType.DMA(())]),
        out_shape=jax.ShapeDtypeStruct((n_rows,) + ROW_TILE, F32),
        compiler_params=_cparams(("arbitrary",)),
        name="dispatch",
    )(ends, pos, h2)


def _moe_kernel(texp_ref, nact_ref, x_ref, wg32_ref, wu32_ref, wd32_ref, y_ref, wg_ref, wu_ref, wd_ref):
    t = pl.program_id(0)

    @pl.when(jnp.logical_or(t == 0, texp_ref[t] != texp_ref[jnp.maximum(t - 1, 0)]))
    def _():
        wg_ref[...] = wg32_ref[...].astype(BF16)
        wu_ref[...] = wu32_ref[...].astype(BF16)
        wd_ref[...] = wd32_ref[...].astype(BF16)

    @pl.when(t < nact_ref[0])
    def _():
        half = TM_MOE // MOE_SUBTILES
        hids = []
        for a in range(MOE_SUBTILES):
            x = _from_token_major(x_ref[a * half:(a + 1) * half]).astype(BF16)
            g = jnp.dot(x, wg_ref[...], preferred_element_type=F32)
            u = jnp.dot(x, wu_ref[...], preferred_element_type=F32)
            hids.append((g * jax.nn.sigmoid(g) * u).astype(BF16))
        for a in range(MOE_SUBTILES):
            y = jnp.dot(hids[a], wd_ref[...], preferred_element_type=F32)
            y_ref[a * half:(a + 1) * half] = _to_token_major(y)

    @pl.when(t >= nact_ref[0])
    def _():
        y_ref[...] = jnp.zeros(y_ref.shape, F32)


def _moe(tile_expert, n_active, xs, w_eg, w_eu, w_ed):
    n_tiles = xs.shape[0] // TM_MOE
    tile = lambda t, te, na: (jnp.minimum(t, na[0] - 1), 0, 0)
    out_tile = lambda t, te, na: (t, 0, 0)
    wsel = lambda t, te, na: (te[t], 0, 0)
    return pl.pallas_call(
        _moe_kernel,
        grid_spec=pltpu.PrefetchScalarGridSpec(
            num_scalar_prefetch=2,
            grid=(n_tiles,),
            in_specs=[pl.BlockSpec((TM_MOE,) + ROW_TILE, tile),
                      pl.BlockSpec((None, D_MODEL, D_FF_EXP), wsel),
                      pl.BlockSpec((None, D_MODEL, D_FF_EXP), wsel),
                      pl.BlockSpec((None, D_FF_EXP, D_MODEL), wsel)],
            out_specs=pl.BlockSpec((TM_MOE,) + ROW_TILE, out_tile),
            scratch_shapes=[pltpu.VMEM((D_MODEL, D_FF_EXP), BF16),
                            pltpu.VMEM((D_MODEL, D_FF_EXP), BF16),
                            pltpu.VMEM((D_FF_EXP, D_MODEL), BF16)]),
        out_shape=jax.ShapeDtypeStruct(xs.shape, F32),
        compiler_params=_cparams(("arbitrary",)),
        name="moe",
    )(tile_expert, n_active, xs, w_eg, w_eu, w_ed)


def _combine_kernel(pos_ref, posn_ref, x1_ref, gates_ref, gf_ref, ys_ref, o_ref, buf_ref, sem):
    step = pl.program_id(0)
    slot = step % 2

    def gather(p_ref, s):
        def body(r, carry):
            _row_copy(ys_ref, p_ref[0, r], buf_ref.at[s, 0], r, sem.at[s]).start(priority=0)
            _row_copy(ys_ref, p_ref[0, TT + r], buf_ref.at[s, 1], r, sem.at[s]).start(priority=1)
            return carry
        lax.fori_loop(0, TT, body, 0)

    @pl.when(step == 0)
    def _():
        gather(pos_ref, 0)

    @pl.when(step < pl.num_programs(0) - 1)
    def _():
        gather(posn_ref, 1 - slot)

    for k in range(2):
        pltpu.make_async_copy(ys_ref.at[pl.ds(0, TT)], buf_ref.at[slot, k], sem.at[slot]).wait()
    gates = gates_ref[...]
    y = (x1_ref[...] + gates[:, 0:1] * _from_token_major(buf_ref[slot, 0])
         + gates[:, 1:2] * _from_token_major(buf_ref[slot, 1]))
    ms = jnp.mean(y * y, axis=-1, keepdims=True)
    o_ref[...] = y * lax.rsqrt(ms + EPS) * gf_ref[...]


def _combine(pos, x1, gates, gf, ys):
    n = x1.shape[0]
    nblk = n // TT
    return pl.pallas_call(
        _combine_kernel,
        grid=(nblk,),
        in_specs=[pl.BlockSpec((None, 1, 2 * TT), lambda i: (i, 0, 0), memory_space=pltpu.SMEM),
                  pl.BlockSpec((None, 1, 2 * TT), lambda i: (jnp.minimum(i + 1, nblk - 1), 0, 0),
                               memory_space=pltpu.SMEM),
                  pl.BlockSpec((TT, D_MODEL), lambda i: (i, 0)),
                  pl.BlockSpec((TT, LANES), lambda i: (i, 0)),
                  pl.BlockSpec((1, D_MODEL), lambda i: (0, 0)),
                  pl.BlockSpec(memory_space=pl.ANY)],
        out_specs=pl.BlockSpec((TT, D_MODEL), lambda i: (i, 0)),
        out_shape=jax.ShapeDtypeStruct((n, D_MODEL), F32),
        scratch_shapes=[pltpu.VMEM((2, 2, TT) + ROW_TILE, F32),
                        pltpu.SemaphoreType.DMA((2,))],
        compiler_params=_cparams(("arbitrary",)),
        name="combine",
    )(pos, pos, x1, gates, gf, ys)


def kernel(x, w_in, conv_w, conv_b, w_mq, w_mk, w_mgate, b_mgate, m_norm_g, m_skip, lambda_qk, da_norm_g,
           rel_bias, w_out, norm1_g, norm2_g, w_rg, b_rg, w_re, b_re, w_eg, w_eu, w_ed, normf_g):
    bsz, seq, _ = x.shape
    n = bsz * seq
    assert seq % TQ == 0 and seq % CHUNK == 0 and n % TM_PROJ == 0 and n % TT == 0 and bsz % MLSTM_BATCH == 0
    l = 0
    x2 = x.reshape(n, D_MODEL)

    w_in_b = w_in[l].astype(BF16)
    wqv_t = jnp.stack([w_in_b[:, 3 * 512:4 * 512].T, w_in_b[:, 5 * 512:6 * 512].T,
                       w_in_b[:, 1 * 512:2 * 512].T])
    c, vmt, z, qt, kd, vt = _inproj(x2, norm1_g[l][None, :], w_in_b, wqv_t, conv_w[l], conv_b[l][None, :], seq)

    wgt = jnp.zeros((2 * SUBLANES, 3 * D_MLSTM), F32)
    wgt = wgt.at[0:ML_HEADS].set(w_mgate[l][:, :ML_HEADS].T)
    wgt = wgt.at[SUBLANES:SUBLANES + ML_HEADS].set(w_mgate[l][:, ML_HEADS:].T).astype(BF16)
    bgt = jnp.zeros((2 * SUBLANES, 1), F32)
    bgt = bgt.at[0:ML_HEADS, 0].set(b_mgate[l][:ML_HEADS])
    bgt = bgt.at[SUBLANES:SUBLANES + ML_HEADS, 0].set(b_mgate[l][ML_HEADS:])
    y_m = _mlstm(c, vmt, z, jnp.swapaxes(w_mq[l], 1, 2).astype(BF16),
                 w_mk[l].astype(BF16), wgt, bgt, m_norm_g[l][None, :], m_skip[l][None, :], bsz, seq)

    y_d = _attn(lambda_qk[l].astype(F32), qt, kd, vt, _rel_bias_tiles(rel_bias),
                da_norm_g[l][:, None], bsz, seq)

    wr = jnp.zeros((D_MODEL, LANES), F32)
    wr = wr.at[:, :N_GROUPS].set(w_rg[l])
    wr = wr.at[:, SUBLANES:SUBLANES + N_EXPERTS].set(w_re[l].reshape(D_MODEL, N_EXPERTS))
    wr_hi = wr.astype(BF16)
    wr_lo = (wr - wr_hi.astype(F32)).astype(BF16)
    wr3 = jnp.concatenate([wr_hi, wr_hi, wr_lo], axis=0)
    br = jnp.zeros((1, LANES), F32)
    br = br.at[0, :N_GROUPS].set(b_rg[l])
    br = br.at[0, SUBLANES:SUBLANES + N_EXPERTS].set(b_re[l].reshape(N_EXPERTS))
    x1, h2, meta, gates, counts = _outproj(x2, y_m, y_d, w_out[l].astype(BF16), norm2_g[l][None, :], wr3, br)

    cnt = counts[:, 0]
    tiles_e = (cnt + TM_MOE - 1) // TM_MOE
    ends_t = jnp.cumsum(tiles_e)
    ends = jnp.concatenate([jnp.zeros((1,), jnp.int32), ends_t * TM_MOE]).astype(jnp.int32)
    offs = ends[:-1]
    n_tiles = (2 * n) // TM_MOE + N_EXPERTS
    n_active = ends_t[-1:].astype(jnp.int32)
    tile_ids = jnp.minimum(jnp.arange(n_tiles, dtype=jnp.int32), n_active[0] - 1)
    tile_expert = jnp.sum(tile_ids[:, None] >= ends_t[None, :], axis=1).astype(jnp.int32)
    eids = jnp.arange(N_EXPERTS, dtype=jnp.int32)[:, None]
    pos1 = jnp.sum(jnp.where(meta[0][None, :] == eids, offs[:, None], 0), axis=0) + meta[2]
    pos2 = jnp.sum(jnp.where(meta[1][None, :] == eids, offs[:, None], 0), axis=0) + meta[3]
    pos = jnp.concatenate([pos1.reshape(n // TT, 1, TT), pos2.reshape(n // TT, 1, TT)], axis=2)

    xs = _dispatch(ends, pos, h2, n_tiles * TM_MOE)
    ys = _moe(tile_expert, n_active, xs, w_eg[l], w_eu[l], w_ed[l])
    out = _combine(pos, x1, gates, normf_g[None, :], ys)
    return out.reshape(bsz, seq, D_MODEL)
```

```python
import functools
import math

import jax
import jax.numpy as jnp
from jax import lax
from jax.experimental import pallas as pl
from jax.experimental.pallas import tpu as pltpu

F32 = jnp.float32
BF16 = jnp.bfloat16

D_MODEL = 1024
D_MLSTM = 512
D_DIFF = 512
ML_HEADS = 4
ML_HD = 128
CONV_W = 4
CHUNK = 128
DA_HEADS = 4
DA_VD = 128
DA_QD = 64
REL_BUCKETS = 32
REL_MAX_DIST = 128
N_GROUPS = 4
EXP_PER_GROUP = 8
N_EXPERTS = 32
D_FF_EXP = 512
EPS = 1e-6
SUBLN_EPS = 1e-5
LAM_INIT = 0.8 - 0.6 * math.exp(-0.3 * 0)
NEG = -1e30
LOG2E = math.log2(math.e)

LANES = 128
SUBLANES = 8

TM_PROJ = 512
INPROJ_PARTS = 2
OUTPROJ_PARTS = 2
TQ = 512
ACC_ROWS = DA_VD + 16
ATTN_HEADS = 2
MLSTM_BATCH = 4
TM_MOE = 512
MOE_SUBTILES = 2
TT = 512
VMEM_LIMIT = 48 * 1024 * 1024


ROW_TILE = (D_MODEL // LANES, LANES)


def _cparams(sem):
    return pltpu.CompilerParams(dimension_semantics=sem, vmem_limit_bytes=VMEM_LIMIT)


def _to_token_major(x):
    return pltpu.einshape("a(bc)->abc", x, b=ROW_TILE[0])


def _from_token_major(x):
    return pltpu.einshape("abc->a(bc)", x)


def _inproj_kernel(tiles_per_seq, x_ref, g_ref, w_ref, wqv_ref, cw_ref, cb_ref,
                   c_ref, vm_ref, z_ref, q_ref, k_ref, v_ref, ext_ref):
    @pl.when(pl.program_id(0) % tiles_per_seq == 0)
    def _():
        ext_ref[0:SUBLANES, :] = jnp.zeros((SUBLANES, D_MLSTM), F32)

    tm = TM_PROJ // INPROJ_PARTS
    nt = (((1,), (1,)), ((), ()))
    cw = cw_ref[...]

    def normed(a):
        x = x_ref[a * tm:(a + 1) * tm, :]
        ms = jnp.mean(x * x, axis=-1, keepdims=True)
        return (x * lax.rsqrt(ms + EPS) * g_ref[...]).astype(BF16)

    def conv_rows(a):
        r0 = SUBLANES + a * tm
        conv = cb_ref[...] + cw[CONV_W - 1:CONV_W, :] * ext_ref[r0:r0 + tm, :]
        for k in range(1, CONV_W):
            conv = conv + cw[CONV_W - 1 - k:CONV_W - k, :] * ext_ref[r0 - k:r0 - k + tm, :]
        c_ref[a * tm:(a + 1) * tm, :] = conv * jax.nn.sigmoid(conv)

    hs = []
    for a in range(INPROJ_PARTS):
        hs.append(normed(a))
        ext_ref[SUBLANES + a * tm:SUBLANES + (a + 1) * tm, :] = jnp.dot(hs[a], w_ref[:, 0:512],
                                                                          preferred_element_type=F32)
    for a in range(INPROJ_PARTS):
        h = hs[a]
        rs = slice(a * tm, (a + 1) * tm)
        z_ref[rs, :] = jnp.dot(h, w_ref[:, 2 * 512:3 * 512], preferred_element_type=F32)
        k_ref[rs, :] = jnp.dot(h, w_ref[:, 4 * 512:5 * 512], preferred_element_type=F32).astype(k_ref.dtype)
        qt = lax.dot_general(wqv_ref[0], h, nt, preferred_element_type=F32)
        q_ref[:, rs] = (qt * (DA_QD ** -0.5 * LOG2E)).astype(q_ref.dtype)
        v_ref[:, rs] = lax.dot_general(wqv_ref[1], h, nt, preferred_element_type=F32).astype(v_ref.dtype)
        vm_ref[:, rs] = lax.dot_general(wqv_ref[2], h, nt, preferred_element_type=F32).astype(vm_ref.dtype)
        conv_rows(a)
    ext_ref[0:SUBLANES, :] = ext_ref[TM_PROJ:TM_PROJ + SUBLANES, :]


def _inproj(x2, g1, w_in, wqv_t, conv_w, conv_b, seq):
    n = x2.shape[0]
    assert seq % TM_PROJ == 0
    row = lambda i: (i, 0)
    fixed = lambda i: (0, 0)
    sds = lambda dt: jax.ShapeDtypeStruct((n, 512), dt)
    sds_t = jax.ShapeDtypeStruct((512, n), BF16)
    blk = pl.BlockSpec((TM_PROJ, 512), row)
    blk_t = pl.BlockSpec((512, TM_PROJ), lambda i: (0, i))
    return pl.pallas_call(
        functools.partial(_inproj_kernel, seq // TM_PROJ),
        grid=(n // TM_PROJ,),
        in_specs=[pl.BlockSpec((TM_PROJ, D_MODEL), row),
                  pl.BlockSpec((1, D_MODEL), fixed),
                  pl.BlockSpec((D_MODEL, 6 * 512), fixed),
                  pl.BlockSpec((3, 512, D_MODEL), lambda i: (0, 0, 0)),
                  pl.BlockSpec((CONV_W, D_MLSTM), fixed),
                  pl.BlockSpec((1, D_MLSTM), fixed)],
        out_specs=[blk, blk_t, blk, blk_t, blk, blk_t],
        out_shape=[sds(F32), sds_t, sds(F32), sds_t, sds(BF16), sds_t],
        scratch_shapes=[pltpu.VMEM((SUBLANES + TM_PROJ, D_MLSTM), F32)],
        compiler_params=_cparams(("arbitrary",)),
        name="inproj",
    )(x2, g1, w_in, wqv_t, conv_w, conv_b)


def _mlstm_kernel(*refs):
    c_ref = refs[0]
    vt_refs = refs[1:1 + MLSTM_BATCH]
    (z_ref, wq_ref, wk_ref, wg_ref, bg_ref,
     ng_ref, sk_ref, tril_ref, y_ref, st_ref, m_ref) = refs[1 + MLSTM_BATCH:]
    j = pl.program_id(1)

    @pl.when(j == 0)
    def _():
        st_ref[...] = jnp.zeros(st_ref.shape, F32)
        m_ref[...] = jnp.zeros(m_ref.shape, F32)

    seqs = range(MLSTM_BATCH)
    proj = [_mlstm_qk(c_ref.at[bb], vt_refs[bb], wq_ref, wk_ref) for bb in seqs]
    for bb in seqs:
        _mlstm_gate_logits(proj[bb], wg_ref, bg_ref)
    for bb in seqs:
        _mlstm_cumulate(proj[bb], tril_ref)
    pairs = [(h, bb) for h in range(ML_HEADS) for bb in seqs]
    prods = [_mlstm_head_products(h, proj[bb], st_ref.at[bb]) for h, bb in pairs]
    for (h, bb), hp in zip(pairs, prods):
        _mlstm_head(h, proj[bb], hp, z_ref.at[bb], ng_ref, sk_ref, y_ref.at[bb], st_ref.at[bb], m_ref.at[bb])


_NT = (((1,), (1,)), ((), ()))


def _mlstm_qk(c_ref, vm_ref, wq_ref, wk_ref):
    c_act = c_ref[...]
    vt = vm_ref[...]
    qts, ks = [], []
    for h in range(ML_HEADS):
        ch = c_act[:, h * ML_HD:(h + 1) * ML_HD].astype(BF16)
        qt = lax.dot_general(wq_ref[h], ch, _NT, preferred_element_type=F32)
        qts.append(qt.astype(BF16))
        ks.append(jnp.dot(ch, wk_ref[h], preferred_element_type=F32))
    return dict(c_act=c_act, vt=vt, qts=qts, ks=ks)


def _mlstm_gate_logits(p, wg_ref, bg_ref):
    gates_t = jnp.dot(wg_ref[:, 2 * D_MLSTM:3 * D_MLSTM], p["vt"], preferred_element_type=F32) + bg_ref[...]
    for h in range(ML_HEADS):
        hs = slice(h * ML_HD, (h + 1) * ML_HD)
        gates_t = gates_t + jnp.dot(wg_ref[:, hs], p["qts"][h], preferred_element_type=F32)
        gates_t = gates_t + lax.dot_general(wg_ref[:, D_MLSTM + h * ML_HD:D_MLSTM + (h + 1) * ML_HD],
                                            p["ks"][h].astype(BF16), _NT, preferred_element_type=F32)
    p["gates_t"] = gates_t


def _mlstm_cumulate(p, tril_ref):
    L = CHUNK
    gates_t = p["gates_t"]
    logf_t = jnp.minimum(gates_t, 0.0) - jnp.log(1.0 + jnp.exp(-jnp.abs(gates_t)))
    bcum_t = jnp.dot(logf_t, tril_ref[...], preferred_element_type=F32, precision=lax.Precision.HIGHEST)
    a8 = gates_t[0:SUBLANES] - bcum_t[SUBLANES:2 * SUBLANES]
    p["bcum_t"] = bcum_t
    p["a_cols"] = jnp.concatenate([a8, jnp.zeros((LANES - SUBLANES, L), F32)], axis=0).T


def _mlstm_head_products(h, p, st_ref):
    qt = p["qts"][h]
    k = (p["ks"][h] * (ML_HD ** -0.5)).astype(BF16)
    state_t = st_ref[h]
    s_t = jnp.dot(k, qt, preferred_element_type=F32)
    r_inter = jnp.dot(state_t.astype(BF16), qt, preferred_element_type=F32)
    return dict(k=k, state_t=state_t, s_t=s_t, r_inter=r_inter)


def _mlstm_head(h, p, hp, z_ref, ng_ref, sk_ref, y_ref, st_ref, m_ref):
    L = CHUNK
    c_act, vt, gates_t, bcum_t, a_cols = p["c_act"], p["vt"], p["gates_t"], p["bcum_t"], p["a_cols"]
    k, state_t, s_t, r_inter = hp["k"], hp["state_t"], hp["s_t"], hp["r_inter"]
    s_idx = lax.broadcasted_iota(jnp.int32, (L, L), 0)
    t_idx = lax.broadcasted_iota(jnp.int32, (L, L), 1)
    causal = s_idx <= t_idx
    ones_row = jnp.where(lax.broadcasted_iota(jnp.int32, (ML_HD, L), 0) == 0, 1.0, 0.0).astype(BF16)
    sl = slice(h * ML_HD, (h + 1) * ML_HD)
    vaug_t = jnp.concatenate([vt[sl, :], ones_row], axis=0)
    i_row = gates_t[h:h + 1, :]
    b_row = bcum_t[SUBLANES + h:SUBLANES + h + 1, :]
    a_row = i_row - b_row
    m_prev = m_ref[h][0:1, 0:1]

    e = jnp.where(causal, a_cols[:, h:h + 1], NEG)
    cm = jnp.max(e, axis=0, keepdims=True)
    mm = jnp.maximum(m_prev, cm)
    dexp = jnp.exp(e - mm)
    w_t = (dexp * s_t).astype(BF16)
    sp = jnp.exp(m_prev - mm)
    r_intra = jnp.dot(vaug_t, w_t, preferred_element_type=F32)
    tot = sp * r_inter + r_intra
    num = tot[:ML_HD]
    den = tot[ML_HD:ML_HD + 1]
    h_t = num / jnp.maximum(jnp.abs(den), jnp.exp(-(b_row + mm)))

    b_end = b_row[:, L - 1:L]
    mm_end = mm[:, L - 1:L]
    wk = jnp.exp(a_row - mm_end)
    decay = jnp.exp(m_prev - mm_end)
    vw = (vaug_t.astype(F32) * wk).astype(BF16)
    upd = jnp.dot(vw, k, preferred_element_type=F32)
    st_ref[h] = decay * state_t + upd
    m_ref[h] = jnp.broadcast_to(b_end + mm_end, (SUBLANES, LANES))

    mu = jnp.mean(h_t, axis=0, keepdims=True)
    xc = h_t - mu
    var = jnp.mean(xc * xc, axis=0, keepdims=True)
    hn = (xc * lax.rsqrt(var + EPS)).T * ng_ref[:, sl]
    hn = hn + sk_ref[:, sl] * c_act[:, sl]
    o = jax.nn.sigmoid(z_ref[:, sl])
    y_ref[:, sl] = (o * hn).astype(y_ref.dtype)


def _mlstm(c, vmt, z, wqt, wk, wgt, bgt, ng, sk, bsz, seq):
    nc = seq // CHUNK
    f2 = lambda b, j: (0, 0)
    f3 = lambda b, j: (0, 0, 0)
    triu = jnp.triu(jnp.ones((CHUNK, CHUNK), F32))
    blk = pl.BlockSpec((MLSTM_BATCH, CHUNK, D_MLSTM), lambda b, j: (b, j, 0))
    vt_specs = [pl.BlockSpec((D_MLSTM, CHUNK), functools.partial(lambda b, j, bb: (0, (b * MLSTM_BATCH + bb) * nc + j),
                                                                  bb=bb))
                for bb in range(MLSTM_BATCH)]
    c, z = (a.reshape(bsz, seq, D_MLSTM) for a in (c, z))
    y = pl.pallas_call(
        _mlstm_kernel,
        grid=(bsz // MLSTM_BATCH, nc),
        in_specs=[blk] + vt_specs + [
                  blk,
                  pl.BlockSpec((ML_HEADS, ML_HD, ML_HD), f3),
                  pl.BlockSpec((ML_HEADS, ML_HD, ML_HD), f3),
                  pl.BlockSpec((2 * SUBLANES, 3 * D_MLSTM), f2),
                  pl.BlockSpec((2 * SUBLANES, 1), f2),
                  pl.BlockSpec((1, D_MLSTM), f2),
                  pl.BlockSpec((1, D_MLSTM), f2),
                  pl.BlockSpec((CHUNK, CHUNK), f2)],
        out_specs=blk,
        out_shape=jax.ShapeDtypeStruct((bsz, seq, D_MLSTM), BF16),
        scratch_shapes=[pltpu.VMEM((MLSTM_BATCH, ML_HEADS, 2 * ML_HD, ML_HD), F32),
                        pltpu.VMEM((MLSTM_BATCH, ML_HEADS, SUBLANES, LANES), F32)],
        compiler_params=_cparams(("parallel", "arbitrary")),
        name="mlstm",
    )(c, *([vmt] * MLSTM_BATCH), z, wqt, wk, wgt, bgt, ng, sk, triu)
    return y.reshape(bsz * seq, D_MLSTM)


def _attn_kernel(lq_ref, qt_ref, k_ref, vt_ref, bias_ref, corner_ref, g_ref, o_ref,
                 acc_ref, m_ref, q2_ref, s_ref, p_ref, a_ref):
    tq = TQ
    i = pl.program_id(2)
    heads = range(ATTN_HEADS)
    row = lax.broadcasted_iota(jnp.int32, (LANES, tq), 0)
    for hh in heads:
        qt = qt_ref[hh * LANES:(hh + 1) * LANES, :]
        zero = jnp.zeros_like(qt)
        q2_ref[hh, :, 0:tq] = jnp.where(row < DA_QD, qt, zero)
        q2_ref[hh, :, tq:2 * tq] = jnp.where(row >= DA_QD, qt, zero)
    acc_ref[...] = jnp.zeros(acc_ref.shape, F32)
    m_ref[...] = jnp.full(m_ref.shape, NEG, F32)

    p_ref[:, 1] = jnp.zeros((ATTN_HEADS,) + p_ref.shape[2:], BF16)
    a_ref[:, 1] = jnp.ones((ATTN_HEADS,) + a_ref.shape[2:], F32)

    def logits(hh, kb, buf):
        start = pl.multiple_of(kb * tq, tq)
        s_ref[hh, buf] = jnp.dot(k_ref[pl.ds(start, tq), hh * LANES:(hh + 1) * LANES], q2_ref[hh],
                                 preferred_element_type=F32)

    ones_rows = jnp.ones((ACC_ROWS - DA_VD, tq), BF16)

    def values(hh, kb, buf):
        start = pl.multiple_of(jnp.maximum(kb, 0) * tq, tq)
        lhs = jnp.concatenate([vt_ref[hh * LANES:(hh + 1) * LANES, pl.ds(start, tq)], ones_rows], axis=0)
        pv = jnp.dot(lhs, p_ref[hh, buf], preferred_element_type=F32)
        acc_ref[hh] = a_ref[hh, buf] * acc_ref[hh] + pv

    def softmax(hh, buf, which_bias):
        if which_bias == 1:
            ks = slice(tq - REL_MAX_DIST, tq)
            corner = corner_ref[hh]
            for c0 in (0, tq):
                s_ref[hh, buf, ks, c0:c0 + REL_MAX_DIST] = s_ref[hh, buf, ks, c0:c0 + REL_MAX_DIST] + corner
        s = s_ref[hh, buf]
        if which_bias == 0:
            bias = bias_ref[hh]
            s = s + jnp.concatenate([bias, bias], axis=1)
        m_old = m_ref[hh]
        m_new = jnp.maximum(m_old, jnp.max(s, axis=0, keepdims=True))
        m_ref[hh] = m_new
        a_ref[hh, buf] = jnp.exp2(m_old - m_new)
        p_ref[hh, buf] = jnp.exp2(s - m_new).astype(BF16)

    def all_logits(kb, buf):
        for hh in heads:
            logits(hh, kb, buf)

    def all_values(kb, buf):
        for hh in heads:
            values(hh, kb, buf)

    def stage(kb, buf, has_next, which_bias):
        for hh in heads:
            softmax(hh, buf, which_bias)
        if has_next:
            all_logits(kb + 1, 1 - buf)
        all_values(kb - 1, 1 - buf)

    all_logits(0, 0)
    n_far = jnp.maximum(i - 1, 0)

    def far_pair(t, carry):
        stage(2 * t, 0, True, None)
        stage(2 * t + 1, 1, True, None)
        return carry

    lax.fori_loop(0, n_far // 2, far_pair, 0)
    kb0 = 2 * (n_far // 2)

    @pl.when(i == 0)
    def _():
        stage(0, 0, False, 0)
        all_values(0, 0)

    @pl.when(jnp.logical_and(i >= 1, n_far % 2 == 0))
    def _():
        stage(kb0, 0, True, 1)
        stage(kb0 + 1, 1, False, 0)
        all_values(kb0 + 1, 1)

    @pl.when(n_far % 2 == 1)
    def _():
        stage(kb0, 0, True, None)
        stage(kb0 + 1, 1, True, 1)
        stage(kb0 + 2, 0, False, 0)
        all_values(kb0 + 2, 0)

    lq = lq_ref[...]
    d01 = jnp.sum(lq[0:1, :] * lq[1:2, :], axis=-1, keepdims=True)
    d23 = jnp.sum(lq[2:3, :] * lq[3:4, :], axis=-1, keepdims=True)
    lam = jnp.exp(d01) - jnp.exp(d23) + LAM_INIT
    for hh in heads:
        o = acc_ref[hh, 0:DA_VD, :] * (1.0 / acc_ref[hh, DA_VD:DA_VD + 1, :])
        y = o[:, :tq] - lam * o[:, tq:]
        y = y * lax.rsqrt(jnp.mean(y * y, axis=0, keepdims=True) + SUBLN_EPS) * g_ref[...]
        o_ref[:, hh * LANES:(hh + 1) * LANES] = (y * (1.0 - LAM_INIT)).T.astype(o_ref.dtype)


def _rel_bias_tiles(rel_bias):
    L = TQ
    max_exact = REL_BUCKETS // 2
    n = jnp.arange(-(L - 1), 2 * L, dtype=jnp.int32)
    nn = jnp.maximum(n, 0)
    large = max_exact + (jnp.log(jnp.maximum(nn, 1).astype(F32) / max_exact)
                         / math.log(REL_MAX_DIST / max_exact) * (REL_BUCKETS - max_exact)).astype(jnp.int32)
    large = jnp.minimum(large, REL_BUCKETS - 1)
    bucket = jnp.where(nn < max_exact, nn, large)
    rb = (rel_bias.astype(F32) - rel_bias[REL_BUCKETS - 1].astype(F32)[None, :]).T
    t = jnp.where(n[None, :] >= 0, rb[:, bucket] * LOG2E, NEG)

    def toeplitz(v, m):
        vp = jnp.pad(v, ((0, 0), (0, 1)))
        flat = jnp.tile(vp, (1, m))[:, :m * (2 * m - 1)]
        return flat.reshape(v.shape[0], m, 2 * m - 1)[:, :, m - 1:]

    c = REL_MAX_DIST
    return toeplitz(t[:, :2 * L - 1], L), toeplitz(t[:, L:L + 2 * c - 1], c)


def _attn(lq, qt, kd, vt, bias, corner, g, bsz, seq):
    nq = seq // TQ
    return pl.pallas_call(
        _attn_kernel,
        grid=(bsz, DA_HEADS // ATTN_HEADS, nq),
        in_specs=[pl.BlockSpec((4, DA_QD), lambda b, h, i: (0, 0)),
                  pl.BlockSpec((ATTN_HEADS * LANES, TQ), lambda b, h, i: (h, b * nq + i)),
                  pl.BlockSpec((seq, ATTN_HEADS * LANES), lambda b, h, i: (b, h)),
                  pl.BlockSpec((ATTN_HEADS * LANES, seq), lambda b, h, i: (h, b)),
                  pl.BlockSpec((ATTN_HEADS, TQ, TQ), lambda b, h, i: (h, 0, 0)),
                  pl.BlockSpec((ATTN_HEADS, REL_MAX_DIST, REL_MAX_DIST), lambda b, h, i: (h, 0, 0)),
                  pl.BlockSpec((DA_VD, 1), lambda b, h, i: (0, 0))],
        out_specs=pl.BlockSpec((TQ, ATTN_HEADS * LANES), lambda b, h, i: (b * nq + i, h)),
        out_shape=jax.ShapeDtypeStruct((bsz * seq, D_DIFF), BF16),
        scratch_shapes=[pltpu.VMEM((ATTN_HEADS, ACC_ROWS, 2 * TQ), F32),
                        pltpu.VMEM((ATTN_HEADS, 1, 2 * TQ), F32),
                        pltpu.VMEM((ATTN_HEADS, LANES, 2 * TQ), BF16),
                        pltpu.VMEM((ATTN_HEADS, 2, TQ, 2 * TQ), F32),
                        pltpu.VMEM((ATTN_HEADS, 2, TQ, 2 * TQ), BF16),
                        pltpu.VMEM((ATTN_HEADS, 2, 1, 2 * TQ), F32)],
        compiler_params=_cparams(("parallel", "parallel", "arbitrary")),
        name="attn",
    )(lq, qt, kd, vt, bias, corner, g)


def _outproj_kernel(x_ref, ym_ref, yd_ref, wo_ref, g2_ref, wr_ref, br_ref, tri_ref,
                    x1_ref, h2_ref, meta_ref, gates_ref, cnt_ref, run_ref):
    step = pl.program_id(0)

    @pl.when(step == 0)
    def _():
        run_ref[...] = jnp.zeros(run_ref.shape, F32)

    tm = TM_PROJ // OUTPROJ_PARTS
    parts = [slice(a * tm, (a + 1) * tm) for a in range(OUTPROJ_PARTS)]
    x1s = [_outproj_residual(rs, x_ref, ym_ref, yd_ref, wo_ref, x1_ref) for rs in parts]
    lts = [_outproj_logits(rs, x1, g2_ref, wr_ref, br_ref, h2_ref) for rs, x1 in zip(parts, x1s)]
    for rs, lt in zip(parts, lts):
        _outproj_route(rs, lt, tri_ref, meta_ref, gates_ref, cnt_ref, run_ref)


def _outproj_residual(rs, x_ref, ym_ref, yd_ref, wo_ref, x1_ref):
    x1 = (x_ref[rs, :]
          + jnp.dot(ym_ref[rs, :], wo_ref[0:D_MLSTM, :], preferred_element_type=F32)
          + jnp.dot(yd_ref[rs, :], wo_ref[D_MLSTM:, :], preferred_element_type=F32))
    x1_ref[rs, :] = x1
    return x1


def _outproj_logits(rs, x1, g2_ref, wr_ref, br_ref, h2_ref):
    ms = jnp.mean(x1 * x1, axis=-1, keepdims=True)
    h2 = x1 * lax.rsqrt(ms + EPS) * g2_ref[...]
    h2_ref[rs] = _to_token_major(h2)
    h_hi = h2.astype(BF16)
    h_lo = (h2 - h_hi.astype(F32)).astype(BF16)
    lhs = jnp.concatenate([h_hi, h_lo, h_hi], axis=1)
    logits = jnp.dot(lhs, wr_ref[...], preferred_element_type=F32) + br_ref[...]
    return logits.T


def _outproj_route(rs, lt, tri_ref, meta_ref, gates_ref, cnt_ref, run_ref):
    tm = lt.shape[1]
    row8 = lax.broadcasted_iota(jnp.int32, (SUBLANES, tm), 0)
    lg = jnp.where(row8 < N_GROUPS, lt[0:SUBLANES], NEG)
    gmax = jnp.max(lg, axis=0, keepdims=True)
    gidx = jnp.min(jnp.where(lg == gmax, row8, SUBLANES), axis=0, keepdims=True)
    pg_sel = 1.0 / jnp.sum(jnp.exp(lg - gmax), axis=0, keepdims=True)
    le = jnp.zeros((EXP_PER_GROUP, tm), F32)
    for g in range(N_GROUPS):
        le = jnp.where(gidx == g, lt[SUBLANES * (g + 1):SUBLANES * (g + 2)], le)
    v1 = jnp.max(le, axis=0, keepdims=True)
    i1 = jnp.min(jnp.where(le == v1, row8, SUBLANES), axis=0, keepdims=True)
    le2 = jnp.where(row8 == i1, NEG, le)
    v2 = jnp.max(le2, axis=0, keepdims=True)
    i2 = jnp.min(jnp.where(le2 == v2, row8, SUBLANES), axis=0, keepdims=True)
    e21 = jnp.exp(v2 - v1)
    pw1 = 1.0 / (1.0 + e21)
    gate1 = pg_sel * pw1
    gate2 = pg_sel * (e21 * pw1)
    eid1 = gidx * EXP_PER_GROUP + i1
    eid2 = gidx * EXP_PER_GROUP + i2

    row32 = lax.broadcasted_iota(jnp.int32, (N_EXPERTS, tm), 0)
    oh1 = row32 == eid1
    oh2 = row32 == eid2
    oh = jnp.where(oh1, 1.0, 0.0) + jnp.where(oh2, 1.0, 0.0)
    before = jnp.dot(oh.astype(BF16), tri_ref[...], preferred_element_type=F32) + run_ref[:, 0:1]
    rank1 = jnp.sum(jnp.where(oh1, before, 0.0), axis=0, keepdims=True).astype(jnp.int32)
    rank2 = jnp.sum(jnp.where(oh2, before, 0.0), axis=0, keepdims=True).astype(jnp.int32)
    run = run_ref[...] + jnp.sum(oh, axis=1, keepdims=True)
    run_ref[...] = run
    cnt_ref[...] = run.astype(jnp.int32)

    meta_ref[:, rs] = jnp.where(row8 == 0, eid1, jnp.where(row8 == 1, eid2,
                                jnp.where(row8 == 2, rank1, jnp.where(row8 == 3, rank2, 0))))
    row128 = lax.broadcasted_iota(jnp.int32, (LANES, tm), 0)
    gt = jnp.where(row128 == 0, gate1, jnp.where(row128 == 1, gate2, 0.0))
    gates_ref[rs, :] = gt.T


def _outproj(x2, ym, yd, wo, g2, wr, br):
    n = x2.shape[0]
    row = lambda i: (i, 0)
    fixed = lambda i: (0, 0)
    part = TM_PROJ // OUTPROJ_PARTS
    tri = jnp.triu(jnp.ones((part, part), F32), k=1).astype(BF16)
    return pl.pallas_call(
        _outproj_kernel,
        grid=(n // TM_PROJ,),
        in_specs=[pl.BlockSpec((TM_PROJ, D_MODEL), row),
                  pl.BlockSpec((TM_PROJ, D_MLSTM), row),
                  pl.BlockSpec((TM_PROJ, D_DIFF), row),
                  pl.BlockSpec((D_MODEL, D_MODEL), fixed),
                  pl.BlockSpec((1, D_MODEL), fixed),
                  pl.BlockSpec((3 * D_MODEL, LANES), fixed),
                  pl.BlockSpec((1, LANES), fixed),
                  pl.BlockSpec((part, part), fixed)],
        out_specs=[pl.BlockSpec((TM_PROJ, D_MODEL), row),
                   pl.BlockSpec((TM_PROJ,) + ROW_TILE, lambda i: (i, 0, 0)),
                   pl.BlockSpec((SUBLANES, TM_PROJ), lambda i: (0, i)),
                   pl.BlockSpec((TM_PROJ, LANES), row),
                   pl.BlockSpec((N_EXPERTS, LANES), fixed)],
        out_shape=[jax.ShapeDtypeStruct((n, D_MODEL), F32),
                   jax.ShapeDtypeStruct((n,) + ROW_TILE, F32),
                   jax.ShapeDtypeStruct((SUBLANES, n), jnp.int32),
                   jax.ShapeDtypeStruct((n, LANES), F32),
                   jax.ShapeDtypeStruct((N_EXPERTS, LANES), jnp.int32)],
        scratch_shapes=[pltpu.VMEM((N_EXPERTS, LANES), F32)],
        compiler_params=_cparams(("arbitrary",)),
        name="outproj_router",
    )(x2, ym, yd, wo, g2, wr, br, tri)


def _row_copy(src_ref, src_row, dst_ref, dst_row, sem):
    return pltpu.make_async_copy(src_ref.at[pl.ds(src_row, 1)], dst_ref.at[pl.ds(dst_row, 1)], sem)


def _dispatch_kernel(ends_ref, pos_ref, h_ref, xs_ref, zero_ref, hbuf_ref, sem, bsem, zsem):
    step = pl.program_id(0)

    @pl.when(step == 0)
    def _():
        zero_ref[...] = jnp.zeros(zero_ref.shape, F32)

        def tile_copy(e):
            start = pl.multiple_of(ends_ref[e + 1] - TM_MOE, TM_MOE)
            return pltpu.make_async_copy(zero_ref, xs_ref.at[pl.ds(start, TM_MOE)], zsem)

        def zstart(e, carry):
            @pl.when(ends_ref[e + 1] > ends_ref[e])
            def _():
                tile_copy(e).start()
            return carry

        def zwait(e, carry):
            @pl.when(ends_ref[e + 1] > ends_ref[e])
            def _():
                tile_copy(e).wait()
            return carry

        lax.fori_loop(0, N_EXPERTS, zstart, 0)
        lax.fori_loop(0, N_EXPERTS, zwait, 0)

        def tail_copy(t):
            return pltpu.make_async_copy(zero_ref, xs_ref.at[pl.ds(pl.multiple_of(t * TM_MOE, TM_MOE), TM_MOE)], zsem)

        def tstart(t, carry):
            tail_copy(t).start()
            return carry

        def twait(t, carry):
            tail_copy(t).wait()
            return carry

        first_tail = ends_ref[N_EXPERTS] // TM_MOE
        lax.fori_loop(first_tail, xs_ref.shape[0] // TM_MOE, tstart, 0)
        lax.fori_loop(first_tail, xs_ref.shape[0] // TM_MOE, twait, 0)

    last = pl.num_programs(0) - 1
    slot = step % 3
    par = step % 2

    def block_copy(i, s):
        return pltpu.make_async_copy(h_ref.at[pl.ds(pl.multiple_of(i * TT, TT), TT)], hbuf_ref.at[s], bsem.at[s])

    @pl.when(step == 0)
    def _():
        block_copy(0, 0).start()

    @pl.when(step < last)
    def _():
        block_copy(step + 1, (step + 1) % 3).start()

    block_copy(step, slot).wait()
    src = hbuf_ref.at[slot]

    def issue(r, carry):
        _row_copy(src, r, xs_ref, pos_ref[0, r], sem.at[par]).start(priority=0)
        _row_copy(src, r, xs_ref, pos_ref[0, TT + r], sem.at[par]).start(priority=1)
        return carry

    def drain(s):
        for _ in range(2):
            pltpu.make_async_copy(src, xs_ref.at[pl.ds(0, TT)], sem.at[s]).wait()

    lax.fori_loop(0, TT, issue, 0)

    @pl.when(step > 0)
    def _():
        drain(1 - par)

    @pl.when(step == last)
    def _():
        drain(par)


def _dispatch(ends, pos, h2, n_rows):
    n = h2.shape[0]
    return pl.pallas_call(
        _dispatch_kernel,
        grid_spec=pltpu.PrefetchScalarGridSpec(
            num_scalar_prefetch=1,
            grid=(n // TT,),
            in_specs=[pl.BlockSpec((None, 1, 2 * TT), lambda i, ends: (i, 0, 0), memory_space=pltpu.SMEM),
                      pl.BlockSpec(memory_space=pl.ANY)],
            out_specs=pl.BlockSpec(memory_space=pl.ANY),
            scratch_shapes=[pltpu.VMEM((TM_MOE,) + ROW_TILE, F32),
                            pltpu.VMEM((3, TT) + ROW_TILE, F32),
                            pltpu.SemaphoreType.DMA((2,)),
                            pltpu.SemaphoreType.DMA((3,)),
                            pltpu.SemaphoreType.DMA(())]),
        out_shape=jax.ShapeDtypeStruct((n_rows,) + ROW_TILE, F32),
        compiler_params=_cparams(("arbitrary",)),
        name="dispatch",
    )(ends, pos, h2)


def _moe_kernel(texp_ref, nact_ref, x_ref, wg32_ref, wu32_ref, wd32_ref, y_ref, wg_ref, wu_ref, wd_ref):
    t = pl.program_id(0)

    @pl.when(jnp.logical_or(t == 0, texp_ref[t] != texp_ref[jnp.maximum(t - 1, 0)]))
    def _():
        wg_ref[...] = wg32_ref[...].astype(BF16)
        wu_ref[...] = wu32_ref[...].astype(BF16)
        wd_ref[...] = wd32_ref[...].astype(BF16)

    @pl.when(t < nact_ref[0])
    def _():
        half = TM_MOE // MOE_SUBTILES
        hids = []
        for a in range(MOE_SUBTILES):
            x = _from_token_major(x_ref[a * half:(a + 1) * half]).astype(BF16)
            g = jnp.dot(x, wg_ref[...], preferred_element_type=F32)
            u = jnp.dot(x, wu_ref[...], preferred_element_type=F32)
            hids.append((g * jax.nn.sigmoid(g) * u).astype(BF16))
        for a in range(MOE_SUBTILES):
            y = jnp.dot(hids[a], wd_ref[...], preferred_element_type=F32)
            y_ref[a * half:(a + 1) * half] = _to_token_major(y)

    @pl.when(t >= nact_ref[0])
    def _():
        y_ref[...] = jnp.zeros(y_ref.shape, F32)


def _moe(tile_expert, n_active, xs, w_eg, w_eu, w_ed):
    n_tiles = xs.shape[0] // TM_MOE
    tile = lambda t, te, na: (jnp.minimum(t, na[0] - 1), 0, 0)
    out_tile = lambda t, te, na: (t, 0, 0)
    wsel = lambda t, te, na: (te[t], 0, 0)
    return pl.pallas_call(
        _moe_kernel,
        grid_spec=pltpu.PrefetchScalarGridSpec(
            num_scalar_prefetch=2,
            grid=(n_tiles,),
            in_specs=[pl.BlockSpec((TM_MOE,) + ROW_TILE, tile),
                      pl.BlockSpec((None, D_MODEL, D_FF_EXP), wsel),
                      pl.BlockSpec((None, D_MODEL, D_FF_EXP), wsel),
                      pl.BlockSpec((None, D_FF_EXP, D_MODEL), wsel)],
            out_specs=pl.BlockSpec((TM_MOE,) + ROW_TILE, out_tile),
            scratch_shapes=[pltpu.VMEM((D_MODEL, D_FF_EXP), BF16),
                            pltpu.VMEM((D_MODEL, D_FF_EXP), BF16),
                            pltpu.VMEM((D_FF_EXP, D_MODEL), BF16)]),
        out_shape=jax.ShapeDtypeStruct(xs.shape, F32),
        compiler_params=_cparams(("arbitrary",)),
        name="moe",
    )(tile_expert, n_active, xs, w_eg, w_eu, w_ed)


def _combine_kernel(pos_ref, posn_ref, x1_ref, gates_ref, gf_ref, ys_ref, o_ref, buf_ref, sem):
    step = pl.program_id(0)
    slot = step % 2

    def gather(p_ref, s):
        def body(r, carry):
            _row_copy(ys_ref, p_ref[0, r], buf_ref.at[s, 0], r, sem.at[s]).start(priority=0)
            _row_copy(ys_ref, p_ref[0, TT + r], buf_ref.at[s, 1], r, sem.at[s]).start(priority=1)
            return carry
        lax.fori_loop(0, TT, body, 0)

    @pl.when(step == 0)
    def _():
        gather(pos_ref, 0)

    @pl.when(step < pl.num_programs(0) - 1)
    def _():
        gather(posn_ref, 1 - slot)

    for k in range(2):
        pltpu.make_async_copy(ys_ref.at[pl.ds(0, TT)], buf_ref.at[slot, k], sem.at[slot]).wait()
    gates = gates_ref[...]
    y = (x1_ref[...] + gates[:, 0:1] * _from_token_major(buf_ref[slot, 0])
         + gates[:, 1:2] * _from_token_major(buf_ref[slot, 1]))
    ms = jnp.mean(y * y, axis=-1, keepdims=True)
    o_ref[...] = y * lax.rsqrt(ms + EPS) * gf_ref[...]


def _combine(pos, x1, gates, gf, ys):
    n = x1.shape[0]
    nblk = n // TT
    return pl.pallas_call(
        _combine_kernel,
        grid=(nblk,),
        in_specs=[pl.BlockSpec((None, 1, 2 * TT), lambda i: (i, 0, 0), memory_space=pltpu.SMEM),
                  pl.BlockSpec((None, 1, 2 * TT), lambda i: (jnp.minimum(i + 1, nblk - 1), 0, 0),
                               memory_space=pltpu.SMEM),
                  pl.BlockSpec((TT, D_MODEL), lambda i: (i, 0)),
                  pl.BlockSpec((TT, LANES), lambda i: (i, 0)),
                  pl.BlockSpec((1, D_MODEL), lambda i: (0, 0)),
                  pl.BlockSpec(memory_space=pl.ANY)],
        out_specs=pl.BlockSpec((TT, D_MODEL), lambda i: (i, 0)),
        out_shape=jax.ShapeDtypeStruct((n, D_MODEL), F32),
        scratch_shapes=[pltpu.VMEM((2, 2, TT) + ROW_TILE, F32),
                        pltpu.SemaphoreType.DMA((2,))],
        compiler_params=_cparams(("arbitrary",)),
        name="combine",
    )(pos, pos, x1, gates, gf, ys)


def kernel(x, w_in, conv_w, conv_b, w_mq, w_mk, w_mgate, b_mgate, m_norm_g, m_skip, lambda_qk, da_norm_g,
           rel_bias, w_out, norm1_g, norm2_g, w_rg, b_rg, w_re, b_re, w_eg, w_eu, w_ed, normf_g):
    bsz, seq, _ = x.shape
    n = bsz * seq
    assert seq % TQ == 0 and seq % CHUNK == 0 and n % TM_PROJ == 0 and n % TT == 0 and bsz % MLSTM_BATCH == 0
    l = 0
    x2 = x.reshape(n, D_MODEL)

    w_in_b = w_in[l].astype(BF16)
    wqv_t = jnp.stack([w_in_b[:, 3 * 512:4 * 512].T, w_in_b[:, 5 * 512:6 * 512].T,
                       w_in_b[:, 1 * 512:2 * 512].T])
    c, vmt, z, qt, kd, vt = _inproj(x2, norm1_g[l][None, :], w_in_b, wqv_t, conv_w[l], conv_b[l][None, :], seq)

    wgt = jnp.zeros((2 * SUBLANES, 3 * D_MLSTM), F32)
    wgt = wgt.at[0:ML_HEADS].set(w_mgate[l][:, :ML_HEADS].T)
    wgt = wgt.at[SUBLANES:SUBLANES + ML_HEADS].set(w_mgate[l][:, ML_HEADS:].T).astype(BF16)
    bgt = jnp.zeros((2 * SUBLANES, 1), F32)
    bgt = bgt.at[0:ML_HEADS, 0].set(b_mgate[l][:ML_HEADS])
    bgt = bgt.at[SUBLANES:SUBLANES + ML_HEADS, 0].set(b_mgate[l][ML_HEADS:])
    y_m = _mlstm(c, vmt, z, jnp.swapaxes(w_mq[l], 1, 2).astype(BF16),
                 w_mk[l].astype(BF16), wgt, bgt, m_norm_g[l][None, :], m_skip[l][None, :], bsz, seq)

    bias_diag, bias_corner = _rel_bias_tiles(rel_bias)
    y_d = _attn(lambda_qk[l].astype(F32), qt, kd, vt, bias_diag, bias_corner,
                da_norm_g[l][:, None], bsz, seq)

    wr = jnp.zeros((D_MODEL, LANES), F32)
    wr = wr.at[:, :N_GROUPS].set(w_rg[l])
    wr = wr.at[:, SUBLANES:SUBLANES + N_EXPERTS].set(w_re[l].reshape(D_MODEL, N_EXPERTS))
    wr_hi = wr.astype(BF16)
    wr_lo = (wr - wr_hi.astype(F32)).astype(BF16)
    wr3 = jnp.concatenate([wr_hi, wr_hi, wr_lo], axis=0)
    br = jnp.zeros((1, LANES), F32)
    br = br.at[0, :N_GROUPS].set(b_rg[l])
    br = br.at[0, SUBLANES:SUBLANES + N_EXPERTS].set(b_re[l].reshape(N_EXPERTS))
    x1, h2, meta, gates, counts = _outproj(x2, y_m, y_d, w_out[l].astype(BF16), norm2_g[l][None, :], wr3, br)

    cnt = counts[:, 0]
    tiles_e = (cnt + TM_MOE - 1) // TM_MOE
    ends_t = jnp.cumsum(tiles_e)
    ends = jnp.concatenate([jnp.zeros((1,), jnp.int32), ends_t * TM_MOE]).astype(jnp.int32)
    offs = ends[:-1]
    n_tiles = (2 * n) // TM_MOE + N_EXPERTS
    n_active = ends_t[-1:].astype(jnp.int32)
    tile_ids = jnp.minimum(jnp.arange(n_tiles, dtype=jnp.int32), n_active[0] - 1)
    tile_expert = jnp.sum(tile_ids[:, None] >= ends_t[None, :], axis=1).astype(jnp.int32)
    eids = jnp.arange(N_EXPERTS, dtype=jnp.int32)[:, None]
    pos1 = jnp.sum(jnp.where(meta[0][None, :] == eids, offs[:, None], 0), axis=0) + meta[2]
    pos2 = jnp.sum(jnp.where(meta[1][None, :] == eids, offs[:, None], 0), axis=0) + meta[3]
    pos = jnp.concatenate([pos1.reshape(n // TT, 1, TT), pos2.reshape(n // TT, 1, TT)], axis=2)

    xs = _dispatch(ends, pos, h2, n_tiles * TM_MOE)
    ys = _moe(tile_expert, n_active, xs, w_eg[l], w_eu[l], w_ed[l])
    out = _combine(pos, x1, gates, normf_g[None, :], ys)
    return out.reshape(bsz, seq, D_MODEL)
```

```python
import functools
import math

import jax
import jax.numpy as jnp
from jax import lax
from jax.experimental import pallas as pl
from jax.experimental.pallas import tpu as pltpu

F32 = jnp.float32
BF16 = jnp.bfloat16

D_MODEL = 1024
D_MLSTM = 512
D_DIFF = 512
ML_HEADS = 4
ML_HD = 128
CONV_W = 4
CHUNK = 128
DA_HEADS = 4
DA_VD = 128
DA_QD = 64
REL_BUCKETS = 32
REL_MAX_DIST = 128
N_GROUPS = 4
EXP_PER_GROUP = 8
N_EXPERTS = 32
D_FF_EXP = 512
EPS = 1e-6
SUBLN_EPS = 1e-5
LAM_INIT = 0.8 - 0.6 * math.exp(-0.3 * 0)
NEG = -1e30
LOG2E = math.log2(math.e)

LANES = 128
SUBLANES = 8

TM_PROJ = 512
INPROJ_PARTS = 2
OUTPROJ_PARTS = 2
TQ = 512
ACC_ROWS = DA_VD + 16
ATTN_HEADS = 2
MLSTM_BATCH = 4
TM_MOE = 512
MOE_SUBTILES = 2
MOE_XSLOTS = 3
TT = 512
VMEM_LIMIT = 48 * 1024 * 1024


ROW_TILE = (D_MODEL // LANES, LANES)


def _cparams(sem):
    return pltpu.CompilerParams(dimension_semantics=sem, vmem_limit_bytes=VMEM_LIMIT)


def _to_token_major(x):
    return pltpu.einshape("a(bc)->abc", x, b=ROW_TILE[0])


def _from_token_major(x):
    return pltpu.einshape("abc->a(bc)", x)


def _inproj_kernel(tiles_per_seq, x_ref, g_ref, w_ref, wqv_ref, cw_ref, cb_ref,
                   c_ref, vm_ref, z_ref, q_ref, k_ref, v_ref, ext_ref):
    @pl.when(pl.program_id(0) % tiles_per_seq == 0)
    def _():
        ext_ref[0:SUBLANES, :] = jnp.zeros((SUBLANES, D_MLSTM), F32)

    tm = TM_PROJ // INPROJ_PARTS
    nt = (((1,), (1,)), ((), ()))
    cw = cw_ref[...]

    def normed(a):
        x = x_ref[a * tm:(a + 1) * tm, :]
        ms = jnp.mean(x * x, axis=-1, keepdims=True)
        return (x * lax.rsqrt(ms + EPS) * g_ref[...]).astype(BF16)

    def conv_rows(a):
        r0 = SUBLANES + a * tm
        conv = cb_ref[...] + cw[CONV_W - 1:CONV_W, :] * ext_ref[r0:r0 + tm, :]
        for k in range(1, CONV_W):
            conv = conv + cw[CONV_W - 1 - k:CONV_W - k, :] * ext_ref[r0 - k:r0 - k + tm, :]
        c_ref[a * tm:(a + 1) * tm, :] = conv * jax.nn.sigmoid(conv)

    hs = []
    for a in range(INPROJ_PARTS):
        hs.append(normed(a))
        ext_ref[SUBLANES + a * tm:SUBLANES + (a + 1) * tm, :] = jnp.dot(hs[a], w_ref[:, 0:512],
                                                                          preferred_element_type=F32)
    for a in range(INPROJ_PARTS):
        h = hs[a]
        rs = slice(a * tm, (a + 1) * tm)
        z_ref[rs, :] = jnp.dot(h, w_ref[:, 2 * 512:3 * 512], preferred_element_type=F32)
        k_ref[rs, :] = jnp.dot(h, w_ref[:, 4 * 512:5 * 512], preferred_element_type=F32).astype(k_ref.dtype)
        qt = lax.dot_general(wqv_ref[0], h, nt, preferred_element_type=F32)
        q_ref[:, rs] = (qt * (DA_QD ** -0.5 * LOG2E)).astype(q_ref.dtype)
        v_ref[:, rs] = lax.dot_general(wqv_ref[1], h, nt, preferred_element_type=F32).astype(v_ref.dtype)
        vm_ref[:, rs] = lax.dot_general(wqv_ref[2], h, nt, preferred_element_type=F32).astype(vm_ref.dtype)
        conv_rows(a)
    ext_ref[0:SUBLANES, :] = ext_ref[TM_PROJ:TM_PROJ + SUBLANES, :]


def _inproj(x2, g1, w_in, wqv_t, conv_w, conv_b, seq):
    n = x2.shape[0]
    assert seq % TM_PROJ == 0
    row = lambda i: (i, 0)
    fixed = lambda i: (0, 0)
    sds = lambda dt: jax.ShapeDtypeStruct((n, 512), dt)
    sds_t = jax.ShapeDtypeStruct((512, n), BF16)
    blk = pl.BlockSpec((TM_PROJ, 512), row)
    blk_t = pl.BlockSpec((512, TM_PROJ), lambda i: (0, i))
    return pl.pallas_call(
        functools.partial(_inproj_kernel, seq // TM_PROJ),
        grid=(n // TM_PROJ,),
        in_specs=[pl.BlockSpec((TM_PROJ, D_MODEL), row),
                  pl.BlockSpec((1, D_MODEL), fixed),
                  pl.BlockSpec((D_MODEL, 6 * 512), fixed),
                  pl.BlockSpec((3, 512, D_MODEL), lambda i: (0, 0, 0)),
                  pl.BlockSpec((CONV_W, D_MLSTM), fixed),
                  pl.BlockSpec((1, D_MLSTM), fixed)],
        out_specs=[blk, blk_t, blk, blk_t, blk, blk_t],
        out_shape=[sds(F32), sds_t, sds(F32), sds_t, sds(BF16), sds_t],
        scratch_shapes=[pltpu.VMEM((SUBLANES + TM_PROJ, D_MLSTM), F32)],
        compiler_params=_cparams(("arbitrary",)),
        name="inproj",
    )(x2, g1, w_in, wqv_t, conv_w, conv_b)


def _mlstm_kernel(*refs):
    c_ref = refs[0]
    vt_refs = refs[1:1 + MLSTM_BATCH]
    (z_ref, wq_ref, wk_ref, wg_ref, bg_ref,
     ng_ref, sk_ref, tril_ref, y_ref, st_ref, m_ref) = refs[1 + MLSTM_BATCH:]
    j = pl.program_id(1)

    @pl.when(j == 0)
    def _():
        st_ref[...] = jnp.zeros(st_ref.shape, F32)
        m_ref[...] = jnp.zeros(m_ref.shape, F32)

    seqs = range(MLSTM_BATCH)
    proj = [_mlstm_qk(c_ref.at[bb], vt_refs[bb], wq_ref, wk_ref) for bb in seqs]
    for bb in seqs:
        _mlstm_gate_logits(proj[bb], wg_ref, bg_ref)
    for bb in seqs:
        _mlstm_cumulate(proj[bb], tril_ref)
    pairs = [(h, bb) for h in range(ML_HEADS) for bb in seqs]
    prods = [_mlstm_head_products(h, proj[bb], st_ref.at[bb]) for h, bb in pairs]
    for (h, bb), hp in zip(pairs, prods):
        _mlstm_head(h, proj[bb], hp, z_ref.at[bb], ng_ref, sk_ref, y_ref.at[bb], st_ref.at[bb], m_ref.at[bb])


_NT = (((1,), (1,)), ((), ()))


def _mlstm_qk(c_ref, vm_ref, wq_ref, wk_ref):
    c_act = c_ref[...]
    vt = vm_ref[...]
    qts, ks = [], []
    for h in range(ML_HEADS):
        ch = c_act[:, h * ML_HD:(h + 1) * ML_HD].astype(BF16)
        qt = lax.dot_general(wq_ref[h], ch, _NT, preferred_element_type=F32)
        qts.append(qt.astype(BF16))
        ks.append(jnp.dot(ch, wk_ref[h], preferred_element_type=F32))
    return dict(c_act=c_act, vt=vt, qts=qts, ks=ks)


def _mlstm_gate_logits(p, wg_ref, bg_ref):
    gates_t = jnp.dot(wg_ref[:, 2 * D_MLSTM:3 * D_MLSTM], p["vt"], preferred_element_type=F32) + bg_ref[...]
    for h in range(ML_HEADS):
        hs = slice(h * ML_HD, (h + 1) * ML_HD)
        gates_t = gates_t + jnp.dot(wg_ref[:, hs], p["qts"][h], preferred_element_type=F32)
        gates_t = gates_t + lax.dot_general(wg_ref[:, D_MLSTM + h * ML_HD:D_MLSTM + (h + 1) * ML_HD],
                                            p["ks"][h].astype(BF16), _NT, preferred_element_type=F32)
    p["gates_t"] = gates_t


def _mlstm_cumulate(p, tril_ref):
    L = CHUNK
    gates_t = p["gates_t"]
    logf_t = jnp.minimum(gates_t, 0.0) - jnp.log(1.0 + jnp.exp(-jnp.abs(gates_t)))
    bcum_t = jnp.dot(logf_t, tril_ref[...], preferred_element_type=F32, precision=lax.Precision.HIGHEST)
    a8 = gates_t[0:SUBLANES] - bcum_t[SUBLANES:2 * SUBLANES]
    p["bcum_t"] = bcum_t
    p["a_cols"] = jnp.concatenate([a8, jnp.zeros((LANES - SUBLANES, L), F32)], axis=0).T


def _mlstm_head_products(h, p, st_ref):
    qt = p["qts"][h]
    k = (p["ks"][h] * (ML_HD ** -0.5)).astype(BF16)
    state_t = st_ref[h]
    s_t = jnp.dot(k, qt, preferred_element_type=F32)
    r_inter = jnp.dot(state_t.astype(BF16), qt, preferred_element_type=F32)
    return dict(k=k, state_t=state_t, s_t=s_t, r_inter=r_inter)


def _mlstm_head(h, p, hp, z_ref, ng_ref, sk_ref, y_ref, st_ref, m_ref):
    L = CHUNK
    c_act, vt, gates_t, bcum_t, a_cols = p["c_act"], p["vt"], p["gates_t"], p["bcum_t"], p["a_cols"]
    k, state_t, s_t, r_inter = hp["k"], hp["state_t"], hp["s_t"], hp["r_inter"]
    s_idx = lax.broadcasted_iota(jnp.int32, (L, L), 0)
    t_idx = lax.broadcasted_iota(jnp.int32, (L, L), 1)
    causal = s_idx <= t_idx
    ones_row = jnp.where(lax.broadcasted_iota(jnp.int32, (ML_HD, L), 0) == 0, 1.0, 0.0).astype(BF16)
    sl = slice(h * ML_HD, (h + 1) * ML_HD)
    vaug_t = jnp.concatenate([vt[sl, :], ones_row], axis=0)
    i_row = gates_t[h:h + 1, :]
    b_row = bcum_t[SUBLANES + h:SUBLANES + h + 1, :]
    a_row = i_row - b_row
    m_prev = m_ref[h][0:1, 0:1]

    e = jnp.where(causal, a_cols[:, h:h + 1], NEG)
    cm = jnp.max(e, axis=0, keepdims=True)
    mm = jnp.maximum(m_prev, cm)
    dexp = jnp.exp(e - mm)
    w_t = (dexp * s_t).astype(BF16)
    sp = jnp.exp(m_prev - mm)
    r_intra = jnp.dot(vaug_t, w_t, preferred_element_type=F32)
    tot = sp * r_inter + r_intra
    num = tot[:ML_HD]
    den = tot[ML_HD:ML_HD + 1]
    h_t = num / jnp.maximum(jnp.abs(den), jnp.exp(-(b_row + mm)))

    b_end = b_row[:, L - 1:L]
    mm_end = mm[:, L - 1:L]
    wk = jnp.exp(a_row - mm_end)
    decay = jnp.exp(m_prev - mm_end)
    vw = (vaug_t.astype(F32) * wk).astype(BF16)
    upd = jnp.dot(vw, k, preferred_element_type=F32)
    st_ref[h] = decay * state_t + upd
    m_ref[h] = jnp.broadcast_to(b_end + mm_end, (SUBLANES, LANES))

    mu = jnp.mean(h_t, axis=0, keepdims=True)
    xc = h_t - mu
    var = jnp.mean(xc * xc, axis=0, keepdims=True)
    hn = (xc * lax.rsqrt(var + EPS)).T * ng_ref[:, sl]
    hn = hn + sk_ref[:, sl] * c_act[:, sl]
    o = jax.nn.sigmoid(z_ref[:, sl])
    y_ref[:, sl] = (o * hn).astype(y_ref.dtype)


def _mlstm(c, vmt, z, wqt, wk, wgt, bgt, ng, sk, bsz, seq):
    nc = seq // CHUNK
    f2 = lambda b, j: (0, 0)
    f3 = lambda b, j: (0, 0, 0)
    triu = jnp.triu(jnp.ones((CHUNK, CHUNK), F32))
    blk = pl.BlockSpec((MLSTM_BATCH, CHUNK, D_MLSTM), lambda b, j: (b, j, 0))
    vt_specs = [pl.BlockSpec((D_MLSTM, CHUNK), functools.partial(lambda b, j, bb: (0, (b * MLSTM_BATCH + bb) * nc + j),
                                                                  bb=bb))
                for bb in range(MLSTM_BATCH)]
    c, z = (a.reshape(bsz, seq, D_MLSTM) for a in (c, z))
    y = pl.pallas_call(
        _mlstm_kernel,
        grid=(bsz // MLSTM_BATCH, nc),
        in_specs=[blk] + vt_specs + [
                  blk,
                  pl.BlockSpec((ML_HEADS, ML_HD, ML_HD), f3),
                  pl.BlockSpec((ML_HEADS, ML_HD, ML_HD), f3),
                  pl.BlockSpec((2 * SUBLANES, 3 * D_MLSTM), f2),
                  pl.BlockSpec((2 * SUBLANES, 1), f2),
                  pl.BlockSpec((1, D_MLSTM), f2),
                  pl.BlockSpec((1, D_MLSTM), f2),
                  pl.BlockSpec((CHUNK, CHUNK), f2)],
        out_specs=blk,
        out_shape=jax.ShapeDtypeStruct((bsz, seq, D_MLSTM), BF16),
        scratch_shapes=[pltpu.VMEM((MLSTM_BATCH, ML_HEADS, 2 * ML_HD, ML_HD), F32),
                        pltpu.VMEM((MLSTM_BATCH, ML_HEADS, SUBLANES, LANES), F32)],
        compiler_params=_cparams(("parallel", "arbitrary")),
        name="mlstm",
    )(c, *([vmt] * MLSTM_BATCH), z, wqt, wk, wgt, bgt, ng, sk, triu)
    return y.reshape(bsz * seq, D_MLSTM)


def _attn_kernel(lq_ref, qt_ref, k_ref, vt_ref, bias_ref, corner_ref, g_ref, o_ref,
                 acc_ref, m_ref, q2_ref, s_ref, p_ref, a_ref):
    tq = TQ
    i = pl.program_id(2)
    heads = range(ATTN_HEADS)
    row = lax.broadcasted_iota(jnp.int32, (LANES, tq), 0)
    for hh in heads:
        qt = qt_ref[hh * LANES:(hh + 1) * LANES, :]
        zero = jnp.zeros_like(qt)
        q2_ref[hh, :, 0:tq] = jnp.where(row < DA_QD, qt, zero)
        q2_ref[hh, :, tq:2 * tq] = jnp.where(row >= DA_QD, qt, zero)
    acc_ref[...] = jnp.zeros(acc_ref.shape, F32)
    m_ref[...] = jnp.full(m_ref.shape, NEG, F32)

    p_ref[:, 1] = jnp.zeros((ATTN_HEADS,) + p_ref.shape[2:], BF16)
    a_ref[:, 1] = jnp.ones((ATTN_HEADS,) + a_ref.shape[2:], F32)

    def logits(hh, kb, buf):
        start = pl.multiple_of(kb * tq, tq)
        s_ref[hh, buf] = jnp.dot(k_ref[pl.ds(start, tq), hh * LANES:(hh + 1) * LANES], q2_ref[hh],
                                 preferred_element_type=F32)

    ones_rows = jnp.ones((ACC_ROWS - DA_VD, tq), BF16)

    def values(hh, kb, buf):
        start = pl.multiple_of(jnp.maximum(kb, 0) * tq, tq)
        lhs = jnp.concatenate([vt_ref[hh * LANES:(hh + 1) * LANES, pl.ds(start, tq)], ones_rows], axis=0)
        pv = jnp.dot(lhs, p_ref[hh, buf], preferred_element_type=F32)
        acc_ref[hh] = a_ref[hh, buf] * acc_ref[hh] + pv

    def softmax(hh, buf, which_bias):
        if which_bias == 1:
            ks = slice(tq - REL_MAX_DIST, tq)
            corner = corner_ref[hh]
            for c0 in (0, tq):
                s_ref[hh, buf, ks, c0:c0 + REL_MAX_DIST] = s_ref[hh, buf, ks, c0:c0 + REL_MAX_DIST] + corner
        s = s_ref[hh, buf]
        if which_bias == 0:
            bias = bias_ref[hh]
            s = s + jnp.concatenate([bias, bias], axis=1)
        m_old = m_ref[hh]
        m_new = jnp.maximum(m_old, jnp.max(s, axis=0, keepdims=True))
        m_ref[hh] = m_new
        a_ref[hh, buf] = jnp.exp2(m_old - m_new)
        p_ref[hh, buf] = jnp.exp2(s - m_new).astype(BF16)

    def all_logits(kb, buf):
        for hh in heads:
            logits(hh, kb, buf)

    def all_values(kb, buf):
        for hh in heads:
            values(hh, kb, buf)

    def stage(kb, buf, has_next, which_bias):
        for hh in heads:
            softmax(hh, buf, which_bias)
        if has_next:
            all_logits(kb + 1, 1 - buf)
        all_values(kb - 1, 1 - buf)

    all_logits(0, 0)
    n_far = jnp.maximum(i - 1, 0)

    def far_pair(t, carry):
        stage(2 * t, 0, True, None)
        stage(2 * t + 1, 1, True, None)
        return carry

    lax.fori_loop(0, n_far // 2, far_pair, 0)
    kb0 = 2 * (n_far // 2)

    @pl.when(i == 0)
    def _():
        stage(0, 0, False, 0)
        all_values(0, 0)

    @pl.when(jnp.logical_and(i >= 1, n_far % 2 == 0))
    def _():
        stage(kb0, 0, True, 1)
        stage(kb0 + 1, 1, False, 0)
        all_values(kb0 + 1, 1)

    @pl.when(n_far % 2 == 1)
    def _():
        stage(kb0, 0, True, None)
        stage(kb0 + 1, 1, True, 1)
        stage(kb0 + 2, 0, False, 0)
        all_values(kb0 + 2, 0)

    lq = lq_ref[...]
    d01 = jnp.sum(lq[0:1, :] * lq[1:2, :], axis=-1, keepdims=True)
    d23 = jnp.sum(lq[2:3, :] * lq[3:4, :], axis=-1, keepdims=True)
    lam = jnp.exp(d01) - jnp.exp(d23) + LAM_INIT
    for hh in heads:
        o = acc_ref[hh, 0:DA_VD, :] * (1.0 / acc_ref[hh, DA_VD:DA_VD + 1, :])
        y = o[:, :tq] - lam * o[:, tq:]
        y = y * lax.rsqrt(jnp.mean(y * y, axis=0, keepdims=True) + SUBLN_EPS) * g_ref[...]
        o_ref[:, hh * LANES:(hh + 1) * LANES] = (y * (1.0 - LAM_INIT)).T.astype(o_ref.dtype)


def _rel_bias_tiles(rel_bias):
    L = TQ
    max_exact = REL_BUCKETS // 2
    n = jnp.arange(-(L - 1), 2 * L, dtype=jnp.int32)
    nn = jnp.maximum(n, 0)
    large = max_exact + (jnp.log(jnp.maximum(nn, 1).astype(F32) / max_exact)
                         / math.log(REL_MAX_DIST / max_exact) * (REL_BUCKETS - max_exact)).astype(jnp.int32)
    large = jnp.minimum(large, REL_BUCKETS - 1)
    bucket = jnp.where(nn < max_exact, nn, large)
    rb = (rel_bias.astype(F32) - rel_bias[REL_BUCKETS - 1].astype(F32)[None, :]).T
    t = jnp.where(n[None, :] >= 0, rb[:, bucket] * LOG2E, NEG)

    def toeplitz(v, m):
        vp = jnp.pad(v, ((0, 0), (0, 1)))
        flat = jnp.tile(vp, (1, m))[:, :m * (2 * m - 1)]
        return flat.reshape(v.shape[0], m, 2 * m - 1)[:, :, m - 1:]

    c = REL_MAX_DIST
    return toeplitz(t[:, :2 * L - 1], L), toeplitz(t[:, L:L + 2 * c - 1], c)


def _attn(lq, qt, kd, vt, bias, corner, g, bsz, seq):
    nq = seq // TQ
    return pl.pallas_call(
        _attn_kernel,
        grid=(bsz, DA_HEADS // ATTN_HEADS, nq),
        in_specs=[pl.BlockSpec((4, DA_QD), lambda b, h, i: (0, 0)),
                  pl.BlockSpec((ATTN_HEADS * LANES, TQ), lambda b, h, i: (h, b * nq + i)),
                  pl.BlockSpec((seq, ATTN_HEADS * LANES), lambda b, h, i: (b, h)),
                  pl.BlockSpec((ATTN_HEADS * LANES, seq), lambda b, h, i: (h, b)),
                  pl.BlockSpec((ATTN_HEADS, TQ, TQ), lambda b, h, i: (h, 0, 0)),
                  pl.BlockSpec((ATTN_HEADS, REL_MAX_DIST, REL_MAX_DIST), lambda b, h, i: (h, 0, 0)),
                  pl.BlockSpec((DA_VD, 1), lambda b, h, i: (0, 0))],
        out_specs=pl.BlockSpec((TQ, ATTN_HEADS * LANES), lambda b, h, i: (b * nq + i, h)),
        out_shape=jax.ShapeDtypeStruct((bsz * seq, D_DIFF), BF16),
        scratch_shapes=[pltpu.VMEM((ATTN_HEADS, ACC_ROWS, 2 * TQ), F32),
                        pltpu.VMEM((ATTN_HEADS, 1, 2 * TQ), F32),
                        pltpu.VMEM((ATTN_HEADS, LANES, 2 * TQ), BF16),
                        pltpu.VMEM((ATTN_HEADS, 2, TQ, 2 * TQ), F32),
                        pltpu.VMEM((ATTN_HEADS, 2, TQ, 2 * TQ), BF16),
                        pltpu.VMEM((ATTN_HEADS, 2, 1, 2 * TQ), F32)],
        compiler_params=_cparams(("parallel", "parallel", "arbitrary")),
        name="attn",
    )(lq, qt, kd, vt, bias, corner, g)


def _outproj_kernel(x_ref, ym_ref, yd_ref, wo_ref, g2_ref, wr_ref, br_ref, tri_ref,
                    x1_ref, h2_ref, meta_ref, gates_ref, cnt_ref, run_ref):
    step = pl.program_id(0)

    @pl.when(step == 0)
    def _():
        run_ref[...] = jnp.zeros(run_ref.shape, F32)

    tm = TM_PROJ // OUTPROJ_PARTS
    parts = [slice(a * tm, (a + 1) * tm) for a in range(OUTPROJ_PARTS)]
    x1s = [_outproj_residual(rs, x_ref, ym_ref, yd_ref, wo_ref, x1_ref) for rs in parts]
    lts = [_outproj_logits(rs, x1, g2_ref, wr_ref, br_ref, h2_ref) for rs, x1 in zip(parts, x1s)]
    for rs, lt in zip(parts, lts):
        _outproj_route(rs, lt, tri_ref, meta_ref, gates_ref, cnt_ref, run_ref)


def _outproj_residual(rs, x_ref, ym_ref, yd_ref, wo_ref, x1_ref):
    x1 = (x_ref[rs, :]
          + jnp.dot(ym_ref[rs, :], wo_ref[0:D_MLSTM, :], preferred_element_type=F32)
          + jnp.dot(yd_ref[rs, :], wo_ref[D_MLSTM:, :], preferred_element_type=F32))
    x1_ref[rs, :] = x1
    return x1


def _outproj_logits(rs, x1, g2_ref, wr_ref, br_ref, h2_ref):
    ms = jnp.mean(x1 * x1, axis=-1, keepdims=True)
    h2 = x1 * lax.rsqrt(ms + EPS) * g2_ref[...]
    h2_ref[rs] = _to_token_major(h2)
    h_hi = h2.astype(BF16)
    h_lo = (h2 - h_hi.astype(F32)).astype(BF16)
    lhs = jnp.concatenate([h_hi, h_lo, h_hi], axis=1)
    logits = jnp.dot(lhs, wr_ref[...], preferred_element_type=F32) + br_ref[...]
    return logits.T


def _outproj_route(rs, lt, tri_ref, meta_ref, gates_ref, cnt_ref, run_ref):
    tm = lt.shape[1]
    row8 = lax.broadcasted_iota(jnp.int32, (SUBLANES, tm), 0)
    lg = jnp.where(row8 < N_GROUPS, lt[0:SUBLANES], NEG)
    gmax = jnp.max(lg, axis=0, keepdims=True)
    gidx = jnp.min(jnp.where(lg == gmax, row8, SUBLANES), axis=0, keepdims=True)
    pg_sel = 1.0 / jnp.sum(jnp.exp(lg - gmax), axis=0, keepdims=True)
    le = jnp.zeros((EXP_PER_GROUP, tm), F32)
    for g in range(N_GROUPS):
        le = jnp.where(gidx == g, lt[SUBLANES * (g + 1):SUBLANES * (g + 2)], le)
    v1 = jnp.max(le, axis=0, keepdims=True)
    i1 = jnp.min(jnp.where(le == v1, row8, SUBLANES), axis=0, keepdims=True)
    le2 = jnp.where(row8 == i1, NEG, le)
    v2 = jnp.max(le2, axis=0, keepdims=True)
    i2 = jnp.min(jnp.where(le2 == v2, row8, SUBLANES), axis=0, keepdims=True)
    e21 = jnp.exp(v2 - v1)
    pw1 = 1.0 / (1.0 + e21)
    gate1 = pg_sel * pw1
    gate2 = pg_sel * (e21 * pw1)
    eid1 = gidx * EXP_PER_GROUP + i1
    eid2 = gidx * EXP_PER_GROUP + i2

    row32 = lax.broadcasted_iota(jnp.int32, (N_EXPERTS, tm), 0)
    oh1 = row32 == eid1
    oh2 = row32 == eid2
    oh = jnp.where(oh1, 1.0, 0.0) + jnp.where(oh2, 1.0, 0.0)
    before = jnp.dot(oh.astype(BF16), tri_ref[...], preferred_element_type=F32) + run_ref[:, 0:1]
    rank1 = jnp.sum(jnp.where(oh1, before, 0.0), axis=0, keepdims=True).astype(jnp.int32)
    rank2 = jnp.sum(jnp.where(oh2, before, 0.0), axis=0, keepdims=True).astype(jnp.int32)
    run = run_ref[...] + jnp.sum(oh, axis=1, keepdims=True)
    run_ref[...] = run
    cnt_ref[...] = run.astype(jnp.int32)

    meta_ref[:, rs] = jnp.where(row8 == 0, eid1, jnp.where(row8 == 1, eid2,
                                jnp.where(row8 == 2, rank1, jnp.where(row8 == 3, rank2, 0))))
    row128 = lax.broadcasted_iota(jnp.int32, (LANES, tm), 0)
    gt = jnp.where(row128 == 0, gate1, jnp.where(row128 == 1, gate2, 0.0))
    gates_ref[rs, :] = gt.T


def _outproj(x2, ym, yd, wo, g2, wr, br):
    n = x2.shape[0]
    row = lambda i: (i, 0)
    fixed = lambda i: (0, 0)
    part = TM_PROJ // OUTPROJ_PARTS
    tri = jnp.triu(jnp.ones((part, part), F32), k=1).astype(BF16)
    return pl.pallas_call(
        _outproj_kernel,
        grid=(n // TM_PROJ,),
        in_specs=[pl.BlockSpec((TM_PROJ, D_MODEL), row),
                  pl.BlockSpec((TM_PROJ, D_MLSTM), row),
                  pl.BlockSpec((TM_PROJ, D_DIFF), row),
                  pl.BlockSpec((D_MODEL, D_MODEL), fixed),
                  pl.BlockSpec((1, D_MODEL), fixed),
                  pl.BlockSpec((3 * D_MODEL, LANES), fixed),
                  pl.BlockSpec((1, LANES), fixed),
                  pl.BlockSpec((part, part), fixed)],
        out_specs=[pl.BlockSpec((TM_PROJ, D_MODEL), row),
                   pl.BlockSpec((TM_PROJ,) + ROW_TILE, lambda i: (i, 0, 0)),
                   pl.BlockSpec((SUBLANES, TM_PROJ), lambda i: (0, i)),
                   pl.BlockSpec((TM_PROJ, LANES), row),
                   pl.BlockSpec((N_EXPERTS, LANES), fixed)],
        out_shape=[jax.ShapeDtypeStruct((n, D_MODEL), F32),
                   jax.ShapeDtypeStruct((n,) + ROW_TILE, F32),
                   jax.ShapeDtypeStruct((SUBLANES, n), jnp.int32),
                   jax.ShapeDtypeStruct((n, LANES), F32),
                   jax.ShapeDtypeStruct((N_EXPERTS, LANES), jnp.int32)],
        scratch_shapes=[pltpu.VMEM((N_EXPERTS, LANES), F32)],
        compiler_params=_cparams(("arbitrary",)),
        name="outproj_router",
    )(x2, ym, yd, wo, g2, wr, br, tri)


def _row_copy(src_ref, src_row, dst_ref, dst_row, sem):
    return pltpu.make_async_copy(src_ref.at[pl.ds(src_row, 1)], dst_ref.at[pl.ds(dst_row, 1)], sem)


def _dispatch_kernel(ends_ref, pos_ref, h_ref, xs_ref, zero_ref, hbuf_ref, sem, bsem, zsem):
    step = pl.program_id(0)

    @pl.when(step == 0)
    def _():
        zero_ref[...] = jnp.zeros(zero_ref.shape, F32)

        def tile_copy(e):
            start = pl.multiple_of(ends_ref[e + 1] - TM_MOE, TM_MOE)
            return pltpu.make_async_copy(zero_ref, xs_ref.at[pl.ds(start, TM_MOE)], zsem)

        def zstart(e, carry):
            @pl.when(ends_ref[e + 1] > ends_ref[e])
            def _():
                tile_copy(e).start()
            return carry

        def zwait(e, carry):
            @pl.when(ends_ref[e + 1] > ends_ref[e])
            def _():
                tile_copy(e).wait()
            return carry

        lax.fori_loop(0, N_EXPERTS, zstart, 0)
        lax.fori_loop(0, N_EXPERTS, zwait, 0)

        def tail_copy(t):
            return pltpu.make_async_copy(zero_ref, xs_ref.at[pl.ds(pl.multiple_of(t * TM_MOE, TM_MOE), TM_MOE)], zsem)

        def tstart(t, carry):
            tail_copy(t).start()
            return carry

        def twait(t, carry):
            tail_copy(t).wait()
            return carry

        first_tail = ends_ref[N_EXPERTS] // TM_MOE
        lax.fori_loop(first_tail, xs_ref.shape[0] // TM_MOE, tstart, 0)
        lax.fori_loop(first_tail, xs_ref.shape[0] // TM_MOE, twait, 0)

    last = pl.num_programs(0) - 1
    slot = step % 3
    par = step % 2

    def block_copy(i, s):
        return pltpu.make_async_copy(h_ref.at[pl.ds(pl.multiple_of(i * TT, TT), TT)], hbuf_ref.at[s], bsem.at[s])

    @pl.when(step == 0)
    def _():
        block_copy(0, 0).start()

    @pl.when(step < last)
    def _():
        block_copy(step + 1, (step + 1) % 3).start()

    block_copy(step, slot).wait()
    src = hbuf_ref.at[slot]

    def issue(r, carry):
        _row_copy(src, r, xs_ref, pos_ref[0, r], sem.at[par]).start(priority=0)
        _row_copy(src, r, xs_ref, pos_ref[0, TT + r], sem.at[par]).start(priority=1)
        return carry

    def drain(s):
        for _ in range(2):
            pltpu.make_async_copy(src, xs_ref.at[pl.ds(0, TT)], sem.at[s]).wait()

    lax.fori_loop(0, TT, issue, 0)

    @pl.when(step > 0)
    def _():
        drain(1 - par)

    @pl.when(step == last)
    def _():
        drain(par)


def _dispatch(ends, pos, h2, n_rows):
    n = h2.shape[0]
    return pl.pallas_call(
        _dispatch_kernel,
        grid_spec=pltpu.PrefetchScalarGridSpec(
            num_scalar_prefetch=1,
            grid=(n // TT,),
            in_specs=[pl.BlockSpec((None, 1, 2 * TT), lambda i, ends: (i, 0, 0), memory_space=pltpu.SMEM),
                      pl.BlockSpec(memory_space=pl.ANY)],
            out_specs=pl.BlockSpec(memory_space=pl.ANY),
            scratch_shapes=[pltpu.VMEM((TM_MOE,) + ROW_TILE, F32),
                            pltpu.VMEM((3, TT) + ROW_TILE, F32),
                            pltpu.SemaphoreType.DMA((2,)),
                            pltpu.SemaphoreType.DMA((3,)),
                            pltpu.SemaphoreType.DMA(())]),
        out_shape=jax.ShapeDtypeStruct((n_rows,) + ROW_TILE, F32),
        compiler_params=_cparams(("arbitrary",)),
        name="dispatch",
    )(ends, pos, h2)


def _moe_kernel(texp_ref, nact_ref, xs_ref, wg32_ref, wu32_ref, wd32_ref, y_ref,
                wg_ref, wu_ref, wd_ref, xbuf_ref, xsem):
    t = pl.program_id(0)
    nact = nact_ref[0]

    def fetch(i):
        s = i % MOE_XSLOTS
        return pltpu.make_async_copy(xs_ref.at[pl.ds(pl.multiple_of(i * TM_MOE, TM_MOE), TM_MOE)],
                                     xbuf_ref.at[s], xsem.at[s])

    @pl.when(t == 0)
    def _():
        for i in range(MOE_XSLOTS - 1):
            @pl.when(i < nact)
            def _():
                fetch(i).start()

    @pl.when(t + MOE_XSLOTS - 1 < nact)
    def _():
        fetch(t + MOE_XSLOTS - 1).start()

    @pl.when(jnp.logical_or(t == 0, texp_ref[t] != texp_ref[jnp.maximum(t - 1, 0)]))
    def _():
        wg_ref[...] = wg32_ref[...].astype(BF16)
        wu_ref[...] = wu32_ref[...].astype(BF16)
        wd_ref[...] = wd32_ref[...].astype(BF16)

    @pl.when(t < nact)
    def _():
        fetch(t).wait()
        x_ref = xbuf_ref.at[t % MOE_XSLOTS]
        half = TM_MOE // MOE_SUBTILES
        hids = []
        for a in range(MOE_SUBTILES):
            x = _from_token_major(x_ref[a * half:(a + 1) * half]).astype(BF16)
            g = jnp.dot(x, wg_ref[...], preferred_element_type=F32)
            u = jnp.dot(x, wu_ref[...], preferred_element_type=F32)
            hids.append((g * jax.nn.sigmoid(g) * u).astype(BF16))
        for a in range(MOE_SUBTILES):
            y = jnp.dot(hids[a], wd_ref[...], preferred_element_type=F32)
            y_ref[a * half:(a + 1) * half] = _to_token_major(y)

    @pl.when(t >= nact)
    def _():
        y_ref[...] = jnp.zeros(y_ref.shape, F32)


def _moe(tile_expert, n_active, xs, w_eg, w_eu, w_ed):
    n_tiles = xs.shape[0] // TM_MOE
    out_tile = lambda t, te, na: (t, 0, 0)
    wsel = lambda t, te, na: (te[t], 0, 0)
    return pl.pallas_call(
        _moe_kernel,
        grid_spec=pltpu.PrefetchScalarGridSpec(
            num_scalar_prefetch=2,
            grid=(n_tiles,),
            in_specs=[pl.BlockSpec(memory_space=pl.ANY),
                      pl.BlockSpec((None, D_MODEL, D_FF_EXP), wsel),
                      pl.BlockSpec((None, D_MODEL, D_FF_EXP), wsel),
                      pl.BlockSpec((None, D_FF_EXP, D_MODEL), wsel)],
            out_specs=pl.BlockSpec((TM_MOE,) + ROW_TILE, out_tile),
            scratch_shapes=[pltpu.VMEM((D_MODEL, D_FF_EXP), BF16),
                            pltpu.VMEM((D_MODEL, D_FF_EXP), BF16),
                            pltpu.VMEM((D_FF_EXP, D_MODEL), BF16),
                            pltpu.VMEM((MOE_XSLOTS, TM_MOE) + ROW_TILE, F32),
                            pltpu.SemaphoreType.DMA((MOE_XSLOTS,))]),
        out_shape=jax.ShapeDtypeStruct(xs.shape, F32),
        compiler_params=_cparams(("arbitrary",)),
        name="moe",
    )(tile_expert, n_active, xs, w_eg, w_eu, w_ed)


def _combine_kernel(pos_ref, posn_ref, x1_ref, gates_ref, gf_ref, ys_ref, o_ref, buf_ref, sem):
    step = pl.program_id(0)
    slot = step % 2

    def gather(p_ref, s):
        def body(r, carry):
            _row_copy(ys_ref, p_ref[0, r], buf_ref.at[s, 0], r, sem.at[s]).start(priority=0)
            _row_copy(ys_ref, p_ref[0, TT + r], buf_ref.at[s, 1], r, sem.at[s]).start(priority=1)
            return carry
        lax.fori_loop(0, TT, body, 0)

    @pl.when(step == 0)
    def _():
        gather(pos_ref, 0)

    @pl.when(step < pl.num_programs(0) - 1)
    def _():
        gather(posn_ref, 1 - slot)

    for k in range(2):
        pltpu.make_async_copy(ys_ref.at[pl.ds(0, TT)], buf_ref.at[slot, k], sem.at[slot]).wait()
    gates = gates_ref[...]
    y = (x1_ref[...] + gates[:, 0:1] * _from_token_major(buf_ref[slot, 0])
         + gates[:, 1:2] * _from_token_major(buf_ref[slot, 1]))
    ms = jnp.mean(y * y, axis=-1, keepdims=True)
    o_ref[...] = y * lax.rsqrt(ms + EPS) * gf_ref[...]


def _combine(pos, x1, gates, gf, ys):
    n = x1.shape[0]
    nblk = n // TT
    return pl.pallas_call(
        _combine_kernel,
        grid=(nblk,),
        in_specs=[pl.BlockSpec((None, 1, 2 * TT), lambda i: (i, 0, 0), memory_space=pltpu.SMEM),
                  pl.BlockSpec((None, 1, 2 * TT), lambda i: (jnp.minimum(i + 1, nblk - 1), 0, 0),
                               memory_space=pltpu.SMEM),
                  pl.BlockSpec((TT, D_MODEL), lambda i: (i, 0)),
                  pl.BlockSpec((TT, LANES), lambda i: (i, 0)),
                  pl.BlockSpec((1, D_MODEL), lambda i: (0, 0)),
                  pl.BlockSpec(memory_space=pl.ANY)],
        out_specs=pl.BlockSpec((TT, D_MODEL), lambda i: (i, 0)),
        out_shape=jax.ShapeDtypeStruct((n, D_MODEL), F32),
        scratch_shapes=[pltpu.VMEM((2, 2, TT) + ROW_TILE, F32),
                        pltpu.SemaphoreType.DMA((2,))],
        compiler_params=_cparams(("arbitrary",)),
        name="combine",
    )(pos, pos, x1, gates, gf, ys)


def kernel(x, w_in, conv_w, conv_b, w_mq, w_mk, w_mgate, b_mgate, m_norm_g, m_skip, lambda_qk, da_norm_g,
           rel_bias, w_out, norm1_g, norm2_g, w_rg, b_rg, w_re, b_re, w_eg, w_eu, w_ed, normf_g):
    bsz, seq, _ = x.shape
    n = bsz * seq
    assert seq % TQ == 0 and seq % CHUNK == 0 and n % TM_PROJ == 0 and n % TT == 0 and bsz % MLSTM_BATCH == 0
    l = 0
    x2 = x.reshape(n, D_MODEL)

    w_in_b = w_in[l].astype(BF16)
    wqv_t = jnp.stack([w_in_b[:, 3 * 512:4 * 512].T, w_in_b[:, 5 * 512:6 * 512].T,
                       w_in_b[:, 1 * 512:2 * 512].T])
    c, vmt, z, qt, kd, vt = _inproj(x2, norm1_g[l][None, :], w_in_b, wqv_t, conv_w[l], conv_b[l][None, :], seq)

    wgt = jnp.zeros((2 * SUBLANES, 3 * D_MLSTM), F32)
    wgt = wgt.at[0:ML_HEADS].set(w_mgate[l][:, :ML_HEADS].T)
    wgt = wgt.at[SUBLANES:SUBLANES + ML_HEADS].set(w_mgate[l][:, ML_HEADS:].T).astype(BF16)
    bgt = jnp.zeros((2 * SUBLANES, 1), F32)
    bgt = bgt.at[0:ML_HEADS, 0].set(b_mgate[l][:ML_HEADS])
    bgt = bgt.at[SUBLANES:SUBLANES + ML_HEADS, 0].set(b_mgate[l][ML_HEADS:])
    y_m = _mlstm(c, vmt, z, jnp.swapaxes(w_mq[l], 1, 2).astype(BF16),
                 w_mk[l].astype(BF16), wgt, bgt, m_norm_g[l][None, :], m_skip[l][None, :], bsz, seq)

    bias_diag, bias_corner = _rel_bias_tiles(rel_bias)
    y_d = _attn(lambda_qk[l].astype(F32), qt, kd, vt, bias_diag, bias_corner,
                da_norm_g[l][:, None], bsz, seq)

    wr = jnp.zeros((D_MODEL, LANES), F32)
    wr = wr.at[:, :N_GROUPS].set(w_rg[l])
    wr = wr.at[:, SUBLANES:SUBLANES + N_EXPERTS].set(w_re[l].reshape(D_MODEL, N_EXPERTS))
    wr_hi = wr.astype(BF16)
    wr_lo = (wr - wr_hi.astype(F32)).astype(BF16)
    wr3 = jnp.concatenate([wr_hi, wr_hi, wr_lo], axis=0)
    br = jnp.zeros((1, LANES), F32)
    br = br.at[0, :N_GROUPS].set(b_rg[l])
    br = br.at[0, SUBLANES:SUBLANES + N_EXPERTS].set(b_re[l].reshape(N_EXPERTS))
    x1, h2, meta, gates, counts = _outproj(x2, y_m, y_d, w_out[l].astype(BF16), norm2_g[l][None, :], wr3, br)

    cnt = counts[:, 0]
    tiles_e = (cnt + TM_MOE - 1) // TM_MOE
    ends_t = jnp.cumsum(tiles_e)
    ends = jnp.concatenate([jnp.zeros((1,), jnp.int32), ends_t * TM_MOE]).astype(jnp.int32)
    offs = ends[:-1]
    n_tiles = (2 * n) // TM_MOE + N_EXPERTS
    n_active = ends_t[-1:].astype(jnp.int32)
    tile_ids = jnp.minimum(jnp.arange(n_tiles, dtype=jnp.int32), n_active[0] - 1)
    tile_expert = jnp.sum(tile_ids[:, None] >= ends_t[None, :], axis=1).astype(jnp.int32)
    eids = jnp.arange(N_EXPERTS, dtype=jnp.int32)[:, None]
    pos1 = jnp.sum(jnp.where(meta[0][None, :] == eids, offs[:, None], 0), axis=0) + meta[2]
    pos2 = jnp.sum(jnp.where(meta[1][None, :] == eids, offs[:, None], 0), axis=0) + meta[3]
    pos = jnp.concatenate([pos1.reshape(n // TT, 1, TT), pos2.reshape(n // TT, 1, TT)], axis=2)

    xs = _dispatch(ends, pos, h2, n_tiles * TM_MOE)
    ys = _moe(tile_expert, n_active, xs, w_eg[l], w_eu[l], w_ed[l])
    out = _combine(pos, x1, gates, normf_g[None, :], ys)
    return out.reshape(bsz, seq, D_MODEL)
```

```python
import functools
import math

import jax
import jax.numpy as jnp
from jax import lax
from jax.experimental import pallas as pl
from jax.experimental.pallas import tpu as pltpu

F32 = jnp.float32
BF16 = jnp.bfloat16

D_MODEL = 1024
D_MLSTM = 512
D_DIFF = 512
ML_HEADS = 4
ML_HD = 128
CONV_W = 4
CHUNK = 128
DA_HEADS = 4
DA_VD = 128
DA_QD = 64
REL_BUCKETS = 32
REL_MAX_DIST = 128
N_GROUPS = 4
EXP_PER_GROUP = 8
N_EXPERTS = 32
D_FF_EXP = 512
EPS = 1e-6
SUBLN_EPS = 1e-5
LAM_INIT = 0.8 - 0.6 * math.exp(-0.3 * 0)
NEG = -1e30
LOG2E = math.log2(math.e)

LANES = 128
SUBLANES = 8

TM_PROJ = 512
INPROJ_PARTS = 2
OUTPROJ_PARTS = 2
TQ = 512
ACC_ROWS = DA_VD + 16
ATTN_HEADS = 2
MLSTM_BATCH = 8
TM_MOE = 512
MOE_SUBTILES = 2
MOE_XSLOTS = 4
TT = 512
VMEM_LIMIT = 48 * 1024 * 1024


ROW_TILE = (D_MODEL // LANES, LANES)


def _cparams(sem):
    return pltpu.CompilerParams(dimension_semantics=sem, vmem_limit_bytes=VMEM_LIMIT)


def _to_token_major(x):
    return pltpu.einshape("a(bc)->abc", x, b=ROW_TILE[0])


def _from_token_major(x):
    return pltpu.einshape("abc->a(bc)", x)


def _inproj_kernel(tiles_per_seq, x_ref, g_ref, w_ref, wqv_ref, cw_ref, cb_ref,
                   c_ref, vm_ref, z_ref, q_ref, k_ref, v_ref, ext_ref):
    @pl.when(pl.program_id(0) % tiles_per_seq == 0)
    def _():
        ext_ref[0:SUBLANES, :] = jnp.zeros((SUBLANES, D_MLSTM), F32)

    tm = TM_PROJ // INPROJ_PARTS
    nt = (((1,), (1,)), ((), ()))
    cw = cw_ref[...]

    def normed(a):
        x = x_ref[a * tm:(a + 1) * tm, :]
        ms = jnp.mean(x * x, axis=-1, keepdims=True)
        return (x * lax.rsqrt(ms + EPS) * g_ref[...]).astype(BF16)

    def conv_rows(a):
        r0 = SUBLANES + a * tm
        conv = cb_ref[...] + cw[CONV_W - 1:CONV_W, :] * ext_ref[r0:r0 + tm, :]
        for k in range(1, CONV_W):
            conv = conv + cw[CONV_W - 1 - k:CONV_W - k, :] * ext_ref[r0 - k:r0 - k + tm, :]
        c_ref[a * tm:(a + 1) * tm, :] = conv * jax.nn.sigmoid(conv)

    hs = []
    for a in range(INPROJ_PARTS):
        hs.append(normed(a))
        ext_ref[SUBLANES + a * tm:SUBLANES + (a + 1) * tm, :] = jnp.dot(hs[a], w_ref[:, 0:512],
                                                                          preferred_element_type=F32)
    for a in range(INPROJ_PARTS):
        h = hs[a]
        rs = slice(a * tm, (a + 1) * tm)
        z_ref[rs, :] = jnp.dot(h, w_ref[:, 2 * 512:3 * 512], preferred_element_type=F32)
        k_ref[rs, :] = jnp.dot(h, w_ref[:, 4 * 512:5 * 512], preferred_element_type=F32).astype(k_ref.dtype)
        qt = lax.dot_general(wqv_ref[0], h, nt, preferred_element_type=F32)
        q_ref[:, rs] = (qt * (DA_QD ** -0.5 * LOG2E)).astype(q_ref.dtype)
        v_ref[:, rs] = lax.dot_general(wqv_ref[1], h, nt, preferred_element_type=F32).astype(v_ref.dtype)
        vm_ref[:, rs] = lax.dot_general(wqv_ref[2], h, nt, preferred_element_type=F32).astype(vm_ref.dtype)
        conv_rows(a)
    ext_ref[0:SUBLANES, :] = ext_ref[TM_PROJ:TM_PROJ + SUBLANES, :]


def _inproj(x2, g1, w_in, wqv_t, conv_w, conv_b, seq):
    n = x2.shape[0]
    assert seq % TM_PROJ == 0
    row = lambda i: (i, 0)
    fixed = lambda i: (0, 0)
    sds = lambda dt: jax.ShapeDtypeStruct((n, 512), dt)
    sds_t = jax.ShapeDtypeStruct((512, n), BF16)
    blk = pl.BlockSpec((TM_PROJ, 512), row)
    blk_t = pl.BlockSpec((512, TM_PROJ), lambda i: (0, i))
    return pl.pallas_call(
        functools.partial(_inproj_kernel, seq // TM_PROJ),
        grid=(n // TM_PROJ,),
        in_specs=[pl.BlockSpec((TM_PROJ, D_MODEL), row),
                  pl.BlockSpec((1, D_MODEL), fixed),
                  pl.BlockSpec((D_MODEL, 6 * 512), fixed),
                  pl.BlockSpec((3, 512, D_MODEL), lambda i: (0, 0, 0)),
                  pl.BlockSpec((CONV_W, D_MLSTM), fixed),
                  pl.BlockSpec((1, D_MLSTM), fixed)],
        out_specs=[blk, blk_t, blk, blk_t, blk, blk_t],
        out_shape=[sds(F32), sds_t, sds(F32), sds_t, sds(BF16), sds_t],
        scratch_shapes=[pltpu.VMEM((SUBLANES + TM_PROJ, D_MLSTM), F32)],
        compiler_params=_cparams(("arbitrary",)),
        name="inproj",
    )(x2, g1, w_in, wqv_t, conv_w, conv_b)


def _mlstm_kernel(*refs):
    c_ref = refs[0]
    vt_refs = refs[1:1 + MLSTM_BATCH]
    (z_ref, wq_ref, wk_ref, wg_ref, bg_ref,
     ng_ref, sk_ref, tril_ref, y_ref, st_ref, m_ref) = refs[1 + MLSTM_BATCH:]
    j = pl.program_id(1)

    @pl.when(j == 0)
    def _():
        st_ref[...] = jnp.zeros(st_ref.shape, F32)
        m_ref[...] = jnp.zeros(m_ref.shape, F32)

    seqs = range(MLSTM_BATCH)
    proj = [_mlstm_qk(c_ref.at[bb], vt_refs[bb], wq_ref, wk_ref) for bb in seqs]
    for bb in seqs:
        _mlstm_gate_logits(proj[bb], wg_ref, bg_ref)
    for bb in seqs:
        _mlstm_cumulate(proj[bb], tril_ref)
    pairs = [(h, bb) for h in range(ML_HEADS) for bb in seqs]
    prods = [_mlstm_head_products(h, proj[bb], st_ref.at[bb]) for h, bb in pairs]
    for (h, bb), hp in zip(pairs, prods):
        _mlstm_head(h, proj[bb], hp, z_ref.at[bb], ng_ref, sk_ref, y_ref.at[bb], st_ref.at[bb], m_ref.at[bb])


_NT = (((1,), (1,)), ((), ()))


def _mlstm_qk(c_ref, vm_ref, wq_ref, wk_ref):
    c_act = c_ref[...]
    vt = vm_ref[...]
    qts, ks = [], []
    for h in range(ML_HEADS):
        ch = c_act[:, h * ML_HD:(h + 1) * ML_HD].astype(BF16)
        qt = lax.dot_general(wq_ref[h], ch, _NT, preferred_element_type=F32)
        qts.append(qt.astype(BF16))
        ks.append(jnp.dot(ch, wk_ref[h], preferred_element_type=F32))
    return dict(c_act=c_act, vt=vt, qts=qts, ks=ks)


def _mlstm_gate_logits(p, wg_ref, bg_ref):
    gates_t = jnp.dot(wg_ref[:, 2 * D_MLSTM:3 * D_MLSTM], p["vt"], preferred_element_type=F32) + bg_ref[...]
    for h in range(ML_HEADS):
        hs = slice(h * ML_HD, (h + 1) * ML_HD)
        gates_t = gates_t + jnp.dot(wg_ref[:, hs], p["qts"][h], preferred_element_type=F32)
        gates_t = gates_t + lax.dot_general(wg_ref[:, D_MLSTM + h * ML_HD:D_MLSTM + (h + 1) * ML_HD],
                                            p["ks"][h].astype(BF16), _NT, preferred_element_type=F32)
    p["gates_t"] = gates_t


def _mlstm_cumulate(p, tril_ref):
    L = CHUNK
    gates_t = p["gates_t"]
    logf_t = jnp.minimum(gates_t, 0.0) - jnp.log(1.0 + jnp.exp(-jnp.abs(gates_t)))
    bcum_t = jnp.dot(logf_t, tril_ref[...], preferred_element_type=F32, precision=lax.Precision.HIGHEST)
    a8 = gates_t[0:SUBLANES] - bcum_t[SUBLANES:2 * SUBLANES]
    p["bcum_t"] = bcum_t
    p["a_cols"] = jnp.concatenate([a8, jnp.zeros((LANES - SUBLANES, L), F32)], axis=0).T


def _mlstm_head_products(h, p, st_ref):
    qt = p["qts"][h]
    k = (p["ks"][h] * (ML_HD ** -0.5)).astype(BF16)
    state_t = st_ref[h]
    s_t = jnp.dot(k, qt, preferred_element_type=F32)
    r_inter = jnp.dot(state_t.astype(BF16), qt, preferred_element_type=F32)
    return dict(k=k, state_t=state_t, s_t=s_t, r_inter=r_inter)


def _mlstm_head(h, p, hp, z_ref, ng_ref, sk_ref, y_ref, st_ref, m_ref):
    L = CHUNK
    c_act, vt, gates_t, bcum_t, a_cols = p["c_act"], p["vt"], p["gates_t"], p["bcum_t"], p["a_cols"]
    k, state_t, s_t, r_inter = hp["k"], hp["state_t"], hp["s_t"], hp["r_inter"]
    s_idx = lax.broadcasted_iota(jnp.int32, (L, L), 0)
    t_idx = lax.broadcasted_iota(jnp.int32, (L, L), 1)
    causal = s_idx <= t_idx
    ones_row = jnp.where(lax.broadcasted_iota(jnp.int32, (ML_HD, L), 0) == 0, 1.0, 0.0).astype(BF16)
    sl = slice(h * ML_HD, (h + 1) * ML_HD)
    vaug_t = jnp.concatenate([vt[sl, :], ones_row], axis=0)
    i_row = gates_t[h:h + 1, :]
    b_row = bcum_t[SUBLANES + h:SUBLANES + h + 1, :]
    a_row = i_row - b_row
    m_prev = m_ref[h][0:1, 0:1]

    e = jnp.where(causal, a_cols[:, h:h + 1], NEG)
    cm = jnp.max(e, axis=0, keepdims=True)
    mm = jnp.maximum(m_prev, cm)
    dexp = jnp.exp(e - mm)
    w_t = (dexp * s_t).astype(BF16)
    sp = jnp.exp(m_prev - mm)
    r_intra = jnp.dot(vaug_t, w_t, preferred_element_type=F32)
    tot = sp * r_inter + r_intra
    num = tot[:ML_HD]
    den = tot[ML_HD:ML_HD + 1]
    h_t = num / jnp.maximum(jnp.abs(den), jnp.exp(-(b_row + mm)))

    b_end = b_row[:, L - 1:L]
    mm_end = mm[:, L - 1:L]
    wk = jnp.exp(a_row - mm_end)
    decay = jnp.exp(m_prev - mm_end)
    vw = (vaug_t.astype(F32) * wk).astype(BF16)
    upd = jnp.dot(vw, k, preferred_element_type=F32)
    st_ref[h] = decay * state_t + upd
    m_ref[h] = jnp.broadcast_to(b_end + mm_end, (SUBLANES, LANES))

    mu = jnp.mean(h_t, axis=0, keepdims=True)
    xc = h_t - mu
    var = jnp.mean(xc * xc, axis=0, keepdims=True)
    hn = (xc * lax.rsqrt(var + EPS)).T * ng_ref[:, sl]
    hn = hn + sk_ref[:, sl] * c_act[:, sl]
    o = jax.nn.sigmoid(z_ref[:, sl])
    y_ref[:, sl] = (o * hn).astype(y_ref.dtype)


def _mlstm(c, vmt, z, wqt, wk, wgt, bgt, ng, sk, bsz, seq):
    nc = seq // CHUNK
    f2 = lambda b, j: (0, 0)
    f3 = lambda b, j: (0, 0, 0)
    triu = jnp.triu(jnp.ones((CHUNK, CHUNK), F32))
    blk = pl.BlockSpec((MLSTM_BATCH, CHUNK, D_MLSTM), lambda b, j: (b, j, 0))
    vt_specs = [pl.BlockSpec((D_MLSTM, CHUNK), functools.partial(lambda b, j, bb: (0, (b * MLSTM_BATCH + bb) * nc + j),
                                                                  bb=bb))
                for bb in range(MLSTM_BATCH)]
    c, z = (a.reshape(bsz, seq, D_MLSTM) for a in (c, z))
    y = pl.pallas_call(
        _mlstm_kernel,
        grid=(bsz // MLSTM_BATCH, nc),
        in_specs=[blk] + vt_specs + [
                  blk,
                  pl.BlockSpec((ML_HEADS, ML_HD, ML_HD), f3),
                  pl.BlockSpec((ML_HEADS, ML_HD, ML_HD), f3),
                  pl.BlockSpec((2 * SUBLANES, 3 * D_MLSTM), f2),
                  pl.BlockSpec((2 * SUBLANES, 1), f2),
                  pl.BlockSpec((1, D_MLSTM), f2),
                  pl.BlockSpec((1, D_MLSTM), f2),
                  pl.BlockSpec((CHUNK, CHUNK), f2)],
        out_specs=blk,
        out_shape=jax.ShapeDtypeStruct((bsz, seq, D_MLSTM), BF16),
        scratch_shapes=[pltpu.VMEM((MLSTM_BATCH, ML_HEADS, 2 * ML_HD, ML_HD), F32),
                        pltpu.VMEM((MLSTM_BATCH, ML_HEADS, SUBLANES, LANES), F32)],
        compiler_params=_cparams(("parallel", "arbitrary")),
        name="mlstm",
    )(c, *([vmt] * MLSTM_BATCH), z, wqt, wk, wgt, bgt, ng, sk, triu)
    return y.reshape(bsz * seq, D_MLSTM)


def _attn_kernel(lq_ref, qt_ref, k_ref, vt_ref, bias_ref, corner_ref, g_ref, o_ref,
                 acc_ref, m_ref, q2_ref, s_ref, p_ref, a_ref):
    tq = TQ
    i = pl.program_id(2)
    heads = range(ATTN_HEADS)
    row = lax.broadcasted_iota(jnp.int32, (LANES, tq), 0)
    for hh in heads:
        qt = qt_ref[hh * LANES:(hh + 1) * LANES, :]
        zero = jnp.zeros_like(qt)
        q2_ref[hh, :, 0:tq] = jnp.where(row < DA_QD, qt, zero)
        q2_ref[hh, :, tq:2 * tq] = jnp.where(row >= DA_QD, qt, zero)
    acc_ref[...] = jnp.zeros(acc_ref.shape, F32)
    m_ref[...] = jnp.full(m_ref.shape, NEG, F32)

    p_ref[:, 1] = jnp.zeros((ATTN_HEADS,) + p_ref.shape[2:], BF16)
    a_ref[:, 1] = jnp.ones((ATTN_HEADS,) + a_ref.shape[2:], F32)

    def logits(hh, kb, buf):
        start = pl.multiple_of(kb * tq, tq)
        s_ref[hh, buf] = jnp.dot(k_ref[pl.ds(start, tq), hh * LANES:(hh + 1) * LANES], q2_ref[hh],
                                 preferred_element_type=F32)

    ones_rows = jnp.ones((ACC_ROWS - DA_VD, tq), BF16)

    def values(hh, kb, buf):
        start = pl.multiple_of(jnp.maximum(kb, 0) * tq, tq)
        lhs = jnp.concatenate([vt_ref[hh * LANES:(hh + 1) * LANES, pl.ds(start, tq)], ones_rows], axis=0)
        pv = jnp.dot(lhs, p_ref[hh, buf], preferred_element_type=F32)
        acc_ref[hh] = a_ref[hh, buf] * acc_ref[hh] + pv

    def softmax(hh, buf, which_bias):
        if which_bias == 1:
            ks = slice(tq - REL_MAX_DIST, tq)
            corner = corner_ref[hh]
            for c0 in (0, tq):
                s_ref[hh, buf, ks, c0:c0 + REL_MAX_DIST] = s_ref[hh, buf, ks, c0:c0 + REL_MAX_DIST] + corner
        s = s_ref[hh, buf]
        if which_bias == 0:
            bias = bias_ref[hh]
            s = s + jnp.concatenate([bias, bias], axis=1)
        m_old = m_ref[hh]
        m_new = jnp.maximum(m_old, jnp.max(s, axis=0, keepdims=True))
        m_ref[hh] = m_new
        a_ref[hh, buf] = jnp.exp2(m_old - m_new)
        p_ref[hh, buf] = jnp.exp2(s - m_new).astype(BF16)

    def all_logits(kb, buf):
        for hh in heads:
            logits(hh, kb, buf)

    def all_values(kb, buf):
        for hh in heads:
            values(hh, kb, buf)

    def stage(kb, buf, has_next, which_bias):
        for hh in heads:
            softmax(hh, buf, which_bias)
        if has_next:
            all_logits(kb + 1, 1 - buf)
        all_values(kb - 1, 1 - buf)

    all_logits(0, 0)
    n_far = jnp.maximum(i - 1, 0)

    def far_pair(t, carry):
        stage(2 * t, 0, True, None)
        stage(2 * t + 1, 1, True, None)
        return carry

    lax.fori_loop(0, n_far // 2, far_pair, 0)
    kb0 = 2 * (n_far // 2)

    @pl.when(i == 0)
    def _():
        stage(0, 0, False, 0)
        all_values(0, 0)

    @pl.when(jnp.logical_and(i >= 1, n_far % 2 == 0))
    def _():
        stage(kb0, 0, True, 1)
        stage(kb0 + 1, 1, False, 0)
        all_values(kb0 + 1, 1)

    @pl.when(n_far % 2 == 1)
    def _():
        stage(kb0, 0, True, None)
        stage(kb0 + 1, 1, True, 1)
        stage(kb0 + 2, 0, False, 0)
        all_values(kb0 + 2, 0)

    lq = lq_ref[...]
    d01 = jnp.sum(lq[0:1, :] * lq[1:2, :], axis=-1, keepdims=True)
    d23 = jnp.sum(lq[2:3, :] * lq[3:4, :], axis=-1, keepdims=True)
    lam = jnp.exp(d01) - jnp.exp(d23) + LAM_INIT
    for hh in heads:
        o = acc_ref[hh, 0:DA_VD, :] * (1.0 / acc_ref[hh, DA_VD:DA_VD + 1, :])
        y = o[:, :tq] - lam * o[:, tq:]
        y = y * lax.rsqrt(jnp.mean(y * y, axis=0, keepdims=True) + SUBLN_EPS) * g_ref[...]
        o_ref[:, hh * LANES:(hh + 1) * LANES] = (y * (1.0 - LAM_INIT)).T.astype(o_ref.dtype)


def _rel_bias_tiles(rel_bias):
    L = TQ
    max_exact = REL_BUCKETS // 2
    n = jnp.arange(-(L - 1), 2 * L, dtype=jnp.int32)
    nn = jnp.maximum(n, 0)
    large = max_exact + (jnp.log(jnp.maximum(nn, 1).astype(F32) / max_exact)
                         / math.log(REL_MAX_DIST / max_exact) * (REL_BUCKETS - max_exact)).astype(jnp.int32)
    large = jnp.minimum(large, REL_BUCKETS - 1)
    bucket = jnp.where(nn < max_exact, nn, large)
    rb = (rel_bias.astype(F32) - rel_bias[REL_BUCKETS - 1].astype(F32)[None, :]).T
    t = jnp.where(n[None, :] >= 0, rb[:, bucket] * LOG2E, NEG)

    def toeplitz(v, m):
        vp = jnp.pad(v, ((0, 0), (0, 1)))
        flat = jnp.tile(vp, (1, m))[:, :m * (2 * m - 1)]
        return flat.reshape(v.shape[0], m, 2 * m - 1)[:, :, m - 1:]

    c = REL_MAX_DIST
    return toeplitz(t[:, :2 * L - 1], L), toeplitz(t[:, L:L + 2 * c - 1], c)


def _attn(lq, qt, kd, vt, bias, corner, g, bsz, seq):
    nq = seq // TQ
    return pl.pallas_call(
        _attn_kernel,
        grid=(bsz, DA_HEADS // ATTN_HEADS, nq),
        in_specs=[pl.BlockSpec((4, DA_QD), lambda b, h, i: (0, 0)),
                  pl.BlockSpec((ATTN_HEADS * LANES, TQ), lambda b, h, i: (h, b * nq + i)),
                  pl.BlockSpec((seq, ATTN_HEADS * LANES), lambda b, h, i: (b, h)),
                  pl.BlockSpec((ATTN_HEADS * LANES, seq), lambda b, h, i: (h, b)),
                  pl.BlockSpec((ATTN_HEADS, TQ, TQ), lambda b, h, i: (h, 0, 0)),
                  pl.BlockSpec((ATTN_HEADS, REL_MAX_DIST, REL_MAX_DIST), lambda b, h, i: (h, 0, 0)),
                  pl.BlockSpec((DA_VD, 1), lambda b, h, i: (0, 0))],
        out_specs=pl.BlockSpec((TQ, ATTN_HEADS * LANES), lambda b, h, i: (b * nq + i, h)),
        out_shape=jax.ShapeDtypeStruct((bsz * seq, D_DIFF), BF16),
        scratch_shapes=[pltpu.VMEM((ATTN_HEADS, ACC_ROWS, 2 * TQ), F32),
                        pltpu.VMEM((ATTN_HEADS, 1, 2 * TQ), F32),
                        pltpu.VMEM((ATTN_HEADS, LANES, 2 * TQ), BF16),
                        pltpu.VMEM((ATTN_HEADS, 2, TQ, 2 * TQ), F32),
                        pltpu.VMEM((ATTN_HEADS, 2, TQ, 2 * TQ), BF16),
                        pltpu.VMEM((ATTN_HEADS, 2, 1, 2 * TQ), F32)],
        compiler_params=_cparams(("parallel", "parallel", "arbitrary")),
        name="attn",
    )(lq, qt, kd, vt, bias, corner, g)


def _outproj_kernel(x_ref, ym_ref, yd_ref, wo_ref, g2_ref, wr_ref, br_ref, tri_ref,
                    x1_ref, h2_ref, meta_ref, gates_ref, cnt_ref, run_ref):
    step = pl.program_id(0)

    @pl.when(step == 0)
    def _():
        run_ref[...] = jnp.zeros(run_ref.shape, F32)

    tm = TM_PROJ // OUTPROJ_PARTS
    parts = [slice(a * tm, (a + 1) * tm) for a in range(OUTPROJ_PARTS)]
    x1s = [_outproj_residual(rs, x_ref, ym_ref, yd_ref, wo_ref, x1_ref) for rs in parts]
    lts = [_outproj_logits(rs, x1, g2_ref, wr_ref, br_ref, h2_ref) for rs, x1 in zip(parts, x1s)]
    for rs, lt in zip(parts, lts):
        _outproj_route(rs, lt, tri_ref, meta_ref, gates_ref, cnt_ref, run_ref)


def _outproj_residual(rs, x_ref, ym_ref, yd_ref, wo_ref, x1_ref):
    x1 = (x_ref[rs, :]
          + jnp.dot(ym_ref[rs, :], wo_ref[0:D_MLSTM, :], preferred_element_type=F32)
          + jnp.dot(yd_ref[rs, :], wo_ref[D_MLSTM:, :], preferred_element_type=F32))
    x1_ref[rs, :] = x1
    return x1


def _outproj_logits(rs, x1, g2_ref, wr_ref, br_ref, h2_ref):
    ms = jnp.mean(x1 * x1, axis=-1, keepdims=True)
    h2 = x1 * lax.rsqrt(ms + EPS) * g2_ref[...]
    h2_ref[rs] = _to_token_major(h2)
    h_hi = h2.astype(BF16)
    h_lo = (h2 - h_hi.astype(F32)).astype(BF16)
    lhs = jnp.concatenate([h_hi, h_lo, h_hi], axis=1)
    logits = jnp.dot(lhs, wr_ref[...], preferred_element_type=F32) + br_ref[...]
    return logits.T


def _outproj_route(rs, lt, tri_ref, meta_ref, gates_ref, cnt_ref, run_ref):
    tm = lt.shape[1]
    row8 = lax.broadcasted_iota(jnp.int32, (SUBLANES, tm), 0)
    lg = jnp.where(row8 < N_GROUPS, lt[0:SUBLANES], NEG)
    gmax = jnp.max(lg, axis=0, keepdims=True)
    gidx = jnp.min(jnp.where(lg == gmax, row8, SUBLANES), axis=0, keepdims=True)
    pg_sel = 1.0 / jnp.sum(jnp.exp(lg - gmax), axis=0, keepdims=True)
    le = jnp.zeros((EXP_PER_GROUP, tm), F32)
    for g in range(N_GROUPS):
        le = jnp.where(gidx == g, lt[SUBLANES * (g + 1):SUBLANES * (g + 2)], le)
    v1 = jnp.max(le, axis=0, keepdims=True)
    i1 = jnp.min(jnp.where(le == v1, row8, SUBLANES), axis=0, keepdims=True)
    le2 = jnp.where(row8 == i1, NEG, le)
    v2 = jnp.max(le2, axis=0, keepdims=True)
    i2 = jnp.min(jnp.where(le2 == v2, row8, SUBLANES), axis=0, keepdims=True)
    e21 = jnp.exp(v2 - v1)
    pw1 = 1.0 / (1.0 + e21)
    gate1 = pg_sel * pw1
    gate2 = pg_sel * (e21 * pw1)
    eid1 = gidx * EXP_PER_GROUP + i1
    eid2 = gidx * EXP_PER_GROUP + i2

    row32 = lax.broadcasted_iota(jnp.int32, (N_EXPERTS, tm), 0)
    oh1 = row32 == eid1
    oh2 = row32 == eid2
    oh = jnp.where(oh1, 1.0, 0.0) + jnp.where(oh2, 1.0, 0.0)
    before = jnp.dot(oh.astype(BF16), tri_ref[...], preferred_element_type=F32) + run_ref[:, 0:1]
    rank1 = jnp.sum(jnp.where(oh1, before, 0.0), axis=0, keepdims=True).astype(jnp.int32)
    rank2 = jnp.sum(jnp.where(oh2, before, 0.0), axis=0, keepdims=True).astype(jnp.int32)
    run = run_ref[...] + jnp.sum(oh, axis=1, keepdims=True)
    run_ref[...] = run
    cnt_ref[...] = run.astype(jnp.int32)

    meta_ref[:, rs] = jnp.where(row8 == 0, eid1, jnp.where(row8 == 1, eid2,
                                jnp.where(row8 == 2, rank1, jnp.where(row8 == 3, rank2, 0))))
    row128 = lax.broadcasted_iota(jnp.int32, (LANES, tm), 0)
    gt = jnp.where(row128 == 0, gate1, jnp.where(row128 == 1, gate2, 0.0))
    gates_ref[rs, :] = gt.T


def _outproj(x2, ym, yd, wo, g2, wr, br):
    n = x2.shape[0]
    row = lambda i: (i, 0)
    fixed = lambda i: (0, 0)
    part = TM_PROJ // OUTPROJ_PARTS
    tri = jnp.triu(jnp.ones((part, part), F32), k=1).astype(BF16)
    return pl.pallas_call(
        _outproj_kernel,
        grid=(n // TM_PROJ,),
        in_specs=[pl.BlockSpec((TM_PROJ, D_MODEL), row),
                  pl.BlockSpec((TM_PROJ, D_MLSTM), row),
                  pl.BlockSpec((TM_PROJ, D_DIFF), row),
                  pl.BlockSpec((D_MODEL, D_MODEL), fixed),
                  pl.BlockSpec((1, D_MODEL), fixed),
                  pl.BlockSpec((3 * D_MODEL, LANES), fixed),
                  pl.BlockSpec((1, LANES), fixed),
                  pl.BlockSpec((part, part), fixed)],
        out_specs=[pl.BlockSpec((TM_PROJ, D_MODEL), row),
                   pl.BlockSpec((TM_PROJ,) + ROW_TILE, lambda i: (i, 0, 0)),
                   pl.BlockSpec((SUBLANES, TM_PROJ), lambda i: (0, i)),
                   pl.BlockSpec((TM_PROJ, LANES), row),
                   pl.BlockSpec((N_EXPERTS, LANES), fixed)],
        out_shape=[jax.ShapeDtypeStruct((n, D_MODEL), F32),
                   jax.ShapeDtypeStruct((n,) + ROW_TILE, F32),
                   jax.ShapeDtypeStruct((SUBLANES, n), jnp.int32),
                   jax.ShapeDtypeStruct((n, LANES), F32),
                   jax.ShapeDtypeStruct((N_EXPERTS, LANES), jnp.int32)],
        scratch_shapes=[pltpu.VMEM((N_EXPERTS, LANES), F32)],
        compiler_params=_cparams(("arbitrary",)),
        name="outproj_router",
    )(x2, ym, yd, wo, g2, wr, br, tri)


def _row_copy(src_ref, src_row, dst_ref, dst_row, sem):
    return pltpu.make_async_copy(src_ref.at[pl.ds(src_row, 1)], dst_ref.at[pl.ds(dst_row, 1)], sem)


def _dispatch_kernel(ends_ref, pos_ref, h_ref, xs_ref, zero_ref, hbuf_ref, sem, bsem, zsem):
    step = pl.program_id(0)

    @pl.when(step == 0)
    def _():
        zero_ref[...] = jnp.zeros(zero_ref.shape, F32)

        def tile_copy(e):
            start = pl.multiple_of(ends_ref[e + 1] - TM_MOE, TM_MOE)
            return pltpu.make_async_copy(zero_ref, xs_ref.at[pl.ds(start, TM_MOE)], zsem)

        def zstart(e, carry):
            @pl.when(ends_ref[e + 1] > ends_ref[e])
            def _():
                tile_copy(e).start()
            return carry

        def zwait(e, carry):
            @pl.when(ends_ref[e + 1] > ends_ref[e])
            def _():
                tile_copy(e).wait()
            return carry

        lax.fori_loop(0, N_EXPERTS, zstart, 0)
        lax.fori_loop(0, N_EXPERTS, zwait, 0)

        def tail_copy(t):
            return pltpu.make_async_copy(zero_ref, xs_ref.at[pl.ds(pl.multiple_of(t * TM_MOE, TM_MOE), TM_MOE)], zsem)

        def tstart(t, carry):
            tail_copy(t).start()
            return carry

        def twait(t, carry):
            tail_copy(t).wait()
            return carry

        first_tail = ends_ref[N_EXPERTS] // TM_MOE
        lax.fori_loop(first_tail, xs_ref.shape[0] // TM_MOE, tstart, 0)
        lax.fori_loop(first_tail, xs_ref.shape[0] // TM_MOE, twait, 0)

    last = pl.num_programs(0) - 1
    slot = step % 3
    par = step % 2

    def block_copy(i, s):
        return pltpu.make_async_copy(h_ref.at[pl.ds(pl.multiple_of(i * TT, TT), TT)], hbuf_ref.at[s], bsem.at[s])

    @pl.when(step == 0)
    def _():
        block_copy(0, 0).start()

    @pl.when(step < last)
    def _():
        block_copy(step + 1, (step + 1) % 3).start()

    block_copy(step, slot).wait()
    src = hbuf_ref.at[slot]

    def issue(r, carry):
        _row_copy(src, r, xs_ref, pos_ref[0, r], sem.at[par]).start(priority=0)
        _row_copy(src, r, xs_ref, pos_ref[0, TT + r], sem.at[par]).start(priority=1)
        return carry

    def drain(s):
        for _ in range(2):
            pltpu.make_async_copy(src, xs_ref.at[pl.ds(0, TT)], sem.at[s]).wait()

    lax.fori_loop(0, TT, issue, 0)

    @pl.when(step > 0)
    def _():
        drain(1 - par)

    @pl.when(step == last)
    def _():
        drain(par)


def _dispatch(ends, pos, h2, n_rows):
    n = h2.shape[0]
    return pl.pallas_call(
        _dispatch_kernel,
        grid_spec=pltpu.PrefetchScalarGridSpec(
            num_scalar_prefetch=1,
            grid=(n // TT,),
            in_specs=[pl.BlockSpec((None, 1, 2 * TT), lambda i, ends: (i, 0, 0), memory_space=pltpu.SMEM),
                      pl.BlockSpec(memory_space=pl.ANY)],
            out_specs=pl.BlockSpec(memory_space=pl.ANY),
            scratch_shapes=[pltpu.VMEM((TM_MOE,) + ROW_TILE, F32),
                            pltpu.VMEM((3, TT) + ROW_TILE, F32),
                            pltpu.SemaphoreType.DMA((2,)),
                            pltpu.SemaphoreType.DMA((3,)),
                            pltpu.SemaphoreType.DMA(())]),
        out_shape=jax.ShapeDtypeStruct((n_rows,) + ROW_TILE, F32),
        compiler_params=_cparams(("arbitrary",)),
        name="dispatch",
    )(ends, pos, h2)


def _moe_kernel(texp_ref, nact_ref, xs_ref, wg32_ref, wu32_ref, wd32_ref, y_ref,
                wg_ref, wu_ref, wd_ref, xbuf_ref, xsem):
    t = pl.program_id(0)
    nact = nact_ref[0]

    def fetch(i):
        s = i % MOE_XSLOTS
        return pltpu.make_async_copy(xs_ref.at[pl.ds(pl.multiple_of(i * TM_MOE, TM_MOE), TM_MOE)],
                                     xbuf_ref.at[s], xsem.at[s])

    @pl.when(t == 0)
    def _():
        for i in range(MOE_XSLOTS - 1):
            @pl.when(i < nact)
            def _():
                fetch(i).start()

    @pl.when(t + MOE_XSLOTS - 1 < nact)
    def _():
        fetch(t + MOE_XSLOTS - 1).start()

    @pl.when(jnp.logical_or(t == 0, texp_ref[t] != texp_ref[jnp.maximum(t - 1, 0)]))
    def _():
        wg_ref[...] = wg32_ref[...].astype(BF16)
        wu_ref[...] = wu32_ref[...].astype(BF16)
        wd_ref[...] = wd32_ref[...].astype(BF16)

    @pl.when(t < nact)
    def _():
        fetch(t).wait()
        x_ref = xbuf_ref.at[t % MOE_XSLOTS]
        half = TM_MOE // MOE_SUBTILES
        hids = []
        for a in range(MOE_SUBTILES):
            x = _from_token_major(x_ref[a * half:(a + 1) * half]).astype(BF16)
            g = jnp.dot(x, wg_ref[...], preferred_element_type=F32)
            u = jnp.dot(x, wu_ref[...], preferred_element_type=F32)
            hids.append((g * jax.nn.sigmoid(g) * u).astype(BF16))
        for a in range(MOE_SUBTILES):
            y = jnp.dot(hids[a], wd_ref[...], preferred_element_type=F32)
            y_ref[a * half:(a + 1) * half] = _to_token_major(y)

    @pl.when(t >= nact)
    def _():
        y_ref[...] = jnp.zeros(y_ref.shape, F32)


def _moe(tile_expert, n_active, xs, w_eg, w_eu, w_ed):
    n_tiles = xs.shape[0] // TM_MOE
    out_tile = lambda t, te, na: (t, 0, 0)
    wsel = lambda t, te, na: (te[t], 0, 0)
    return pl.pallas_call(
        _moe_kernel,
        grid_spec=pltpu.PrefetchScalarGridSpec(
            num_scalar_prefetch=2,
            grid=(n_tiles,),
            in_specs=[pl.BlockSpec(memory_space=pl.ANY),
                      pl.BlockSpec((None, D_MODEL, D_FF_EXP), wsel),
                      pl.BlockSpec((None, D_MODEL, D_FF_EXP), wsel),
                      pl.BlockSpec((None, D_FF_EXP, D_MODEL), wsel)],
            out_specs=pl.BlockSpec((TM_MOE,) + ROW_TILE, out_tile),
            scratch_shapes=[pltpu.VMEM((D_MODEL, D_FF_EXP), BF16),
                            pltpu.VMEM((D_MODEL, D_FF_EXP), BF16),
                            pltpu.VMEM((D_FF_EXP, D_MODEL), BF16),
                            pltpu.VMEM((MOE_XSLOTS, TM_MOE) + ROW_TILE, F32),
                            pltpu.SemaphoreType.DMA((MOE_XSLOTS,))]),
        out_shape=jax.ShapeDtypeStruct(xs.shape, F32),
        compiler_params=_cparams(("arbitrary",)),
        name="moe",
    )(tile_expert, n_active, xs, w_eg, w_eu, w_ed)


def _combine_kernel(pos_ref, posn_ref, x1_ref, gates_ref, gf_ref, ys_ref, o_ref, buf_ref, sem):
    step = pl.program_id(0)
    slot = step % 2

    def gather(p_ref, s):
        def body(r, carry):
            _row_copy(ys_ref, p_ref[0, r], buf_ref.at[s, 0], r, sem.at[s]).start(priority=0)
            _row_copy(ys_ref, p_ref[0, TT + r], buf_ref.at[s, 1], r, sem.at[s]).start(priority=1)
            return carry
        lax.fori_loop(0, TT, body, 0)

    @pl.when(step == 0)
    def _():
        gather(pos_ref, 0)

    @pl.when(step < pl.num_programs(0) - 1)
    def _():
        gather(posn_ref, 1 - slot)

    for k in range(2):
        pltpu.make_async_copy(ys_ref.at[pl.ds(0, TT)], buf_ref.at[slot, k], sem.at[slot]).wait()
    gates = gates_ref[...]
    y = (x1_ref[...] + gates[:, 0:1] * _from_token_major(buf_ref[slot, 0])
         + gates[:, 1:2] * _from_token_major(buf_ref[slot, 1]))
    ms = jnp.mean(y * y, axis=-1, keepdims=True)
    o_ref[...] = y * lax.rsqrt(ms + EPS) * gf_ref[...]


def _combine(pos, x1, gates, gf, ys):
    n = x1.shape[0]
    nblk = n // TT
    return pl.pallas_call(
        _combine_kernel,
        grid=(nblk,),
        in_specs=[pl.BlockSpec((None, 1, 2 * TT), lambda i: (i, 0, 0), memory_space=pltpu.SMEM),
                  pl.BlockSpec((None, 1, 2 * TT), lambda i: (jnp.minimum(i + 1, nblk - 1), 0, 0),
                               memory_space=pltpu.SMEM),
                  pl.BlockSpec((TT, D_MODEL), lambda i: (i, 0)),
                  pl.BlockSpec((TT, LANES), lambda i: (i, 0)),
                  pl.BlockSpec((1, D_MODEL), lambda i: (0, 0)),
                  pl.BlockSpec(memory_space=pl.ANY)],
        out_specs=pl.BlockSpec((TT, D_MODEL), lambda i: (i, 0)),
        out_shape=jax.ShapeDtypeStruct((n, D_MODEL), F32),
        scratch_shapes=[pltpu.VMEM((2, 2, TT) + ROW_TILE, F32),
                        pltpu.SemaphoreType.DMA((2,))],
        compiler_params=_cparams(("arbitrary",)),
        name="combine",
    )(pos, pos, x1, gates, gf, ys)


def kernel(x, w_in, conv_w, conv_b, w_mq, w_mk, w_mgate, b_mgate, m_norm_g, m_skip, lambda_qk, da_norm_g,
           rel_bias, w_out, norm1_g, norm2_g, w_rg, b_rg, w_re, b_re, w_eg, w_eu, w_ed, normf_g):
    bsz, seq, _ = x.shape
    n = bsz * seq
    assert seq % TQ == 0 and seq % CHUNK == 0 and n % TM_PROJ == 0 and n % TT == 0 and bsz % MLSTM_BATCH == 0
    l = 0
    x2 = x.reshape(n, D_MODEL)

    w_in_b = w_in[l].astype(BF16)
    wqv_t = jnp.stack([w_in_b[:, 3 * 512:4 * 512].T, w_in_b[:, 5 * 512:6 * 512].T,
                       w_in_b[:, 1 * 512:2 * 512].T])
    c, vmt, z, qt, kd, vt = _inproj(x2, norm1_g[l][None, :], w_in_b, wqv_t, conv_w[l], conv_b[l][None, :], seq)

    wgt = jnp.zeros((2 * SUBLANES, 3 * D_MLSTM), F32)
    wgt = wgt.at[0:ML_HEADS].set(w_mgate[l][:, :ML_HEADS].T)
    wgt = wgt.at[SUBLANES:SUBLANES + ML_HEADS].set(w_mgate[l][:, ML_HEADS:].T).astype(BF16)
    bgt = jnp.zeros((2 * SUBLANES, 1), F32)
    bgt = bgt.at[0:ML_HEADS, 0].set(b_mgate[l][:ML_HEADS])
    bgt = bgt.at[SUBLANES:SUBLANES + ML_HEADS, 0].set(b_mgate[l][ML_HEADS:])
    y_m = _mlstm(c, vmt, z, jnp.swapaxes(w_mq[l], 1, 2).astype(BF16),
                 w_mk[l].astype(BF16), wgt, bgt, m_norm_g[l][None, :], m_skip[l][None, :], bsz, seq)

    bias_diag, bias_corner = _rel_bias_tiles(rel_bias)
    y_d = _attn(lambda_qk[l].astype(F32), qt, kd, vt, bias_diag, bias_corner,
                da_norm_g[l][:, None], bsz, seq)

    wr = jnp.zeros((D_MODEL, LANES), F32)
    wr = wr.at[:, :N_GROUPS].set(w_rg[l])
    wr = wr.at[:, SUBLANES:SUBLANES + N_EXPERTS].set(w_re[l].reshape(D_MODEL, N_EXPERTS))
    wr_hi = wr.astype(BF16)
    wr_lo = (wr - wr_hi.astype(F32)).astype(BF16)
    wr3 = jnp.concatenate([wr_hi, wr_hi, wr_lo], axis=0)
    br = jnp.zeros((1, LANES), F32)
    br = br.at[0, :N_GROUPS].set(b_rg[l])
    br = br.at[0, SUBLANES:SUBLANES + N_EXPERTS].set(b_re[l].reshape(N_EXPERTS))
    x1, h2, meta, gates, counts = _outproj(x2, y_m, y_d, w_out[l].astype(BF16), norm2_g[l][None, :], wr3, br)

    cnt = counts[:, 0]
    tiles_e = (cnt + TM_MOE - 1) // TM_MOE
    ends_t = jnp.cumsum(tiles_e)
    ends = jnp.concatenate([jnp.zeros((1,), jnp.int32), ends_t * TM_MOE]).astype(jnp.int32)
    offs = ends[:-1]
    n_tiles = (2 * n) // TM_MOE + N_EXPERTS
    n_active = ends_t[-1:].astype(jnp.int32)
    tile_ids = jnp.minimum(jnp.arange(n_tiles, dtype=jnp.int32), n_active[0] - 1)
    tile_expert = jnp.sum(tile_ids[:, None] >= ends_t[None, :], axis=1).astype(jnp.int32)
    eids = jnp.arange(N_EXPERTS, dtype=jnp.int32)[:, None]
    pos1 = jnp.sum(jnp.where(meta[0][None, :] == eids, offs[:, None], 0), axis=0) + meta[2]
    pos2 = jnp.sum(jnp.where(meta[1][None, :] == eids, offs[:, None], 0), axis=0) + meta[3]
    pos = jnp.concatenate([pos1.reshape(n // TT, 1, TT), pos2.reshape(n // TT, 1, TT)], axis=2)

    xs = _dispatch(ends, pos, h2, n_tiles * TM_MOE)
    ys = _moe(tile_expert, n_active, xs, w_eg[l], w_eu[l], w_ed[l])
    out = _combine(pos, x1, gates, normf_g[None, :], ys)
    return out.reshape(bsz, seq, D_MODEL)
```
